```python
import math
import jax, jax.numpy as jnp
from jax import lax
import numpy as np

D_MODEL = 1024
BATCH = 2
SEQ = 8192
DEPTH = 1

D_MIX = 2 * D_MODEL
D_SSM = D_MIX // 2
SSM_HEAD_DIM = 64
SSM_HEADS = D_SSM // SSM_HEAD_DIM
SSM_GROUPS = 2
SSM_HEADS_PER_GROUP = SSM_HEADS // SSM_GROUPS
D_STATE = 128
D_CONV = 5
CHUNK = 128
D_XBC = D_SSM + 2 * SSM_GROUPS * D_STATE
D_ATTN = D_MIX - D_SSM
ATTN_HEADS = 8
ATTN_HEAD_DIM = D_ATTN // ATTN_HEADS // 2
ATTN_V_DIM = 2 * ATTN_HEAD_DIM
Q_BLOCK = 128
D_IN_PROJ = D_SSM + D_XBC + 2 * SSM_HEADS + 4 * D_ATTN
ALPHA = (2.0 * DEPTH) ** 0.25
BETA = (8.0 * DEPTH) ** -0.25
LN_EPS = 1e-5
RMS_EPS = 1e-5

kernel_name = "hybrid_ssd_diffattn_deepnorm_encoder"


def layer_norm(x, g, b):
    xf = x.astype(jnp.float32)
    mu = jnp.mean(xf, axis=-1, keepdims=True)
    var = jnp.mean(jnp.square(xf - mu), axis=-1, keepdims=True)
    return ((xf - mu) * lax.rsqrt(var + LN_EPS) * g + b).astype(x.dtype)


def rms_norm(x, g):
    xf = x.astype(jnp.float32)
    ms = jnp.mean(jnp.square(xf), axis=-1, keepdims=True)
    return (xf * lax.rsqrt(ms + RMS_EPS) * g).astype(x.dtype)


def centred_dwconv(u, w, b):
    out = lax.conv_general_dilated(
        u, w[:, None, :].astype(u.dtype), window_strides=(1,),
        padding=[(D_CONV // 2, D_CONV // 2)],
        dimension_numbers=("NWC", "WIO", "NWC"),
        feature_group_count=u.shape[-1])
    return out + b.astype(u.dtype)


def ssd_chunked(x, dt, A, Bm, Cm):
    b, s, g, j, p = x.shape
    n = Bm.shape[-1]
    nc = s // CHUNK
    x = x.reshape(b, nc, CHUNK, g, j, p)
    dt = dt.reshape(b, nc, CHUNK, g, j)
    Bm = Bm.reshape(b, nc, CHUNK, g, n)
    Cm = Cm.reshape(b, nc, CHUNK, g, n)
    a_cs = jnp.cumsum(dt * A, axis=2)
    xdt = x * dt[..., None]
    lower = jnp.tril(jnp.ones((CHUNK, CHUNK), dtype=bool))[None, None, :, :, None, None]
    seg = a_cs[:, :, :, None] - a_cs[:, :, None, :]
    decay = jnp.exp(jnp.where(lower, seg, -jnp.inf))
    cb = jnp.einsum("bclgn,bcsgn->bclsg", Cm, Bm)
    y_diag = jnp.einsum("bclsgj,bcsgjp->bclgjp", cb[..., None] * decay, xdt)
    decay_to_end = jnp.exp(a_cs[:, :, -1:] - a_cs)
    chunk_states = jnp.einsum("bclgn,bclgjp->bcgjpn", Bm, xdt * decay_to_end[..., None])
    chunk_decay = jnp.exp(a_cs[:, :, -1])

    def step(h, inp):
        st, dec = inp
        return dec[..., None, None] * h + st, h

    h0 = jnp.zeros((b, g, j, p, n), x.dtype)
    _, prev = lax.scan(step, h0, (jnp.moveaxis(chunk_states, 1, 0),
                                  jnp.moveaxis(chunk_decay, 1, 0)))
    prev = jnp.moveaxis(prev, 0, 1)
    y_off = jnp.einsum("bclgn,bcgjpn->bclgjp", Cm, prev) * jnp.exp(a_cs)[..., None]
    return (y_diag + y_off).reshape(b, s, g, j, p)


def bidirectional_ssd(xs, dt_f, dt_b, Bm, Cm, A_log_f, A_log_b, dt_bias_f, dt_bias_b, D):
    b, s = xs.shape[:2]
    shp = (SSM_GROUPS, SSM_HEADS_PER_GROUP)
    A_f = -jnp.exp(A_log_f.astype(jnp.float32)).reshape(shp)
    A_b = -jnp.exp(A_log_b.astype(jnp.float32)).reshape(shp)
    dtf = jax.nn.softplus(dt_f.astype(jnp.float32) + dt_bias_f.astype(jnp.float32)).reshape(b, s, *shp)
    dtb = jax.nn.softplus(dt_b.astype(jnp.float32) + dt_bias_b.astype(jnp.float32)).reshape(b, s, *shp)
    xf, Bf, Cf = xs.astype(jnp.float32), Bm.astype(jnp.float32), Cm.astype(jnp.float32)
    flip = lambda t: jnp.flip(t, axis=1)
    y_fwd = ssd_chunked(xf, dtf, A_f, Bf, Cf)
    y_bwd = flip(ssd_chunked(flip(xf), flip(dtb), A_b, flip(Bf), flip(Cf)))
    y = y_fwd + y_bwd + D.astype(jnp.float32).reshape(*shp, 1) * xf
    return y.astype(xs.dtype)


def diff_attention(q, k, v, lam, slopes):
    b, s, h, _, d = q.shape
    nb = s // Q_BLOCK
    scale = ATTN_HEAD_DIM ** -0.5
    k_pos = jnp.arange(s)
    qb = jnp.moveaxis(q.reshape(b, nb, Q_BLOCK, h, 2, d), 1, 0)
    starts = jnp.arange(nb) * Q_BLOCK

    def block(args):
        qi, start = args
        scores = jnp.einsum("bqhrd,bshrd->bhrqs", qi, k).astype(jnp.float32) * scale
        dist = jnp.abs((start + jnp.arange(Q_BLOCK))[:, None] - k_pos[None, :]).astype(jnp.float32)
        scores = scores - slopes[None, :, None, None, None] * dist[None, None, None]
        probs = jax.nn.softmax(scores, axis=-1)
        w = probs[:, :, 0] - lam * probs[:, :, 1]
        return jnp.einsum("bhqs,bshe->bqhe", w.astype(v.dtype), v)

    o = lax.map(block, (qb, starts))
    return jnp.moveaxis(o, 0, 1).reshape(b, s, h, v.shape[-1])


def hybrid_layer(h, layer_idx, w_in, conv_w, conv_b, A_log_f, A_log_b, dt_bias_f, dt_bias_b,
                 D, ssm_norm_g, lq1, lk1, lq2, lk2, subln_g, w_out, ln_g, ln_b):
    b, s, _ = h.shape
    proj = jnp.einsum("bsd,de->bse", h, w_in.astype(h.dtype))
    offs = np.cumsum([D_SSM, D_XBC, SSM_HEADS, SSM_HEADS, D_ATTN, D_ATTN, D_ATTN, D_ATTN])[:-1].tolist()
    z, xbc, dt_f, dt_b, q, k, v, g = jnp.split(proj, offs, axis=-1)

    xbc = jax.nn.silu(centred_dwconv(xbc, conv_w, conv_b))
    xs, Bm, Cm = jnp.split(xbc, [D_SSM, D_SSM + SSM_GROUPS * D_STATE], axis=-1)
    xs = xs.reshape(b, s, SSM_GROUPS, SSM_HEADS_PER_GROUP, SSM_HEAD_DIM)
    Bm = Bm.reshape(b, s, SSM_GROUPS, D_STATE)
    Cm = Cm.reshape(b, s, SSM_GROUPS, D_STATE)
    y = bidirectional_ssd(xs, dt_f, dt_b, Bm, Cm, A_log_f, A_log_b, dt_bias_f, dt_bias_b, D)
    y = y.reshape(b, s, SSM_GROUPS, D_SSM // SSM_GROUPS)
    zg = jax.nn.silu(z).reshape(b, s, SSM_GROUPS, D_SSM // SSM_GROUPS)
    y_ssm = rms_norm(y * zg, ssm_norm_g.reshape(SSM_GROUPS, -1)).reshape(b, s, D_SSM)

    q = q.reshape(b, s, ATTN_HEADS, 2, ATTN_HEAD_DIM)
    k = k.reshape(b, s, ATTN_HEADS, 2, ATTN_HEAD_DIM)
    v = v.reshape(b, s, ATTN_HEADS, ATTN_V_DIM)
    lam_init = 0.8 - 0.6 * math.exp(-0.3 * layer_idx)
    lam = (jnp.exp(jnp.sum(lq1.astype(jnp.float32) * lk1.astype(jnp.float32)))
           - jnp.exp(jnp.sum(lq2.astype(jnp.float32) * lk2.astype(jnp.float32))) + lam_init)
    slopes = jnp.exp2(-8.0 * (jnp.arange(ATTN_HEADS, dtype=jnp.float32) + 1.0) / ATTN_HEADS)
    o = diff_attention(q, k, v, lam, slopes)
    o = rms_norm(o, subln_g) * (1.0 - lam_init)
    y_attn = o.reshape(b, s, D_ATTN) * jax.nn.silu(g)

    mix = jnp.einsum("bse,ed->bsd", jnp.concatenate([y_ssm, y_attn], axis=-1), w_out.astype(h.dtype))
    return layer_norm(ALPHA * h + mix, ln_g, ln_b)


def setup_inputs(seed: int = 0) -> dict:
    key = jax.random.key(seed)
    ks = jax.random.split(key, 24)
    f32 = jnp.float32
    x = jax.random.normal(ks[0], (BATCH, SEQ, D_MODEL), f32)
    ln_emb_g = 1.0 + 0.02 * jax.random.normal(ks[1], (D_MODEL,), f32)
    ln_emb_b = 0.02 * jax.random.normal(ks[2], (D_MODEL,), f32)
    x_start = D_SSM
    v_start = D_SSM + D_XBC + 2 * SSM_HEADS + 2 * D_ATTN
    col_scale = (jnp.ones((D_IN_PROJ,), f32)
                 .at[x_start:x_start + D_SSM].set(BETA)
                 .at[v_start:v_start + D_ATTN].set(BETA))
    w_in = jax.random.normal(ks[3], (DEPTH, D_MODEL, D_IN_PROJ), f32) * (D_MODEL ** -0.5) * col_scale
    conv_w = jax.random.normal(ks[4], (DEPTH, D_CONV, D_XBC), f32) * (D_CONV ** -0.5)
    conv_b = 0.01 * jax.random.normal(ks[5], (DEPTH, D_XBC), f32)
    A_log_fwd = jnp.log(jax.random.uniform(ks[6], (DEPTH, SSM_HEADS), f32, 1.0, 16.0))
    A_log_bwd = jnp.log(jax.random.uniform(ks[7], (DEPTH, SSM_HEADS), f32, 1.0, 16.0))
    dt_f0 = jnp.exp(jax.random.uniform(ks[8], (DEPTH, SSM_HEADS), f32, math.log(1e-3), math.log(1e-1)))
    dt_b0 = jnp.exp(jax.random.uniform(ks[9], (DEPTH, SSM_HEADS), f32, math.log(1e-3), math.log(1e-1)))
    dt_bias_fwd = dt_f0 + jnp.log(-jnp.expm1(-dt_f0))
    dt_bias_bwd = dt_b0 + jnp.log(-jnp.expm1(-dt_b0))
    D_skip = 1.0 + 0.1 * jax.random.normal(ks[10], (DEPTH, SSM_HEADS), f32)
    ssm_norm_g = 1.0 + 0.02 * jax.random.normal(ks[11], (DEPTH, D_SSM), f32)
    lambda_q1 = 0.1 * jax.random.normal(ks[12], (DEPTH, ATTN_HEAD_DIM), f32)
    lambda_k1 = 0.1 * jax.random.normal(ks[13], (DEPTH, ATTN_HEAD_DIM), f32)
    lambda_q2 = 0.1 * jax.random.normal(ks[14], (DEPTH, ATTN_HEAD_DIM), f32)
    lambda_k2 = 0.1 * jax.random.normal(ks[15], (DEPTH, ATTN_HEAD_DIM), f32)
    subln_g = 1.0 + 0.02 * jax.random.normal(ks[16], (DEPTH, ATTN_V_DIM), f32)
    w_out = jax.random.normal(ks[17], (DEPTH, D_MIX, D_MODEL), f32) * (D_MIX ** -0.5) * BETA
    ln_g = 1.0 + 0.02 * jax.random.normal(ks[18], (DEPTH, D_MODEL), f32)
    ln_b = 0.02 * jax.random.normal(ks[19], (DEPTH, D_MODEL), f32)
    return {"x": x, "ln_emb_g": ln_emb_g, "ln_emb_b": ln_emb_b, "w_in": w_in,
            "conv_w": conv_w, "conv_b": conv_b, "A_log_fwd": A_log_fwd, "A_log_bwd": A_log_bwd,
            "dt_bias_fwd": dt_bias_fwd, "dt_bias_bwd": dt_bias_bwd, "D_skip": D_skip,
            "ssm_norm_g": ssm_norm_g, "lambda_q1": lambda_q1, "lambda_k1": lambda_k1,
            "lambda_q2": lambda_q2, "lambda_k2": lambda_k2, "subln_g": subln_g,
            "w_out": w_out, "ln_g": ln_g, "ln_b": ln_b}


def reference(x, ln_emb_g, ln_emb_b, w_in, conv_w, conv_b, A_log_fwd, A_log_bwd,
              dt_bias_fwd, dt_bias_bwd, D_skip, ssm_norm_g, lambda_q1, lambda_k1,
              lambda_q2, lambda_k2, subln_g, w_out, ln_g, ln_b):
    h = layer_norm(x, ln_emb_g, ln_emb_b)
    for l in range(DEPTH):
        h = hybrid_layer(h, l, w_in[l], conv_w[l], conv_b[l], A_log_fwd[l], A_log_bwd[l],
                         dt_bias_fwd[l], dt_bias_bwd[l], D_skip[l], ssm_norm_g[l],
                         lambda_q1[l], lambda_k1[l], lambda_q2[l], lambda_k2[l],
                         subln_g[l], w_out[l], ln_g[l], ln_b[l])
    return h
```

```python
import functools
import math

import jax
import jax.numpy as jnp
from jax import lax
from jax.experimental import pallas as pl
from jax.experimental.pallas import tpu as pltpu

D_MODEL = 1024
D_SSM = 1024
SSM_HEAD_DIM = 64
SSM_HEADS = 16
SSM_GROUPS = 2
HEADS_PER_GROUP = SSM_HEADS // SSM_GROUPS
D_STATE = 128
D_CONV = 5
CHUNK = 128
D_XBC = D_SSM + 2 * SSM_GROUPS * D_STATE
D_ATTN = 1024
ATTN_HEADS = 8
ATTN_HEAD_DIM = 64
ATTN_V_DIM = 128
GROUP_WIDTH = D_SSM // SSM_GROUPS
DEPTH = 1
ALPHA = (2.0 * DEPTH) ** 0.25
LN_EPS = 1e-5
RMS_EPS = 1e-5
LAM_INIT = 0.8 - 0.6 * math.exp(-0.3 * 0)

LANES = 128
VMEM_LIMIT = 48 * 1024 * 1024

F32 = jnp.float32
BF16 = jnp.bfloat16


def _dot(a, b):
    return jnp.dot(a, b, preferred_element_type=F32)


def _layer_norm(x, g, b):
    mu = jnp.mean(x, axis=-1, keepdims=True)
    xc = x - mu
    var = jnp.mean(xc * xc, axis=-1, keepdims=True)
    return xc * lax.rsqrt(var + LN_EPS) * g + b


def _silu(x):
    return x / (1.0 + jnp.exp(-x))


def _softplus(x):
    return jnp.maximum(x, 0.0) + jnp.log1p(jnp.exp(-jnp.abs(x)))


def _split3(x):
    hi = x.astype(BF16)
    r = x - hi.astype(F32)
    mid = r.astype(BF16)
    lo = (r - mid.astype(F32)).astype(BF16)
    return hi, mid, lo


def _dot_exact_rhs(m, x):
    hi, mid, lo = _split3(x)
    return _dot(m, hi) + _dot(m, mid) + _dot(m, lo)


def _dot_exact_lhs(x, m):
    hi, mid, lo = _split3(x)
    return _dot(hi, m) + _dot(mid, m) + _dot(lo, m)


def _const_spec(shape):
    nd = len(shape)
    return pl.BlockSpec(shape, lambda *_: (0,) * nd)


def _ln_inproj_kernel(x_ref, g_ref, b_ref, wz_ref, wxbc_ref, wdt_ref, wq_ref, wk_ref, wv_ref, wg_ref,
                      z_ref, xbc_ref, dt_ref, q_ref, k_ref, v_ref, gate_ref):
    h = _layer_norm(x_ref[...], g_ref[...], b_ref[...]).astype(BF16)
    z_ref[...] = _dot(h, wz_ref[...])
    xbc_ref[...] = _dot(h, wxbc_ref[...])
    dt_ref[...] = _dot(h, wdt_ref[...])
    q_ref[...] = (_dot(h, wq_ref[...]) * (ATTN_HEAD_DIM ** -0.5)).astype(BF16)
    k_ref[...] = _dot(h, wk_ref[...]).astype(BF16)
    v_ref[...] = _dot(h, wv_ref[...]).astype(BF16)
    gate_ref[...] = _dot(h, wg_ref[...])


def _ln_inproj(x2d, ln_g, ln_b, w_in, tm=512):
    t = x2d.shape[0]
    offs = [0, D_SSM, D_SSM + D_XBC, D_SSM + D_XBC + 2 * SSM_HEADS]
    wz = w_in[:, offs[0]:offs[1]].astype(BF16)
    wxbc = w_in[:, offs[1]:offs[2]].astype(BF16)
    wdt = jnp.pad(w_in[:, offs[2]:offs[3]], ((0, 0), (0, LANES - 2 * SSM_HEADS))).astype(BF16)
    o = offs[3]
    wq, wk, wv, wg = (w_in[:, o + i * D_ATTN:o + (i + 1) * D_ATTN].astype(BF16) for i in range(4))
    row = lambda n: pl.BlockSpec((tm, n), lambda i: (i, 0))
    weights = [wz, wxbc, wdt, wq, wk, wv, wg]
    out_cols = [(D_SSM, F32), (D_XBC, F32), (LANES, F32), (D_ATTN, BF16), (D_ATTN, BF16), (D_ATTN, BF16),
                (D_ATTN, F32)]
    return pl.pallas_call(
        _ln_inproj_kernel,
        grid=(t // tm,),
        in_specs=[row(D_MODEL), _const_spec((1, D_MODEL)), _const_spec((1, D_MODEL))]
                 + [pl.BlockSpec(w.shape, lambda i: (0, 0), pipeline_mode=pl.Buffered(1)) for w in weights],
        out_specs=[row(n) for n, _ in out_cols],
        out_shape=[jax.ShapeDtypeStruct((t, n), dt) for n, dt in out_cols],
        compiler_params=pltpu.CompilerParams(dimension_semantics=("parallel",), vmem_limit_bytes=VMEM_LIMIT),
        name="ln_inproj",
    )(x2d, ln_g.reshape(1, -1), ln_b.reshape(1, -1), *weights)


HALO = 8


def _conv_silu_kernel(prev_ref, cur_ref, next_ref, w_ref, b_ref, xs_ref, bt_ref, c_ref, ext_ref):
    i = pl.program_id(1)
    n = pl.num_programs(1)
    tc = cur_ref.shape[0]
    ext_ref[0:HALO, :] = jnp.where(i == 0, 0.0, prev_ref[...])
    ext_ref[HALO:HALO + tc, :] = cur_ref[...]
    ext_ref[HALO + tc:, :] = jnp.where(i == n - 1, 0.0, next_ref[...])
    acc = jnp.zeros((tc, D_XBC), F32) + b_ref[...]
    for kk in range(D_CONV):
        acc = acc + ext_ref[pl.ds(HALO - D_CONV // 2 + kk, tc), :] * w_ref[kk:kk + 1, :]
    act = _silu(acc)
    xs_ref[...] = act[:, :D_SSM]
    for g in range(SSM_GROUPS):
        lo = D_SSM + g * D_STATE
        bt_ref[g] = act[:, lo:lo + D_STATE].T.astype(BF16)
    c_ref[...] = act[:, D_SSM + SSM_GROUPS * D_STATE:].astype(BF16)


def _conv_silu(xbc, conv_w, conv_b, tc=512):
    b, s, _ = xbc.shape
    nb = s // tc
    hb = tc // HALO
    return pl.pallas_call(
        _conv_silu_kernel,
        grid=(b, nb),
        in_specs=[
            pl.BlockSpec((None, HALO, D_XBC), lambda bi, i: (bi, jnp.maximum(i * hb - 1, 0), 0)),
            pl.BlockSpec((None, tc, D_XBC), lambda bi, i: (bi, i, 0)),
            pl.BlockSpec((None, HALO, D_XBC), lambda bi, i: (bi, jnp.minimum((i + 1) * hb, s // HALO - 1), 0)),
            _const_spec((D_CONV, D_XBC)),
            _const_spec((1, D_XBC)),
        ],
        out_specs=[
            pl.BlockSpec((None, tc, D_SSM), lambda bi, i: (bi, i, 0)),
            pl.BlockSpec((None, SSM_GROUPS, D_STATE, tc), lambda bi, i: (bi, 0, 0, i)),
            pl.BlockSpec((None, tc, SSM_GROUPS * D_STATE), lambda bi, i: (bi, i, 0)),
        ],
        out_shape=[
            jax.ShapeDtypeStruct((b, s, D_SSM), F32),
            jax.ShapeDtypeStruct((b, SSM_GROUPS, D_STATE, s), BF16),
            jax.ShapeDtypeStruct((b, s, SSM_GROUPS * D_STATE), BF16),
        ],
        scratch_shapes=[pltpu.VMEM((tc + 2 * HALO, D_XBC), F32)],
        compiler_params=pltpu.CompilerParams(dimension_semantics=("parallel", "parallel"),
                                             vmem_limit_bytes=VMEM_LIMIT),
        name="conv_silu",
    )(xbc, xbc, xbc, conv_w, conv_b.reshape(1, -1))


def _ssd_chunk(rev, x, bt, cm, dt_raw, dtT_raw, alog_row, alog_col, bias_row, bias_col, state):
    L = CHUNK
    r = lax.broadcasted_iota(jnp.int32, (L, L), 0)
    c = lax.broadcasted_iota(jnp.int32, (L, L), 1)
    keep = (c >= r) if rev else (c <= r)
    cum_l = keep.astype(BF16)
    cum_r = ((r >= c) if rev else (r <= c)).astype(BF16)
    eh = lax.broadcasted_iota(jnp.int32, (HEADS_PER_GROUP, GROUP_WIDTH), 0)
    el = lax.broadcasted_iota(jnp.int32, (HEADS_PER_GROUP, GROUP_WIDTH), 1)
    expand = (el // SSM_HEAD_DIM == eh).astype(BF16)

    dt = _softplus(dt_raw + bias_row)
    a = dt * (-jnp.exp(alog_row))
    dt_t = _softplus(dtT_raw + bias_col)
    a_t = dt_t * (-jnp.exp(alog_col))
    acs = _dot_exact_rhs(cum_l, a)
    acs_t = _dot_exact_lhs(a_t, cum_r)
    dt_x = _dot_exact_lhs(dt, expand)
    acs_x = _dot_exact_lhs(acs, expand)
    edge = 0 if rev else L - 1
    tot_x = acs_x[edge:edge + 1, :]

    xdt = x * dt_x
    y = _dot(cm, state.astype(BF16)) * jnp.exp(acs_x)
    cb = _dot(cm, bt)
    lane = lax.broadcasted_iota(jnp.int32, (L, LANES), 1)
    tiles = []
    for t in range(GROUP_WIDTH // LANES):
        xt = xdt[:, t * LANES:(t + 1) * LANES]
        acc = None
        for half in range(LANES // SSM_HEAD_DIM):
            j = t * (LANES // SSM_HEAD_DIM) + half
            seg = acs[:, j:j + 1] - acs_t[j:j + 1, :]
            gmat = (cb * jnp.exp(jnp.where(keep, seg, -jnp.inf))).astype(BF16)
            in_head = (lane >= half * SSM_HEAD_DIM) & (lane < (half + 1) * SSM_HEAD_DIM)
            part = _dot(gmat, jnp.where(in_head, xt, 0.0).astype(BF16))
            acc = part if acc is None else acc + part
        tiles.append(acc)
    y = y + jnp.concatenate(tiles, axis=1)

    w = (xdt * jnp.exp(tot_x - acs_x)).astype(BF16)
    new_state = jnp.exp(tot_x) * state + _dot(bt, w)
    return y, new_state


def _ssd_kernel(rev, nchunk, *refs):
    if rev:
        (xs_ref, bt_ref, c_ref, dt_ref, dtT_ref, alr_ref, alc_ref, br_ref, bc_ref,
         yf_ref, z_ref, ng_ref, y_ref, state_ref) = refs
    else:
        (xs_ref, bt_ref, c_ref, dt_ref, dtT_ref, alr_ref, alc_ref, br_ref, bc_ref,
         dx_ref, y_ref, state_ref) = refs

    @pl.when(pl.program_id(2) == 0)
    def _():
        state_ref[...] = jnp.zeros_like(state_ref)

    order = range(nchunk - 1, -1, -1) if rev else range(nchunk)
    for ci in order:
        rows = slice(ci * CHUNK, (ci + 1) * CHUNK)
        x = xs_ref[rows, :]
        y, new_state = _ssd_chunk(rev, x, bt_ref[:, rows], c_ref[rows, :], dt_ref[rows, :], dtT_ref[:, rows],
                                  alr_ref[...], alc_ref[...], br_ref[...], bc_ref[...], state_ref[...])
        state_ref[...] = new_state
        if rev:
            yy = (yf_ref[rows, :] + y) * _silu(z_ref[rows, :])
            ms = jnp.mean(yy * yy, axis=-1, keepdims=True)
            y_ref[rows, :] = yy * lax.rsqrt(ms + RMS_EPS) * ng_ref[...]
        else:
            y_ref[rows, :] = y + dx_ref[...] * x


def _ssd_pass(rev, xs, bt, cm, dt, dtT, a_log, dt_bias, extra, nchunk=4):
    b, s, _ = xs.shape
    tcs = nchunk * CHUNK
    nblk = s // tcs
    blk = (lambda i: nblk - 1 - i) if rev else (lambda i: i)
    gw, hg = GROUP_WIDTH, HEADS_PER_GROUP
    seq_spec = lambda w: pl.BlockSpec((None, tcs, w), lambda bi, g, i: (bi, blk(i), g))
    in_specs = [
        seq_spec(gw),
        pl.BlockSpec((None, None, D_STATE, tcs), lambda bi, g, i: (bi, g, 0, blk(i))),
        seq_spec(D_STATE),
        pl.BlockSpec((None, None, tcs, hg), lambda bi, g, i: (bi, g, blk(i), 0)),
        pl.BlockSpec((None, None, hg, tcs), lambda bi, g, i: (bi, g, 0, blk(i))),
        pl.BlockSpec((None, 1, hg), lambda bi, g, i: (g, 0, 0)),
        pl.BlockSpec((None, hg, 1), lambda bi, g, i: (g, 0, 0)),
        pl.BlockSpec((None, 1, hg), lambda bi, g, i: (g, 0, 0)),
        pl.BlockSpec((None, hg, 1), lambda bi, g, i: (g, 0, 0)),
    ]
    args = [xs, bt, cm, dt, dtT,
            a_log.reshape(SSM_GROUPS, 1, hg), a_log.reshape(SSM_GROUPS, hg, 1),
            dt_bias.reshape(SSM_GROUPS, 1, hg), dt_bias.reshape(SSM_GROUPS, hg, 1)]
    group_row = pl.BlockSpec((None, 1, gw), lambda bi, g, i: (g, 0, 0))
    if rev:
        y_fwd, z, norm_g = extra
        in_specs += [seq_spec(gw), seq_spec(gw), group_row]
        args += [y_fwd, z, norm_g.reshape(SSM_GROUPS, 1, gw)]
    else:
        (d_skip,) = extra
        in_specs += [group_row]
        args += [jnp.repeat(d_skip, SSM_HEAD_DIM).reshape(SSM_GROUPS, 1, gw)]
    return pl.pallas_call(
        functools.partial(_ssd_kernel, rev, nchunk),
        grid=(b, SSM_GROUPS, nblk),
        in_specs=in_specs,
        out_specs=seq_spec(gw),
        out_shape=jax.ShapeDtypeStruct((b, s, D_SSM), F32),
        scratch_shapes=[pltpu.VMEM((D_STATE, gw), F32)],
        compiler_params=pltpu.CompilerParams(dimension_semantics=("parallel", "parallel", "arbitrary"),
                                             vmem_limit_bytes=VMEM_LIMIT),
        name="ssd_bwd" if rev else "ssd_fwd",
    )(*args)


NEG_BIG = -1e30


def _attn_kernel(tq, tk, nk, slopes_ref, qT_ref, k_ref, vT_ref, lq1_ref, lk1_ref, lq2_ref, lk2_ref, sg_ref,
                 o_ref):
    slope = slopes_ref[pl.program_id(1)]
    i0 = pl.program_id(2) * tq
    ii = lax.broadcasted_iota(jnp.int32, (tk, tq), 1)
    jj = lax.broadcasted_iota(jnp.int32, (tk, tq), 0)
    rel = (ii - jj).astype(F32)

    def body(j, carry):
        off = (i0 - j * tk).astype(F32)
        bias = slope * jnp.abs(rel + off)
        out = []
        for r in range(2):
            m_old, l_old, acc_old = carry[3 * r:3 * r + 3]
            s = _dot(k_ref[r, j], qT_ref[r]) - bias
            m_new = jnp.maximum(m_old, jnp.max(s, axis=0, keepdims=True))
            p = jnp.exp(s - m_new)
            alpha = jnp.exp(m_old - m_new)
            l_new = alpha * l_old + jnp.sum(p, axis=0, keepdims=True)
            acc_new = alpha * acc_old + _dot(vT_ref[j], p.astype(BF16))
            out += [m_new, l_new, acc_new]
        return tuple(out)

    init = (jnp.full((1, tq), NEG_BIG, F32), jnp.zeros((1, tq), F32), jnp.zeros((ATTN_V_DIM, tq), F32)) * 2
    m1, l1, acc1, m2, l2, acc2 = lax.fori_loop(0, nk, body, init)

    lam = (jnp.exp(jnp.sum(lq1_ref[...] * lk1_ref[...], axis=-1, keepdims=True))
           - jnp.exp(jnp.sum(lq2_ref[...] * lk2_ref[...], axis=-1, keepdims=True)) + LAM_INIT)
    o = acc1 / l1 - lam * (acc2 / l2)
    ms = jnp.mean(o * o, axis=0, keepdims=True)
    o_ref[...] = o * lax.rsqrt(ms + RMS_EPS) * sg_ref[...] * (1.0 - LAM_INIT)


def _diff_attn(qT, k6, vT5, lq1, lk1, lq2, lk2, subln_g, tq=256):
    b, h, _, d, s = qT.shape
    nk, tk = k6.shape[3], k6.shape[4]
    slopes = jnp.exp2(-8.0 * (jnp.arange(ATTN_HEADS, dtype=F32) + 1.0) / ATTN_HEADS)
    vec = lambda a: a.reshape(1, -1)
    return pl.pallas_call(
        functools.partial(_attn_kernel, tq, tk, nk),
        grid=(b, h, s // tq),
        in_specs=[
            pl.BlockSpec(memory_space=pltpu.SMEM),
            pl.BlockSpec((None, None, 2, d, tq), lambda bi, hi, qi: (bi, hi, 0, 0, qi)),
            pl.BlockSpec((None, None, 2, nk, tk, d), lambda bi, hi, qi: (bi, hi, 0, 0, 0, 0)),
            pl.BlockSpec((None, None, nk, ATTN_V_DIM, tk), lambda bi, hi, qi: (bi, hi, 0, 0, 0)),
            _const_spec((1, d)), _const_spec((1, d)), _const_spec((1, d)), _const_spec((1, d)),
            _const_spec((ATTN_V_DIM, 1)),
        ],
        out_specs=pl.BlockSpec((None, None, ATTN_V_DIM, tq), lambda bi, hi, qi: (bi, hi, 0, qi)),
        out_shape=jax.ShapeDtypeStruct((b, h, ATTN_V_DIM, s), F32),
        compiler_params=pltpu.CompilerParams(dimension_semantics=("parallel", "parallel", "arbitrary"),
                                             vmem_limit_bytes=VMEM_LIMIT),
        name="diff_attn",
    )(slopes, qT, k6, vT5, vec(lq1), vec(lk1), vec(lq2), vec(lk2), subln_g.reshape(-1, 1))


def _out_proj_kernel(x_ref, eg_ref, eb_ref, ys_ref, o_ref, gate_ref, w_ref, g_ref, b_ref, out_ref):
    h = _layer_norm(x_ref[...], eg_ref[...], eb_ref[...])
    ya = o_ref[...] * _silu(gate_ref[...])
    mix = _dot(ys_ref[...].astype(BF16), w_ref[:D_SSM, :]) + _dot(ya.astype(BF16), w_ref[D_SSM:, :])
    out_ref[...] = _layer_norm(ALPHA * h + mix, g_ref[...], b_ref[...])


def _out_proj(x2d, ln_emb_g, ln_emb_b, y_ssm, o, gate, w_out, ln_g, ln_b, tm=512):
    t = x2d.shape[0]
    row = lambda n: pl.BlockSpec((tm, n), lambda i: (i, 0))
    vec = lambda a: a.reshape(1, -1)
    return pl.pallas_call(
        _out_proj_kernel,
        grid=(t // tm,),
        in_specs=[row(D_MODEL), _const_spec((1, D_MODEL)), _const_spec((1, D_MODEL)),
                  row(D_SSM), row(D_ATTN), row(D_ATTN),
                  _const_spec((D_SSM + D_ATTN, D_MODEL)), _const_spec((1, D_MODEL)), _const_spec((1, D_MODEL))],
        out_specs=row(D_MODEL),
        out_shape=jax.ShapeDtypeStruct((t, D_MODEL), F32),
        compiler_params=pltpu.CompilerParams(dimension_semantics=("parallel",), vmem_limit_bytes=VMEM_LIMIT),
        name="out_proj",
    )(x2d, vec(ln_emb_g), vec(ln_emb_b), y_ssm, o, gate, w_out.astype(BF16), vec(ln_g), vec(ln_b))


def kernel(x, ln_emb_g, ln_emb_b, w_in, conv_w, conv_b, A_log_fwd, A_log_bwd, dt_bias_fwd, dt_bias_bwd, D_skip,
           ssm_norm_g, lambda_q1, lambda_k1, lambda_q2, lambda_k2, subln_g, w_out, ln_g, ln_b):
    b, s, _ = x.shape
    t = b * s
    x2d = x.reshape(t, D_MODEL)
    z, xbc, dt_all, q, k, v, gate = _ln_inproj(x2d, ln_emb_g, ln_emb_b, w_in[0])

    xs, bt, cm = _conv_silu(xbc.reshape(b, s, D_XBC), conv_w[0], conv_b[0])
    hg = HEADS_PER_GROUP

    def dt_layouts(cols):
        d = cols.reshape(b, s, SSM_GROUPS, hg)
        return jnp.transpose(d, (0, 2, 1, 3)), jnp.transpose(d, (0, 2, 3, 1))

    dt_f, dtT_f = dt_layouts(dt_all[:, :SSM_HEADS])
    dt_b, dtT_b = dt_layouts(dt_all[:, SSM_HEADS:2 * SSM_HEADS])
    y_fwd = _ssd_pass(False, xs, bt, cm, dt_f, dtT_f, A_log_fwd[0], dt_bias_fwd[0], (D_skip[0],))
    y_ssm = _ssd_pass(True, xs, bt, cm, dt_b, dtT_b, A_log_bwd[0], dt_bias_bwd[0],
                      (y_fwd, z.reshape(b, s, D_SSM), ssm_norm_g[0]))

    tk = 512
    nk = s // tk
    qT = jnp.transpose(q.reshape(b, s, ATTN_HEADS, 2, ATTN_HEAD_DIM), (0, 2, 3, 4, 1))
    k6 = jnp.transpose(k.reshape(b, nk, tk, ATTN_HEADS, 2, ATTN_HEAD_DIM), (0, 3, 4, 1, 2, 5))
    vT5 = jnp.transpose(v.reshape(b, nk, tk, ATTN_HEADS, ATTN_V_DIM), (0, 3, 1, 4, 2))
    oT = _diff_attn(qT, k6, vT5, lambda_q1[0], lambda_k1[0], lambda_q2[0], lambda_k2[0], subln_g[0])
    o = jnp.transpose(oT, (0, 3, 1, 2)).reshape(t, D_ATTN)

    out = _out_proj(x2d, ln_emb_g, ln_emb_b, y_ssm.reshape(t, D_SSM), o, gate, w_out[0], ln_g[0], ln_b[0])
    return out.reshape(b, s, D_MODEL)
```

```python
import functools
import math

import jax
import jax.numpy as jnp
from jax import lax
from jax.experimental import pallas as pl
from jax.experimental.pallas import tpu as pltpu

D_MODEL = 1024
D_SSM = 1024
SSM_HEAD_DIM = 64
SSM_HEADS = 16
SSM_GROUPS = 2
HEADS_PER_GROUP = SSM_HEADS // SSM_GROUPS
D_STATE = 128
D_CONV = 5
CHUNK = 128
D_XBC = D_SSM + 2 * SSM_GROUPS * D_STATE
D_ATTN = 1024
ATTN_HEADS = 8
ATTN_HEAD_DIM = 64
ATTN_V_DIM = 128
GROUP_WIDTH = D_SSM // SSM_GROUPS
DEPTH = 1
ALPHA = (2.0 * DEPTH) ** 0.25
LN_EPS = 1e-5
RMS_EPS = 1e-5
LAM_INIT = 0.8 - 0.6 * math.exp(-0.3 * 0)

LANES = 128
VMEM_LIMIT = 48 * 1024 * 1024

F32 = jnp.float32
BF16 = jnp.bfloat16


LOG2E = math.log2(math.e)


def _dot(a, b):
    return jnp.dot(a, b, preferred_element_type=F32)


def _dot_nt(a, b):
    return lax.dot_general(a, b, (((1,), (1,)), ((), ())), preferred_element_type=F32)


def _layer_norm(x, g, b):
    mu = jnp.mean(x, axis=-1, keepdims=True)
    xc = x - mu
    var = jnp.mean(xc * xc, axis=-1, keepdims=True)
    return xc * lax.rsqrt(var + LN_EPS) * g + b


def _silu(x):
    return x / (1.0 + jnp.exp(-x))


def _softplus(x):
    return jnp.maximum(x, 0.0) + jnp.log1p(jnp.exp(-jnp.abs(x)))


def _split3(x):
    hi = x.astype(BF16)
    r = x - hi.astype(F32)
    mid = r.astype(BF16)
    lo = (r - mid.astype(F32)).astype(BF16)
    return hi, mid, lo


def _dot_exact_rhs(m, x):
    hi, mid, lo = _split3(x)
    return _dot(m, hi) + _dot(m, mid) + _dot(m, lo)


def _dot_exact_lhs(x, m):
    hi, mid, lo = _split3(x)
    return _dot(hi, m) + _dot(mid, m) + _dot(lo, m)


def _const_spec(shape):
    nd = len(shape)
    return pl.BlockSpec(shape, lambda *_: (0,) * nd)


N_DT_TILES = 2 * SSM_GROUPS


def _ln_inproj_kernel(x_ref, g_ref, b_ref, wz_ref, wxbc_ref, wdt_ref, wqT_ref, wk_ref, wvT_ref, wg_ref,
                      z_ref, xbc_ref, dt_ref, dtT_ref, qT_ref, k_ref, vT_ref, gate_ref):
    h = _layer_norm(x_ref[...], g_ref[...], b_ref[...]).astype(BF16)
    z_ref[...] = _dot(h, wz_ref[...])
    xbc_ref[...] = _dot(h, wxbc_ref[...])
    dt = _dot(h, wdt_ref[...])
    dt_ref[...] = dt
    for i in range(N_DT_TILES):
        dtT_ref[i] = dt[:, i * LANES:(i + 1) * LANES].T[:HEADS_PER_GROUP, :]
    qT_ref[...] = (_dot_nt(wqT_ref[...], h) * (LOG2E * ATTN_HEAD_DIM ** -0.5)).astype(BF16)
    k_ref[...] = _dot(h, wk_ref[...]).astype(BF16)
    vT_ref[...] = _dot_nt(wvT_ref[...], h).astype(BF16)
    gate_ref[...] = _dot(h, wg_ref[...])


def _ln_inproj(x2d, ln_g, ln_b, w_in, tm=512):
    t = x2d.shape[0]
    offs = [0, D_SSM, D_SSM + D_XBC, D_SSM + D_XBC + 2 * SSM_HEADS]
    wz = w_in[:, offs[0]:offs[1]].astype(BF16)
    wxbc = w_in[:, offs[1]:offs[2]].astype(BF16)
    wdt = w_in[:, offs[2]:offs[3]].reshape(D_MODEL, N_DT_TILES, HEADS_PER_GROUP)
    wdt = jnp.pad(wdt, ((0, 0), (0, 0), (0, LANES - HEADS_PER_GROUP))).reshape(D_MODEL, -1).astype(BF16)
    o = offs[3]
    wq, wk, wv, wg = (w_in[:, o + i * D_ATTN:o + (i + 1) * D_ATTN].astype(BF16) for i in range(4))
    row = lambda n: pl.BlockSpec((tm, n), lambda i: (i, 0))
    col = lambda n: pl.BlockSpec((n, tm), lambda i: (0, i))
    weights = [wz, wxbc, wdt, wq.T, wk, wv.T, wg]
    return pl.pallas_call(
        _ln_inproj_kernel,
        grid=(t // tm,),
        in_specs=[row(D_MODEL), _const_spec((1, D_MODEL)), _const_spec((1, D_MODEL))]
                 + [pl.BlockSpec(w.shape, lambda i: (0, 0), pipeline_mode=pl.Buffered(1)) for w in weights],
        out_specs=[row(D_SSM), row(D_XBC), row(N_DT_TILES * LANES),
                   pl.BlockSpec((N_DT_TILES, HEADS_PER_GROUP, tm), lambda i: (0, 0, i)),
                   col(D_ATTN), row(D_ATTN), col(D_ATTN), row(D_ATTN)],
        out_shape=[jax.ShapeDtypeStruct((t, D_SSM), F32), jax.ShapeDtypeStruct((t, D_XBC), F32),
                   jax.ShapeDtypeStruct((t, N_DT_TILES * LANES), F32),
                   jax.ShapeDtypeStruct((N_DT_TILES, HEADS_PER_GROUP, t), F32),
                   jax.ShapeDtypeStruct((D_ATTN, t), BF16), jax.ShapeDtypeStruct((t, D_ATTN), BF16),
                   jax.ShapeDtypeStruct((D_ATTN, t), BF16), jax.ShapeDtypeStruct((t, D_ATTN), F32)],
        compiler_params=pltpu.CompilerParams(dimension_semantics=("parallel",), vmem_limit_bytes=VMEM_LIMIT),
        name="ln_inproj",
    )(x2d, ln_g.reshape(1, -1), ln_b.reshape(1, -1), *weights)


HALO = 8


def _conv_silu_kernel(prev_ref, cur_ref, next_ref, w_ref, b_ref, xs_ref, bt_ref, c_ref, ext_ref):
    i = pl.program_id(1)
    n = pl.num_programs(1)
    tc = cur_ref.shape[0]
    ext_ref[0:HALO, :] = jnp.where(i == 0, 0.0, prev_ref[...])
    ext_ref[HALO:HALO + tc, :] = cur_ref[...]
    ext_ref[HALO + tc:, :] = jnp.where(i == n - 1, 0.0, next_ref[...])
    acc = jnp.zeros((tc, D_XBC), F32) + b_ref[...]
    for kk in range(D_CONV):
        acc = acc + ext_ref[pl.ds(HALO - D_CONV // 2 + kk, tc), :] * w_ref[kk:kk + 1, :]
    act = _silu(acc)
    xs_ref[...] = act[:, :D_SSM]
    for g in range(SSM_GROUPS):
        lo = D_SSM + g * D_STATE
        bt_ref[g] = act[:, lo:lo + D_STATE].T.astype(BF16)
    c_ref[...] = act[:, D_SSM + SSM_GROUPS * D_STATE:].astype(BF16)


def _conv_silu(xbc, conv_w, conv_b, tc=512):
    b, s, _ = xbc.shape
    nb = s // tc
    hb = tc // HALO
    return pl.pallas_call(
        _conv_silu_kernel,
        grid=(b, nb),
        in_specs=[
            pl.BlockSpec((None, HALO, D_XBC), lambda bi, i: (bi, jnp.maximum(i * hb - 1, 0), 0)),
            pl.BlockSpec((None, tc, D_XBC), lambda bi, i: (bi, i, 0)),
            pl.BlockSpec((None, HALO, D_XBC), lambda bi, i: (bi, jnp.minimum((i + 1) * hb, s // HALO - 1), 0)),
            _const_spec((D_CONV, D_XBC)),
            _const_spec((1, D_XBC)),
        ],
        out_specs=[
            pl.BlockSpec((None, tc, D_SSM), lambda bi, i: (bi, i, 0)),
            pl.BlockSpec((None, SSM_GROUPS, D_STATE, tc), lambda bi, i: (bi, 0, 0, i)),
            pl.BlockSpec((None, tc, SSM_GROUPS * D_STATE), lambda bi, i: (bi, i, 0)),
        ],
        out_shape=[
            jax.ShapeDtypeStruct((b, s, D_SSM), F32),
            jax.ShapeDtypeStruct((b, SSM_GROUPS, D_STATE, s), BF16),
            jax.ShapeDtypeStruct((b, s, SSM_GROUPS * D_STATE), BF16),
        ],
        scratch_shapes=[pltpu.VMEM((tc + 2 * HALO, D_XBC), F32)],
        compiler_params=pltpu.CompilerParams(dimension_semantics=("parallel", "parallel"),
                                             vmem_limit_bytes=VMEM_LIMIT),
        name="conv_silu",
    )(xbc, xbc, xbc, conv_w, conv_b.reshape(1, -1))


def _ssd_chunk(rev, x, bt, cm, dt_raw, dtT_raw, alog_row, alog_col, bias_row, bias_col, state):
    L = CHUNK
    r = lax.broadcasted_iota(jnp.int32, (L, L), 0)
    c = lax.broadcasted_iota(jnp.int32, (L, L), 1)
    keep = (c >= r) if rev else (c <= r)
    cum_l = keep.astype(BF16)
    cum_r = ((r >= c) if rev else (r <= c)).astype(BF16)
    eh = lax.broadcasted_iota(jnp.int32, (HEADS_PER_GROUP, GROUP_WIDTH), 0)
    el = lax.broadcasted_iota(jnp.int32, (HEADS_PER_GROUP, GROUP_WIDTH), 1)
    expand = (el // SSM_HEAD_DIM == eh).astype(BF16)

    dt = _softplus(dt_raw + bias_row)
    a = dt * (-jnp.exp(alog_row))
    dt_t = _softplus(dtT_raw + bias_col)
    a_t = dt_t * (-jnp.exp(alog_col))
    acs = _dot_exact_rhs(cum_l, a)
    acs_t = _dot_exact_lhs(a_t, cum_r)
    dt_x = _dot_exact_lhs(dt, expand)
    acs_x = _dot_exact_lhs(acs, expand)
    edge = 0 if rev else L - 1
    tot_x = acs_x[edge:edge + 1, :]

    xdt = x * dt_x
    y = _dot(cm, state.astype(BF16)) * jnp.exp(acs_x)
    cb = _dot(cm, bt)
    lane = lax.broadcasted_iota(jnp.int32, (L, LANES), 1)
    tiles = []
    for t in range(GROUP_WIDTH // LANES):
        xt = xdt[:, t * LANES:(t + 1) * LANES]
        acc = None
        for half in range(LANES // SSM_HEAD_DIM):
            j = t * (LANES // SSM_HEAD_DIM) + half
            seg = acs[:, j:j + 1] - acs_t[j:j + 1, :]
            gmat = (cb * jnp.exp(jnp.where(keep, seg, -jnp.inf))).astype(BF16)
            in_head = (lane >= half * SSM_HEAD_DIM) & (lane < (half + 1) * SSM_HEAD_DIM)
            part = _dot(gmat, jnp.where(in_head, xt, 0.0).astype(BF16))
            acc = part if acc is None else acc + part
        tiles.append(acc)
    y = y + jnp.concatenate(tiles, axis=1)

    w = (xdt * jnp.exp(tot_x - acs_x)).astype(BF16)
    new_state = jnp.exp(tot_x) * state + _dot(bt, w)
    return y, new_state


def _ssd_kernel(rev, nchunk, *refs):
    if rev:
        (xs_ref, bt_ref, c_ref, dt_ref, dtT_ref, alr_ref, alc_ref, br_ref, bc_ref,
         yf_ref, z_ref, ng_ref, y_ref, state_ref) = refs
    else:
        (xs_ref, bt_ref, c_ref, dt_ref, dtT_ref, alr_ref, alc_ref, br_ref, bc_ref,
         dx_ref, y_ref, state_ref) = refs

    @pl.when(pl.program_id(2) == 0)
    def _():
        state_ref[...] = jnp.zeros_like(state_ref)

    order = range(nchunk - 1, -1, -1) if rev else range(nchunk)
    for ci in order:
        rows = slice(ci * CHUNK, (ci + 1) * CHUNK)
        x = xs_ref[rows, :]
        y, new_state = _ssd_chunk(rev, x, bt_ref[:, rows], c_ref[rows, :], dt_ref[rows, :HEADS_PER_GROUP], dtT_ref[:, rows],
                                  alr_ref[...], alc_ref[...], br_ref[...], bc_ref[...], state_ref[...])
        state_ref[...] = new_state
        if rev:
            yy = (yf_ref[rows, :] + y) * _silu(z_ref[rows, :])
            ms = jnp.mean(yy * yy, axis=-1, keepdims=True)
            y_ref[rows, :] = yy * lax.rsqrt(ms + RMS_EPS) * ng_ref[...]
        else:
            y_ref[rows, :] = y + dx_ref[...] * x


def _ssd_pass(rev, xs, bt, cm, dt, dtT, a_log, dt_bias, extra, nchunk=4):
    b, s, _ = xs.shape
    tcs = nchunk * CHUNK
    nblk = s // tcs
    blk = (lambda i: nblk - 1 - i) if rev else (lambda i: i)
    dirn = 1 if rev else 0
    gw, hg = GROUP_WIDTH, HEADS_PER_GROUP
    seq_spec = lambda w: pl.BlockSpec((None, tcs, w), lambda bi, g, i: (bi, blk(i), g))
    in_specs = [
        seq_spec(gw),
        pl.BlockSpec((None, None, D_STATE, tcs), lambda bi, g, i: (bi, g, 0, blk(i))),
        seq_spec(D_STATE),
        pl.BlockSpec((tcs, LANES), lambda bi, g, i: (bi * nblk + blk(i), dirn * SSM_GROUPS + g)),
        pl.BlockSpec((None, hg, tcs), lambda bi, g, i: (dirn * SSM_GROUPS + g, 0, bi * nblk + blk(i))),
        pl.BlockSpec((None, 1, hg), lambda bi, g, i: (g, 0, 0)),
        pl.BlockSpec((None, hg, 1), lambda bi, g, i: (g, 0, 0)),
        pl.BlockSpec((None, 1, hg), lambda bi, g, i: (g, 0, 0)),
        pl.BlockSpec((None, hg, 1), lambda bi, g, i: (g, 0, 0)),
    ]
    args = [xs, bt, cm, dt, dtT,
            a_log.reshape(SSM_GROUPS, 1, hg), a_log.reshape(SSM_GROUPS, hg, 1),
            dt_bias.reshape(SSM_GROUPS, 1, hg), dt_bias.reshape(SSM_GROUPS, hg, 1)]
    group_row = pl.BlockSpec((None, 1, gw), lambda bi, g, i: (g, 0, 0))
    if rev:
        y_fwd, z, norm_g = extra
        in_specs += [seq_spec(gw), seq_spec(gw), group_row]
        args += [y_fwd, z, norm_g.reshape(SSM_GROUPS, 1, gw)]
    else:
        (d_skip,) = extra
        in_specs += [group_row]
        args += [jnp.repeat(d_skip, SSM_HEAD_DIM).reshape(SSM_GROUPS, 1, gw)]
    return pl.pallas_call(
        functools.partial(_ssd_kernel, rev, nchunk),
        grid=(b, SSM_GROUPS, nblk),
        in_specs=in_specs,
        out_specs=seq_spec(gw),
        out_shape=jax.ShapeDtypeStruct((b, s, D_SSM), F32),
        scratch_shapes=[pltpu.VMEM((D_STATE, gw), F32)],
        compiler_params=pltpu.CompilerParams(dimension_semantics=("parallel", "parallel", "arbitrary"),
                                             vmem_limit_bytes=VMEM_LIMIT),
        name="ssd_bwd" if rev else "ssd_fwd",
    )(*args)


NEG_BIG = -1e30


def _attn_kernel(tq, tk, nk, slopes_ref, qT_ref, k_ref, vT_ref, lq1_ref, lk1_ref, lq2_ref, lk2_ref, sg_ref,
                 o_ref, e_ref, s0_ref, s1_ref, acc_ref):
    qi = pl.program_id(2)
    slope = slopes_ref[pl.program_id(1)]
    ratio = tk // tq
    i0 = qi * tq
    jd = qi // ratio
    dsel = qi % ratio

    @pl.when(qi == 0)
    def _():
        ii = lax.broadcasted_iota(jnp.int32, (tk, tq), 1)
        jj = lax.broadcasted_iota(jnp.int32, (tk, tq), 0)
        rel = (ii - jj).astype(F32)
        e_ref[0] = -slope * rel
        e_ref[1] = slope * rel
        for d in range(ratio):
            e_ref[2 + d] = -slope * jnp.abs(rel + float(d * tq))

    row = lax.broadcasted_iota(jnp.int32, (2 * ATTN_HEAD_DIM, tq), 0)
    qf = qT_ref[...].astype(F32)
    rhs = (jnp.where(row < ATTN_HEAD_DIM, qf, 0.0).astype(BF16),
           jnp.where(row >= ATTN_HEAD_DIM, qf, 0.0).astype(BF16))
    acc_ref[...] = jnp.zeros_like(acc_ref)

    def stage_a(j, s_ref, m_old):
        off = (i0 - j * tk).astype(F32)
        before, after = j < jd, j > jd
        eidx = jnp.where(before, 0, jnp.where(after, 1, 2 + dsel))
        cj = jnp.where(before, -slope * off, jnp.where(after, slope * off, 0.0))
        kt = k_ref[pl.ds(pl.multiple_of(j * tk, tk), tk), :]
        m_new, shift, alpha = [], [], []
        for r in range(2):
            s = _dot(kt, rhs[r]) + e_ref[eidx]
            s_ref[r] = s
            mr = jnp.maximum(m_old[r], jnp.max(s, axis=0, keepdims=True) + cj)
            m_new.append(mr)
            shift.append(mr - cj)
            alpha.append(jnp.exp2(m_old[r] - mr))
        return tuple(m_new), tuple(shift), tuple(alpha)

    def stage_b(j, s_ref, shift, alpha, l_old):
        vt = vT_ref[:, pl.ds(pl.multiple_of(j * tk, tk), tk)]
        l_new = []
        for r in range(2):
            p = jnp.exp2(s_ref[r] - shift[r])
            l_new.append(alpha[r] * l_old[r] + jnp.sum(p, axis=0, keepdims=True))
            acc_ref[r] = alpha[r] * acc_ref[r] + _dot(vt, p.astype(BF16))
        return tuple(l_new)

    neg = jnp.full((1, tq), NEG_BIG, F32)
    zero = jnp.zeros((1, tq), F32)
    m, shift, alpha = stage_a(0, s0_ref, (neg, neg))

    def body(t, carry):
        m, shift, alpha, l = carry
        j = 2 * t
        m1, shift1, alpha1 = stage_a(j + 1, s1_ref, m)
        l1 = stage_b(j, s0_ref, shift, alpha, l)
        m2, shift2, alpha2 = stage_a(jnp.minimum(j + 2, nk - 1), s0_ref, m1)
        l2 = stage_b(j + 1, s1_ref, shift1, alpha1, l1)
        return m2, shift2, alpha2, l2

    _, _, _, l = lax.fori_loop(0, nk // 2, body, (m, shift, alpha, (zero, zero)))

    lam = (jnp.exp(jnp.sum(lq1_ref[...] * lk1_ref[...], axis=-1, keepdims=True))
           - jnp.exp(jnp.sum(lq2_ref[...] * lk2_ref[...], axis=-1, keepdims=True)) + LAM_INIT)
    o = acc_ref[0] / l[0] - lam * (acc_ref[1] / l[1])
    ms = jnp.mean(o * o, axis=0, keepdims=True)
    o_ref[...] = (o * lax.rsqrt(ms + RMS_EPS) * sg_ref[...] * (1.0 - LAM_INIT)).T


def _diff_attn(qT, k3, vT, lq1, lk1, lq2, lk2, subln_g, tq=256, tk=512):
    b, s, _ = k3.shape
    nq, nk = s // tq, s // tk
    assert tk % tq == 0 and nk % 2 == 0
    hw = 2 * ATTN_HEAD_DIM
    slopes = jnp.exp2(-8.0 * (jnp.arange(ATTN_HEADS, dtype=F32) + 1.0) / ATTN_HEADS) * LOG2E
    vec = lambda a: a.reshape(1, -1)
    return pl.pallas_call(
        functools.partial(_attn_kernel, tq, tk, nk),
        grid=(b, ATTN_HEADS, nq),
        in_specs=[
            pl.BlockSpec(memory_space=pltpu.SMEM),
            pl.BlockSpec((hw, tq), lambda bi, hi, qi: (hi, bi * nq + qi)),
            pl.BlockSpec((None, s, hw), lambda bi, hi, qi: (bi, 0, hi)),
            pl.BlockSpec((ATTN_V_DIM, s), lambda bi, hi, qi: (hi, bi)),
            _const_spec((1, ATTN_HEAD_DIM)), _const_spec((1, ATTN_HEAD_DIM)),
            _const_spec((1, ATTN_HEAD_DIM)), _const_spec((1, ATTN_HEAD_DIM)),
            _const_spec((ATTN_V_DIM, 1)),
        ],
        out_specs=pl.BlockSpec((None, tq, ATTN_V_DIM), lambda bi, hi, qi: (bi, qi, hi)),
        out_shape=jax.ShapeDtypeStruct((b, s, ATTN_HEADS * ATTN_V_DIM), F32),
        scratch_shapes=[pltpu.VMEM((2 + tk // tq, tk, tq), F32),
                        pltpu.VMEM((2, tk, tq), F32), pltpu.VMEM((2, tk, tq), F32),
                        pltpu.VMEM((2, ATTN_V_DIM, tq), F32)],
        compiler_params=pltpu.CompilerParams(dimension_semantics=("parallel", "parallel", "arbitrary"),
                                             vmem_limit_bytes=VMEM_LIMIT),
        name="diff_attn",
    )(slopes, qT, k3, vT, vec(lq1), vec(lk1), vec(lq2), vec(lk2), subln_g.reshape(-1, 1))


def _out_proj_kernel(x_ref, eg_ref, eb_ref, ys_ref, o_ref, gate_ref, w_ref, g_ref, b_ref, out_ref):
    h = _layer_norm(x_ref[...], eg_ref[...], eb_ref[...])
    ya = o_ref[...] * _silu(gate_ref[...])
    mix = _dot(ys_ref[...].astype(BF16), w_ref[:D_SSM, :]) + _dot(ya.astype(BF16), w_ref[D_SSM:, :])
    out_ref[...] = _layer_norm(ALPHA * h + mix, g_ref[...], b_ref[...])


def _out_proj(x2d, ln_emb_g, ln_emb_b, y_ssm, o, gate, w_out, ln_g, ln_b, tm=512):
    t = x2d.shape[0]
    row = lambda n: pl.BlockSpec((tm, n), lambda i: (i, 0))
    vec = lambda a: a.reshape(1, -1)
    return pl.pallas_call(
        _out_proj_kernel,
        grid=(t // tm,),
        in_specs=[row(D_MODEL), _const_spec((1, D_MODEL)), _const_spec((1, D_MODEL)),
                  row(D_SSM), row(D_ATTN), row(D_ATTN),
                  _const_spec((D_SSM + D_ATTN, D_MODEL)), _const_spec((1, D_MODEL)), _const_spec((1, D_MODEL))],
        out_specs=row(D_MODEL),
        out_shape=jax.ShapeDtypeStruct((t, D_MODEL), F32),
        compiler_params=pltpu.CompilerParams(dimension_semantics=("parallel",), vmem_limit_bytes=VMEM_LIMIT),
        name="out_proj",
    )(x2d, vec(ln_emb_g), vec(ln_emb_b), y_ssm, o, gate, w_out.astype(BF16), vec(ln_g), vec(ln_b))


def kernel(x, ln_emb_g, ln_emb_b, w_in, conv_w, conv_b, A_log_fwd, A_log_bwd, dt_bias_fwd, dt_bias_bwd, D_skip,
           ssm_norm_g, lambda_q1, lambda_k1, lambda_q2, lambda_k2, subln_g, w_out, ln_g, ln_b):
    b, s, _ = x.shape
    t = b * s
    x2d = x.reshape(t, D_MODEL)
    z, xbc, dt, dtT, qT, k, vT, gate = _ln_inproj(x2d, ln_emb_g, ln_emb_b, w_in[0])

    xs, bt, cm = _conv_silu(xbc.reshape(b, s, D_XBC), conv_w[0], conv_b[0])
    y_fwd = _ssd_pass(False, xs, bt, cm, dt, dtT, A_log_fwd[0], dt_bias_fwd[0], (D_skip[0],))
    y_ssm = _ssd_pass(True, xs, bt, cm, dt, dtT, A_log_bwd[0], dt_bias_bwd[0],
                      (y_fwd, z.reshape(b, s, D_SSM), ssm_norm_g[0]))

    o = _diff_attn(qT, k.reshape(b, s, D_ATTN), vT, lambda_q1[0], lambda_k1[0], lambda_q2[0], lambda_k2[0],
                   subln_g[0]).reshape(t, D_ATTN)

    out = _out_proj(x2d, ln_emb_g, ln_emb_b, y_ssm.reshape(t, D_SSM), o, gate, w_out[0], ln_g[0], ln_b[0])
    return out.reshape(b, s, D_MODEL)
```

```python
import functools
import math

import jax
import jax.numpy as jnp
from jax import lax
from jax.experimental import pallas as pl
from jax.experimental.pallas import tpu as pltpu

D_MODEL = 1024
D_SSM = 1024
SSM_HEAD_DIM = 64
SSM_HEADS = 16
SSM_GROUPS = 2
HEADS_PER_GROUP = SSM_HEADS // SSM_GROUPS
D_STATE = 128
D_CONV = 5
CHUNK = 128
D_XBC = D_SSM + 2 * SSM_GROUPS * D_STATE
D_ATTN = 1024
ATTN_HEADS = 8
ATTN_HEAD_DIM = 64
ATTN_V_DIM = 128
GROUP_WIDTH = D_SSM // SSM_GROUPS
DEPTH = 1
ALPHA = (2.0 * DEPTH) ** 0.25
LN_EPS = 1e-5
RMS_EPS = 1e-5
LAM_INIT = 0.8 - 0.6 * math.exp(-0.3 * 0)

LANES = 128
VMEM_LIMIT = 48 * 1024 * 1024

F32 = jnp.float32
BF16 = jnp.bfloat16


LOG2E = math.log2(math.e)


def _dot(a, b):
    return jnp.dot(a, b, preferred_element_type=F32)


def _dot_nt(a, b):
    return lax.dot_general(a, b, (((1,), (1,)), ((), ())), preferred_element_type=F32)


def _layer_norm(x, g, b):
    mu = jnp.mean(x, axis=-1, keepdims=True)
    xc = x - mu
    var = jnp.mean(xc * xc, axis=-1, keepdims=True)
    return xc * lax.rsqrt(var + LN_EPS) * g + b


def _silu(x):
    return x / (1.0 + jnp.exp(-x))


def _softplus(x):
    return jnp.maximum(x, 0.0) + jnp.log1p(jnp.exp(-jnp.abs(x)))


def _split3(x):
    hi = x.astype(BF16)
    r = x - hi.astype(F32)
    mid = r.astype(BF16)
    lo = (r - mid.astype(F32)).astype(BF16)
    return hi, mid, lo


def _dot_exact_rhs(m, x):
    hi, mid, lo = _split3(x)
    return _dot(m, hi) + _dot(m, mid) + _dot(m, lo)


def _dot_exact_lhs(x, m):
    hi, mid, lo = _split3(x)
    return _dot(hi, m) + _dot(mid, m) + _dot(lo, m)


def _const_spec(shape):
    nd = len(shape)
    return pl.BlockSpec(shape, lambda *_: (0,) * nd)


N_DT_TILES = 2 * SSM_GROUPS


ATTN_TQ = 256
ATTN_TK = 512
N_MAPS = 2 * ATTN_HEADS
NORM_SLACK = 1.01


def _ln_inproj_kernel(x_ref, g_ref, b_ref, wz_ref, wxbc_ref, wdt_ref, wqT_ref, wk_ref, wvT_ref, wg_ref,
                      grp_ref, grpT_ref,
                      z_ref, xbc_ref, dt_ref, dtT_ref, qT_ref, k_ref, vT_ref, gate_ref, kn_ref, qn_ref):
    h = _layer_norm(x_ref[...], g_ref[...], b_ref[...]).astype(BF16)
    z_ref[...] = _dot(h, wz_ref[...])
    xbc_ref[...] = _dot(h, wxbc_ref[...])
    dt = _dot(h, wdt_ref[...])
    dt_ref[...] = dt
    for i in range(N_DT_TILES):
        dtT_ref[i] = dt[:, i * LANES:(i + 1) * LANES].T[:HEADS_PER_GROUP, :]
    qT = _dot_nt(wqT_ref[...], h) * (LOG2E * ATTN_HEAD_DIM ** -0.5)
    qT_ref[...] = qT.astype(BF16)
    k = _dot(h, wk_ref[...])
    k_ref[...] = k.astype(BF16)
    vT_ref[...] = _dot_nt(wvT_ref[...], h).astype(BF16)
    gate_ref[...] = _dot(h, wg_ref[...])
    k2 = _dot((k * k).astype(BF16), grp_ref[...])
    kn_ref[...] = jnp.sqrt(jnp.max(k2, axis=0, keepdims=True)) * NORM_SLACK
    q2 = _dot(grpT_ref[...], (qT * qT).astype(BF16))[:N_MAPS, :]
    lane = lax.broadcasted_iota(jnp.int32, (N_MAPS, LANES), 1)
    qn = jnp.zeros((N_MAPS, LANES), F32)
    for part in range(q2.shape[1] // ATTN_TQ):
        pm = jnp.max(q2[:, part * ATTN_TQ:(part + 1) * ATTN_TQ], axis=1, keepdims=True)
        qn = jnp.where(lane == part, jnp.sqrt(pm) * NORM_SLACK, qn)
    qn_ref[...] = qn


def _ln_inproj(x2d, ln_g, ln_b, w_in, tm=ATTN_TK):
    t = x2d.shape[0]
    grp = (jnp.arange(D_ATTN)[:, None] // ATTN_HEAD_DIM == jnp.arange(LANES)[None, :]).astype(BF16)
    offs = [0, D_SSM, D_SSM + D_XBC, D_SSM + D_XBC + 2 * SSM_HEADS]
    wz = w_in[:, offs[0]:offs[1]].astype(BF16)
    wxbc = w_in[:, offs[1]:offs[2]].astype(BF16)
    wdt = w_in[:, offs[2]:offs[3]].reshape(D_MODEL, N_DT_TILES, HEADS_PER_GROUP)
    wdt = jnp.pad(wdt, ((0, 0), (0, 0), (0, LANES - HEADS_PER_GROUP))).reshape(D_MODEL, -1).astype(BF16)
    o = offs[3]
    wq, wk, wv, wg = (w_in[:, o + i * D_ATTN:o + (i + 1) * D_ATTN].astype(BF16) for i in range(4))
    row = lambda n: pl.BlockSpec((tm, n), lambda i: (i, 0))
    col = lambda n: pl.BlockSpec((n, tm), lambda i: (0, i))
    weights = [wz, wxbc, wdt, wq.T, wk, wv.T, wg, grp, grp.T]
    nt = t // tm
    return pl.pallas_call(
        _ln_inproj_kernel,
        grid=(nt,),
        in_specs=[row(D_MODEL), _const_spec((1, D_MODEL)), _const_spec((1, D_MODEL))]
                 + [pl.BlockSpec(w.shape, lambda i: (0, 0), pipeline_mode=pl.Buffered(1)) for w in weights],
        out_specs=[row(D_SSM), row(D_XBC), row(N_DT_TILES * LANES),
                   pl.BlockSpec((N_DT_TILES, HEADS_PER_GROUP, tm), lambda i: (0, 0, i)),
                   col(D_ATTN), row(D_ATTN), col(D_ATTN), row(D_ATTN),
                   pl.BlockSpec((None, 1, LANES), lambda i: (i, 0, 0)),
                   pl.BlockSpec((None, N_MAPS, LANES), lambda i: (i, 0, 0))],
        out_shape=[jax.ShapeDtypeStruct((t, D_SSM), F32), jax.ShapeDtypeStruct((t, D_XBC), F32),
                   jax.ShapeDtypeStruct((t, N_DT_TILES * LANES), F32),
                   jax.ShapeDtypeStruct((N_DT_TILES, HEADS_PER_GROUP, t), F32),
                   jax.ShapeDtypeStruct((D_ATTN, t), BF16), jax.ShapeDtypeStruct((t, D_ATTN), BF16),
                   jax.ShapeDtypeStruct((D_ATTN, t), BF16), jax.ShapeDtypeStruct((t, D_ATTN), F32),
                   jax.ShapeDtypeStruct((nt, 1, LANES), F32), jax.ShapeDtypeStruct((nt, N_MAPS, LANES), F32)],
        compiler_params=pltpu.CompilerParams(dimension_semantics=("parallel",), vmem_limit_bytes=VMEM_LIMIT),
        name="ln_inproj",
    )(x2d, ln_g.reshape(1, -1), ln_b.reshape(1, -1), *weights)


HALO = 8


def _conv_silu_kernel(prev_ref, cur_ref, next_ref, w_ref, b_ref, xs_ref, bt_ref, c_ref, ext_ref):
    i = pl.program_id(1)
    n = pl.num_programs(1)
    tc = cur_ref.shape[0]
    ext_ref[0:HALO, :] = jnp.where(i == 0, 0.0, prev_ref[...])
    ext_ref[HALO:HALO + tc, :] = cur_ref[...]
    ext_ref[HALO + tc:, :] = jnp.where(i == n - 1, 0.0, next_ref[...])
    acc = jnp.zeros((tc, D_XBC), F32) + b_ref[...]
    for kk in range(D_CONV):
        acc = acc + ext_ref[pl.ds(HALO - D_CONV // 2 + kk, tc), :] * w_ref[kk:kk + 1, :]
    act = _silu(acc)
    xs_ref[...] = act[:, :D_SSM]
    for g in range(SSM_GROUPS):
        lo = D_SSM + g * D_STATE
        bt_ref[g] = act[:, lo:lo + D_STATE].T.astype(BF16)
    c_ref[...] = act[:, D_SSM + SSM_GROUPS * D_STATE:].astype(BF16)


def _conv_silu(xbc, conv_w, conv_b, tc=512):
    b, s, _ = xbc.shape
    nb = s // tc
    hb = tc // HALO
    return pl.pallas_call(
        _conv_silu_kernel,
        grid=(b, nb),
        in_specs=[
            pl.BlockSpec((None, HALO, D_XBC), lambda bi, i: (bi, jnp.maximum(i * hb - 1, 0), 0)),
            pl.BlockSpec((None, tc, D_XBC), lambda bi, i: (bi, i, 0)),
            pl.BlockSpec((None, HALO, D_XBC), lambda bi, i: (bi, jnp.minimum((i + 1) * hb, s // HALO - 1), 0)),
            _const_spec((D_CONV, D_XBC)),
            _const_spec((1, D_XBC)),
        ],
        out_specs=[
            pl.BlockSpec((None, tc, D_SSM), lambda bi, i: (bi, i, 0)),
            pl.BlockSpec((None, SSM_GROUPS, D_STATE, tc), lambda bi, i: (bi, 0, 0, i)),
            pl.BlockSpec((None, tc, SSM_GROUPS * D_STATE), lambda bi, i: (bi, i, 0)),
        ],
        out_shape=[
            jax.ShapeDtypeStruct((b, s, D_SSM), F32),
            jax.ShapeDtypeStruct((b, SSM_GROUPS, D_STATE, s), BF16),
            jax.ShapeDtypeStruct((b, s, SSM_GROUPS * D_STATE), BF16),
        ],
        scratch_shapes=[pltpu.VMEM((tc + 2 * HALO, D_XBC), F32)],
        compiler_params=pltpu.CompilerParams(dimension_semantics=("parallel", "parallel"),
                                             vmem_limit_bytes=VMEM_LIMIT),
        name="conv_silu",
    )(xbc, xbc, xbc, conv_w, conv_b.reshape(1, -1))


def _ssd_chunk(rev, x, bt, cm, dt_raw, dtT_raw, alog_row, alog_col, bias_row, bias_col, state):
    L = CHUNK
    r = lax.broadcasted_iota(jnp.int32, (L, L), 0)
    c = lax.broadcasted_iota(jnp.int32, (L, L), 1)
    keep = (c >= r) if rev else (c <= r)
    cum_l = keep.astype(BF16)
    cum_r = ((r >= c) if rev else (r <= c)).astype(BF16)
    eh = lax.broadcasted_iota(jnp.int32, (HEADS_PER_GROUP, GROUP_WIDTH), 0)
    el = lax.broadcasted_iota(jnp.int32, (HEADS_PER_GROUP, GROUP_WIDTH), 1)
    expand = (el // SSM_HEAD_DIM == eh).astype(BF16)

    dt = _softplus(dt_raw + bias_row)
    a = dt * (-jnp.exp(alog_row))
    dt_t = _softplus(dtT_raw + bias_col)
    a_t = dt_t * (-jnp.exp(alog_col))
    acs = _dot_exact_rhs(cum_l, a)
    acs_t = _dot_exact_lhs(a_t, cum_r)
    dt_x = _dot_exact_lhs(dt, expand)
    acs_x = _dot_exact_lhs(acs, expand)
    edge = 0 if rev else L - 1
    tot_x = acs_x[edge:edge + 1, :]

    xdt = x * dt_x
    y = _dot(cm, state.astype(BF16)) * jnp.exp(acs_x)
    cb = _dot(cm, bt)
    lane = lax.broadcasted_iota(jnp.int32, (L, LANES), 1)
    tiles = []
    for t in range(GROUP_WIDTH // LANES):
        xt = xdt[:, t * LANES:(t + 1) * LANES]
        acc = None
        for half in range(LANES // SSM_HEAD_DIM):
            j = t * (LANES // SSM_HEAD_DIM) + half
            seg = acs[:, j:j + 1] - acs_t[j:j + 1, :]
            gmat = (cb * jnp.exp(jnp.where(keep, seg, -jnp.inf))).astype(BF16)
            in_head = (lane >= half * SSM_HEAD_DIM) & (lane < (half + 1) * SSM_HEAD_DIM)
            part = _dot(gmat, jnp.where(in_head, xt, 0.0).astype(BF16))
            acc = part if acc is None else acc + part
        tiles.append(acc)
    y = y + jnp.concatenate(tiles, axis=1)

    w = (xdt * jnp.exp(tot_x - acs_x)).astype(BF16)
    new_state = jnp.exp(tot_x) * state + _dot(bt, w)
    return y, new_state


def _ssd_kernel(rev, nchunk, *refs):
    if rev:
        (xs_ref, bt_ref, c_ref, dt_ref, dtT_ref, alr_ref, alc_ref, br_ref, bc_ref,
         yf_ref, z_ref, ng_ref, y_ref, state_ref) = refs
    else:
        (xs_ref, bt_ref, c_ref, dt_ref, dtT_ref, alr_ref, alc_ref, br_ref, bc_ref,
         dx_ref, y_ref, state_ref) = refs

    @pl.when(pl.program_id(2) == 0)
    def _():
        state_ref[...] = jnp.zeros_like(state_ref)

    order = range(nchunk - 1, -1, -1) if rev else range(nchunk)
    for ci in order:
        rows = slice(ci * CHUNK, (ci + 1) * CHUNK)
        x = xs_ref[rows, :]
        y, new_state = _ssd_chunk(rev, x, bt_ref[:, rows], c_ref[rows, :], dt_ref[rows, :HEADS_PER_GROUP], dtT_ref[:, rows],
                                  alr_ref[...], alc_ref[...], br_ref[...], bc_ref[...], state_ref[...])
        state_ref[...] = new_state
        if rev:
            yy = (yf_ref[rows, :] + y) * _silu(z_ref[rows, :])
            ms = jnp.mean(yy * yy, axis=-1, keepdims=True)
            y_ref[rows, :] = yy * lax.rsqrt(ms + RMS_EPS) * ng_ref[...]
        else:
            y_ref[rows, :] = y + dx_ref[...] * x


def _ssd_pass(rev, xs, bt, cm, dt, dtT, a_log, dt_bias, extra, nchunk=4):
    b, s, _ = xs.shape
    tcs = nchunk * CHUNK
    nblk = s // tcs
    blk = (lambda i: nblk - 1 - i) if rev else (lambda i: i)
    dirn = 1 if rev else 0
    gw, hg = GROUP_WIDTH, HEADS_PER_GROUP
    seq_spec = lambda w: pl.BlockSpec((None, tcs, w), lambda bi, g, i: (bi, blk(i), g))
    in_specs = [
        seq_spec(gw),
        pl.BlockSpec((None, None, D_STATE, tcs), lambda bi, g, i: (bi, g, 0, blk(i))),
        seq_spec(D_STATE),
        pl.BlockSpec((tcs, LANES), lambda bi, g, i: (bi * nblk + blk(i), dirn * SSM_GROUPS + g)),
        pl.BlockSpec((None, hg, tcs), lambda bi, g, i: (dirn * SSM_GROUPS + g, 0, bi * nblk + blk(i))),
        pl.BlockSpec((None, 1, hg), lambda bi, g, i: (g, 0, 0)),
        pl.BlockSpec((None, hg, 1), lambda bi, g, i: (g, 0, 0)),
        pl.BlockSpec((None, 1, hg), lambda bi, g, i: (g, 0, 0)),
        pl.BlockSpec((None, hg, 1), lambda bi, g, i: (g, 0, 0)),
    ]
    args = [xs, bt, cm, dt, dtT,
            a_log.reshape(SSM_GROUPS, 1, hg), a_log.reshape(SSM_GROUPS, hg, 1),
            dt_bias.reshape(SSM_GROUPS, 1, hg), dt_bias.reshape(SSM_GROUPS, hg, 1)]
    group_row = pl.BlockSpec((None, 1, gw), lambda bi, g, i: (g, 0, 0))
    if rev:
        y_fwd, z, norm_g = extra
        in_specs += [seq_spec(gw), seq_spec(gw), group_row]
        args += [y_fwd, z, norm_g.reshape(SSM_GROUPS, 1, gw)]
    else:
        (d_skip,) = extra
        in_specs += [group_row]
        args += [jnp.repeat(d_skip, SSM_HEAD_DIM).reshape(SSM_GROUPS, 1, gw)]
    return pl.pallas_call(
        functools.partial(_ssd_kernel, rev, nchunk),
        grid=(b, SSM_GROUPS, nblk),
        in_specs=in_specs,
        out_specs=seq_spec(gw),
        out_shape=jax.ShapeDtypeStruct((b, s, D_SSM), F32),
        scratch_shapes=[pltpu.VMEM((D_STATE, gw), F32)],
        compiler_params=pltpu.CompilerParams(dimension_semantics=("parallel", "parallel", "arbitrary"),
                                             vmem_limit_bytes=VMEM_LIMIT),
        name="ssd_bwd" if rev else "ssd_fwd",
    )(*args)


NEG_BIG = -1e30


SKIP_LOG2 = 80.0


def _attn_kernel(tq, tk, nk, slopes_ref, kn_ref, qn_ref, qT_ref, k_ref, vT_ref, lq1_ref, lk1_ref, lq2_ref, lk2_ref,
                 sg_ref, o_ref, e_ref, s0_ref, s1_ref, acc_ref):
    bi, hi, qi = pl.program_id(0), pl.program_id(1), pl.program_id(2)
    nq = pl.num_programs(2)
    slope = slopes_ref[hi]
    ratio = tk // tq
    i0 = qi * tq
    jd = qi // ratio
    dsel = qi % ratio

    jlo, jhi = jd, jd
    for jt in range(nk):
        dist = jnp.maximum(jnp.maximum(i0 - (jt * tk + tk - 1), jt * tk - (i0 + tq - 1)), 0).astype(F32)
        need = None
        for r in range(2):
            base = ((bi * ATTN_HEADS + hi) * 2 + r)
            qn = qn_ref[base * nq + qi]
            reach = SKIP_LOG2 + qn * (kn_ref[base * nk + jt] + kn_ref[base * nk + jd])
            need_r = slope * dist <= reach
            need = need_r if need is None else (need | need_r)
        jlo = jnp.where(need, jnp.minimum(jlo, jt), jlo)
        jhi = jnp.where(need, jnp.maximum(jhi, jt), jhi)
    jlo = (jlo // 2) * 2
    jhi = (jhi // 2) * 2 + 1

    @pl.when(qi == 0)
    def _():
        ii = lax.broadcasted_iota(jnp.int32, (tk, tq), 1)
        jj = lax.broadcasted_iota(jnp.int32, (tk, tq), 0)
        rel = (ii - jj).astype(F32)
        e_ref[0] = -slope * rel
        e_ref[1] = slope * rel
        for d in range(ratio):
            e_ref[2 + d] = -slope * jnp.abs(rel + float(d * tq))

    row = lax.broadcasted_iota(jnp.int32, (2 * ATTN_HEAD_DIM, tq), 0)
    qf = qT_ref[...].astype(F32)
    rhs = (jnp.where(row < ATTN_HEAD_DIM, qf, 0.0).astype(BF16),
           jnp.where(row >= ATTN_HEAD_DIM, qf, 0.0).astype(BF16))
    acc_ref[...] = jnp.zeros_like(acc_ref)

    def stage_a(j, s_ref, m_old):
        off = (i0 - j * tk).astype(F32)
        before, after = j < jd, j > jd
        eidx = jnp.where(before, 0, jnp.where(after, 1, 2 + dsel))
        cj = jnp.where(before, -slope * off, jnp.where(after, slope * off, 0.0))
        kt = k_ref[pl.ds(pl.multiple_of(j * tk, tk), tk), :]
        m_new, shift, alpha = [], [], []
        for r in range(2):
            s = _dot(kt, rhs[r]) + e_ref[eidx]
            s_ref[r] = s
            mr = jnp.maximum(m_old[r], jnp.max(s, axis=0, keepdims=True) + cj)
            m_new.append(mr)
            shift.append(mr - cj)
            alpha.append(jnp.exp2(m_old[r] - mr))
        return tuple(m_new), tuple(shift), tuple(alpha)

    def stage_b(j, s_ref, shift, alpha, l_old):
        vt = vT_ref[:, pl.ds(pl.multiple_of(j * tk, tk), tk)]
        l_new = []
        for r in range(2):
            p = jnp.exp2(s_ref[r] - shift[r])
            l_new.append(alpha[r] * l_old[r] + jnp.sum(p, axis=0, keepdims=True))
            acc_ref[r] = alpha[r] * acc_ref[r] + _dot(vt, p.astype(BF16))
        return tuple(l_new)

    neg = jnp.full((1, tq), NEG_BIG, F32)
    zero = jnp.zeros((1, tq), F32)
    m, shift, alpha = stage_a(jlo, s0_ref, (neg, neg))

    def body(t, carry):
        m, shift, alpha, l = carry
        j = jlo + 2 * t
        m1, shift1, alpha1 = stage_a(j + 1, s1_ref, m)
        l1 = stage_b(j, s0_ref, shift, alpha, l)
        m2, shift2, alpha2 = stage_a(jnp.minimum(j + 2, jhi), s0_ref, m1)
        l2 = stage_b(j + 1, s1_ref, shift1, alpha1, l1)
        return m2, shift2, alpha2, l2

    _, _, _, l = lax.fori_loop(0, (jhi - jlo + 1) // 2, body, (m, shift, alpha, (zero, zero)))

    lam = (jnp.exp(jnp.sum(lq1_ref[...] * lk1_ref[...], axis=-1, keepdims=True))
           - jnp.exp(jnp.sum(lq2_ref[...] * lk2_ref[...], axis=-1, keepdims=True)) + LAM_INIT)
    o = acc_ref[0] / l[0] - lam * (acc_ref[1] / l[1])
    ms = jnp.mean(o * o, axis=0, keepdims=True)
    o_ref[...] = (o * lax.rsqrt(ms + RMS_EPS) * sg_ref[...] * (1.0 - LAM_INIT)).T


def _diff_attn(qT, k3, vT, kn, qn, lq1, lk1, lq2, lk2, subln_g, tq=ATTN_TQ, tk=ATTN_TK):
    b, s, _ = k3.shape
    nq, nk = s // tq, s // tk
    ratio = tk // tq
    assert tk % tq == 0 and nk % 2 == 0
    hw = 2 * ATTN_HEAD_DIM
    slopes = jnp.exp2(-8.0 * (jnp.arange(ATTN_HEADS, dtype=F32) + 1.0) / ATTN_HEADS) * LOG2E
    kn_tab = jnp.transpose(kn[:, 0, :N_MAPS].reshape(b, nk, N_MAPS), (0, 2, 1)).reshape(-1)
    qn_tab = jnp.transpose(qn[:, :, :ratio].reshape(b, nk, N_MAPS, ratio), (0, 2, 1, 3)).reshape(-1)
    vec = lambda a: a.reshape(1, -1)
    return pl.pallas_call(
        functools.partial(_attn_kernel, tq, tk, nk),
        grid=(b, ATTN_HEADS, nq),
        in_specs=[
            pl.BlockSpec(memory_space=pltpu.SMEM),
            pl.BlockSpec(memory_space=pltpu.SMEM),
            pl.BlockSpec(memory_space=pltpu.SMEM),
            pl.BlockSpec((hw, tq), lambda bi, hi, qi: (hi, bi * nq + qi)),
            pl.BlockSpec((None, s, hw), lambda bi, hi, qi: (bi, 0, hi)),
            pl.BlockSpec((ATTN_V_DIM, s), lambda bi, hi, qi: (hi, bi)),
            _const_spec((1, ATTN_HEAD_DIM)), _const_spec((1, ATTN_HEAD_DIM)),
            _const_spec((1, ATTN_HEAD_DIM)), _const_spec((1, ATTN_HEAD_DIM)),
            _const_spec((ATTN_V_DIM, 1)),
        ],
        out_specs=pl.BlockSpec((None, tq, ATTN_V_DIM), lambda bi, hi, qi: (bi, qi, hi)),
        out_shape=jax.ShapeDtypeStruct((b, s, ATTN_HEADS * ATTN_V_DIM), F32),
        scratch_shapes=[pltpu.VMEM((2 + tk // tq, tk, tq), F32),
                        pltpu.VMEM((2, tk, tq), F32), pltpu.VMEM((2, tk, tq), F32),
                        pltpu.VMEM((2, ATTN_V_DIM, tq), F32)],
        compiler_params=pltpu.CompilerParams(dimension_semantics=("parallel", "parallel", "arbitrary"),
                                             vmem_limit_bytes=VMEM_LIMIT),
        name="diff_attn",
    )(slopes, kn_tab, qn_tab, qT, k3, vT, vec(lq1), vec(lk1), vec(lq2), vec(lk2), subln_g.reshape(-1, 1))


def _out_proj_kernel(x_ref, eg_ref, eb_ref, ys_ref, o_ref, gate_ref, w_ref, g_ref, b_ref, out_ref):
    h = _layer_norm(x_ref[...], eg_ref[...], eb_ref[...])
    ya = o_ref[...] * _silu(gate_ref[...])
    mix = _dot(ys_ref[...].astype(BF16), w_ref[:D_SSM, :]) + _dot(ya.astype(BF16), w_ref[D_SSM:, :])
    out_ref[...] = _layer_norm(ALPHA * h + mix, g_ref[...], b_ref[...])


def _out_proj(x2d, ln_emb_g, ln_emb_b, y_ssm, o, gate, w_out, ln_g, ln_b, tm=512):
    t = x2d.shape[0]
    row = lambda n: pl.BlockSpec((tm, n), lambda i: (i, 0))
    vec = lambda a: a.reshape(1, -1)
    return pl.pallas_call(
        _out_proj_kernel,
        grid=(t // tm,),
        in_specs=[row(D_MODEL), _const_spec((1, D_MODEL)), _const_spec((1, D_MODEL)),
                  row(D_SSM), row(D_ATTN), row(D_ATTN),
                  _const_spec((D_SSM + D_ATTN, D_MODEL)), _const_spec((1, D_MODEL)), _const_spec((1, D_MODEL))],
        out_specs=row(D_MODEL),
        out_shape=jax.ShapeDtypeStruct((t, D_MODEL), F32),
        compiler_params=pltpu.CompilerParams(dimension_semantics=("parallel",), vmem_limit_bytes=VMEM_LIMIT),
        name="out_proj",
    )(x2d, vec(ln_emb_g), vec(ln_emb_b), y_ssm, o, gate, w_out.astype(BF16), vec(ln_g), vec(ln_b))


def kernel(x, ln_emb_g, ln_emb_b, w_in, conv_w, conv_b, A_log_fwd, A_log_bwd, dt_bias_fwd, dt_bias_bwd, D_skip,
           ssm_norm_g, lambda_q1, lambda_k1, lambda_q2, lambda_k2, subln_g, w_out, ln_g, ln_b):
    b, s, _ = x.shape
    t = b * s
    x2d = x.reshape(t, D_MODEL)
    z, xbc, dt, dtT, qT, k, vT, gate, kn, qn = _ln_inproj(x2d, ln_emb_g, ln_emb_b, w_in[0])

    xs, bt, cm = _conv_silu(xbc.reshape(b, s, D_XBC), conv_w[0], conv_b[0])
    y_fwd = _ssd_pass(False, xs, bt, cm, dt, dtT, A_log_fwd[0], dt_bias_fwd[0], (D_skip[0],))
    y_ssm = _ssd_pass(True, xs, bt, cm, dt, dtT, A_log_bwd[0], dt_bias_bwd[0],
                      (y_fwd, z.reshape(b, s, D_SSM), ssm_norm_g[0]))

    o = _diff_attn(qT, k.reshape(b, s, D_ATTN), vT, kn, qn, lambda_q1[0], lambda_k1[0], lambda_q2[0],
                   lambda_k2[0], subln_g[0]).reshape(t, D_ATTN)

    out = _out_proj(x2d, ln_emb_g, ln_emb_b, y_ssm.reshape(t, D_SSM), o, gate, w_out[0], ln_g[0], ln_b[0])
    return out.reshape(b, s, D_MODEL)
```

```python
import functools
import math

import jax
import jax.numpy as jnp
from jax import lax
from jax.experimental import pallas as pl
from jax.experimental.pallas import tpu as pltpu

D_MODEL = 1024
D_SSM = 1024
SSM_HEAD_DIM = 64
SSM_HEADS = 16
SSM_GROUPS = 2
HEADS_PER_GROUP = SSM_HEADS // SSM_GROUPS
D_STATE = 128
D_CONV = 5
CHUNK = 128
D_XBC = D_SSM + 2 * SSM_GROUPS * D_STATE
D_ATTN = 1024
ATTN_HEADS = 8
ATTN_HEAD_DIM = 64
ATTN_V_DIM = 128
GROUP_WIDTH = D_SSM // SSM_GROUPS
DEPTH = 1
ALPHA = (2.0 * DEPTH) ** 0.25
LN_EPS = 1e-5
RMS_EPS = 1e-5
LAM_INIT = 0.8 - 0.6 * math.exp(-0.3 * 0)

LANES = 128
VMEM_LIMIT = 48 * 1024 * 1024

F32 = jnp.float32
BF16 = jnp.bfloat16


LOG2E = math.log2(math.e)


def _dot(a, b):
    return jnp.dot(a, b, preferred_element_type=F32)


def _dot_nt(a, b):
    return lax.dot_general(a, b, (((1,), (1,)), ((), ())), preferred_element_type=F32)


def _layer_norm(x, g, b):
    mu = jnp.mean(x, axis=-1, keepdims=True)
    xc = x - mu
    var = jnp.mean(xc * xc, axis=-1, keepdims=True)
    return xc * lax.rsqrt(var + LN_EPS) * g + b


def _silu(x):
    return x / (1.0 + jnp.exp(-x))


def _softplus(x):
    return jnp.maximum(x, 0.0) + jnp.log1p(jnp.exp(-jnp.abs(x)))


def _split3(x):
    hi = x.astype(BF16)
    r = x - hi.astype(F32)
    mid = r.astype(BF16)
    lo = (r - mid.astype(F32)).astype(BF16)
    return hi, mid, lo


def _dot_exact_rhs(m, x):
    hi, mid, lo = _split3(x)
    return _dot(m, hi) + _dot(m, mid) + _dot(m, lo)


def _dot_exact_lhs(x, m):
    hi, mid, lo = _split3(x)
    return _dot(hi, m) + _dot(mid, m) + _dot(lo, m)


def _const_spec(shape):
    nd = len(shape)
    return pl.BlockSpec(shape, lambda *_: (0,) * nd)


N_DT_TILES = 2 * SSM_GROUPS


ATTN_TQ = 256
ATTN_QPART = 256
ATTN_TK = 512
N_MAPS = 2 * ATTN_HEADS
NORM_SLACK = 1.01


def _ln_inproj_kernel(x_ref, g_ref, b_ref, wz_ref, wxbc_ref, wdt_ref, wqT_ref, wk_ref, wvT_ref, wg_ref,
                      grp_ref, grpT_ref,
                      z_ref, xbc_ref, dt_ref, dtT_ref, qT_ref, k_ref, vT_ref, gate_ref, kn_ref, qn_ref):
    h = _layer_norm(x_ref[...], g_ref[...], b_ref[...]).astype(BF16)
    z_ref[...] = _dot(h, wz_ref[...])
    xbc_ref[...] = _dot(h, wxbc_ref[...])
    dt = _dot(h, wdt_ref[...])
    dt_ref[...] = dt
    for i in range(N_DT_TILES):
        dtT_ref[i] = dt[:, i * LANES:(i + 1) * LANES].T[:HEADS_PER_GROUP, :]
    qT = _dot_nt(wqT_ref[...], h) * (LOG2E * ATTN_HEAD_DIM ** -0.5)
    qT_ref[...] = qT.astype(BF16)
    k = _dot(h, wk_ref[...])
    k_ref[...] = k.astype(BF16)
    vT_ref[...] = _dot_nt(wvT_ref[...], h).astype(BF16)
    gate_ref[...] = _dot(h, wg_ref[...])
    k2 = _dot((k * k).astype(BF16), grp_ref[...])
    kn_ref[...] = jnp.sqrt(jnp.max(k2, axis=0, keepdims=True)) * NORM_SLACK
    q2 = _dot(grpT_ref[...], (qT * qT).astype(BF16))[:N_MAPS, :]
    lane = lax.broadcasted_iota(jnp.int32, (N_MAPS, LANES), 1)
    qn = jnp.zeros((N_MAPS, LANES), F32)
    for part in range(q2.shape[1] // ATTN_QPART):
        pm = jnp.max(q2[:, part * ATTN_QPART:(part + 1) * ATTN_QPART], axis=1, keepdims=True)
        qn = jnp.where(lane == part, jnp.sqrt(pm) * NORM_SLACK, qn)
    qn_ref[...] = qn


def _ln_inproj(x2d, ln_g, ln_b, w_in, tm=ATTN_TK):
    t = x2d.shape[0]
    grp = (jnp.arange(D_ATTN)[:, None] // ATTN_HEAD_DIM == jnp.arange(LANES)[None, :]).astype(BF16)
    offs = [0, D_SSM, D_SSM + D_XBC, D_SSM + D_XBC + 2 * SSM_HEADS]
    wz = w_in[:, offs[0]:offs[1]].astype(BF16)
    wxbc = w_in[:, offs[1]:offs[2]].astype(BF16)
    wdt = w_in[:, offs[2]:offs[3]].reshape(D_MODEL, N_DT_TILES, HEADS_PER_GROUP)
    wdt = jnp.pad(wdt, ((0, 0), (0, 0), (0, LANES - HEADS_PER_GROUP))).reshape(D_MODEL, -1).astype(BF16)
    o = offs[3]
    wq, wk, wv, wg = (w_in[:, o + i * D_ATTN:o + (i + 1) * D_ATTN].astype(BF16) for i in range(4))
    row = lambda n: pl.BlockSpec((tm, n), lambda i: (i, 0))
    col = lambda n: pl.BlockSpec((n, tm), lambda i: (0, i))
    weights = [wz, wxbc, wdt, wq.T, wk, wv.T, wg, grp, grp.T]
    nt = t // tm
    return pl.pallas_call(
        _ln_inproj_kernel,
        grid=(nt,),
        in_specs=[row(D_MODEL), _const_spec((1, D_MODEL)), _const_spec((1, D_MODEL))]
                 + [pl.BlockSpec(w.shape, lambda i: (0, 0), pipeline_mode=pl.Buffered(1)) for w in weights],
        out_specs=[row(D_SSM), row(D_XBC), row(N_DT_TILES * LANES),
                   pl.BlockSpec((N_DT_TILES, HEADS_PER_GROUP, tm), lambda i: (0, 0, i)),
                   col(D_ATTN), row(D_ATTN), col(D_ATTN), row(D_ATTN),
                   pl.BlockSpec((None, 1, LANES), lambda i: (i, 0, 0)),
                   pl.BlockSpec((None, N_MAPS, LANES), lambda i: (i, 0, 0))],
        out_shape=[jax.ShapeDtypeStruct((t, D_SSM), F32), jax.ShapeDtypeStruct((t, D_XBC), F32),
                   jax.ShapeDtypeStruct((t, N_DT_TILES * LANES), F32),
                   jax.ShapeDtypeStruct((N_DT_TILES, HEADS_PER_GROUP, t), F32),
                   jax.ShapeDtypeStruct((D_ATTN, t), BF16), jax.ShapeDtypeStruct((t, D_ATTN), BF16),
                   jax.ShapeDtypeStruct((D_ATTN, t), BF16), jax.ShapeDtypeStruct((t, D_ATTN), F32),
                   jax.ShapeDtypeStruct((nt, 1, LANES), F32), jax.ShapeDtypeStruct((nt, N_MAPS, LANES), F32)],
        compiler_params=pltpu.CompilerParams(dimension_semantics=("parallel",), vmem_limit_bytes=VMEM_LIMIT),
        name="ln_inproj",
    )(x2d, ln_g.reshape(1, -1), ln_b.reshape(1, -1), *weights)


HALO = 8


def _conv_silu_kernel(prev_ref, cur_ref, next_ref, w_ref, b_ref, xs_ref, bt_ref, c_ref, ext_ref):
    i = pl.program_id(1)
    n = pl.num_programs(1)
    tc = cur_ref.shape[0]
    ext_ref[0:HALO, :] = jnp.where(i == 0, 0.0, prev_ref[...])
    ext_ref[HALO:HALO + tc, :] = cur_ref[...]
    ext_ref[HALO + tc:, :] = jnp.where(i == n - 1, 0.0, next_ref[...])
    acc = jnp.zeros((tc, D_XBC), F32) + b_ref[...]
    for kk in range(D_CONV):
        acc = acc + ext_ref[pl.ds(HALO - D_CONV // 2 + kk, tc), :] * w_ref[kk:kk + 1, :]
    act = _silu(acc)
    xs_ref[...] = act[:, :D_SSM]
    for g in range(SSM_GROUPS):
        lo = D_SSM + g * D_STATE
        bt_ref[g] = act[:, lo:lo + D_STATE].T.astype(BF16)
    c_ref[...] = act[:, D_SSM + SSM_GROUPS * D_STATE:].astype(BF16)


def _conv_silu(xbc, conv_w, conv_b, tc=512):
    b, s, _ = xbc.shape
    nb = s // tc
    hb = tc // HALO
    return pl.pallas_call(
        _conv_silu_kernel,
        grid=(b, nb),
        in_specs=[
            pl.BlockSpec((None, HALO, D_XBC), lambda bi, i: (bi, jnp.maximum(i * hb - 1, 0), 0)),
            pl.BlockSpec((None, tc, D_XBC), lambda bi, i: (bi, i, 0)),
            pl.BlockSpec((None, HALO, D_XBC), lambda bi, i: (bi, jnp.minimum((i + 1) * hb, s // HALO - 1), 0)),
            _const_spec((D_CONV, D_XBC)),
            _const_spec((1, D_XBC)),
        ],
        out_specs=[
            pl.BlockSpec((None, tc, D_SSM), lambda bi, i: (bi, i, 0)),
            pl.BlockSpec((None, SSM_GROUPS, D_STATE, tc), lambda bi, i: (bi, 0, 0, i)),
            pl.BlockSpec((None, tc, SSM_GROUPS * D_STATE), lambda bi, i: (bi, i, 0)),
        ],
        out_shape=[
            jax.ShapeDtypeStruct((b, s, D_SSM), F32),
            jax.ShapeDtypeStruct((b, SSM_GROUPS, D_STATE, s), BF16),
            jax.ShapeDtypeStruct((b, s, SSM_GROUPS * D_STATE), BF16),
        ],
        scratch_shapes=[pltpu.VMEM((tc + 2 * HALO, D_XBC), F32)],
        compiler_params=pltpu.CompilerParams(dimension_semantics=("parallel", "parallel"),
                                             vmem_limit_bytes=VMEM_LIMIT),
        name="conv_silu",
    )(xbc, xbc, xbc, conv_w, conv_b.reshape(1, -1))


def _ssd_chunk(rev, x, bt, cm, dt_raw, dtT_raw, alog_row, alog_col, bias_row, bias_col, state):
    L = CHUNK
    r = lax.broadcasted_iota(jnp.int32, (L, L), 0)
    c = lax.broadcasted_iota(jnp.int32, (L, L), 1)
    keep = (c >= r) if rev else (c <= r)
    cum_l = keep.astype(BF16)
    cum_r = ((r >= c) if rev else (r <= c)).astype(BF16)
    eh = lax.broadcasted_iota(jnp.int32, (HEADS_PER_GROUP, GROUP_WIDTH), 0)
    el = lax.broadcasted_iota(jnp.int32, (HEADS_PER_GROUP, GROUP_WIDTH), 1)
    expand = (el // SSM_HEAD_DIM == eh).astype(BF16)

    dt = _softplus(dt_raw + bias_row)
    a = dt * (-jnp.exp(alog_row))
    dt_t = _softplus(dtT_raw + bias_col)
    a_t = dt_t * (-jnp.exp(alog_col))
    acs = _dot_exact_rhs(cum_l, a)
    acs_t = _dot_exact_lhs(a_t, cum_r)
    dt_x = _dot_exact_lhs(dt, expand)
    acs_x = _dot_exact_lhs(acs, expand)
    edge = 0 if rev else L - 1
    tot_x = acs_x[edge:edge + 1, :]

    xdt = x * dt_x
    y = _dot(cm, state.astype(BF16)) * jnp.exp(acs_x)
    cb = _dot(cm, bt)
    lane = lax.broadcasted_iota(jnp.int32, (L, LANES), 1)
    tiles = []
    for t in range(GROUP_WIDTH // LANES):
        xt = xdt[:, t * LANES:(t + 1) * LANES]
        acc = None
        for half in range(LANES // SSM_HEAD_DIM):
            j = t * (LANES // SSM_HEAD_DIM) + half
            seg = acs[:, j:j + 1] - acs_t[j:j + 1, :]
            gmat = (cb * jnp.exp(jnp.where(keep, seg, -jnp.inf))).astype(BF16)
            in_head = (lane >= half * SSM_HEAD_DIM) & (lane < (half + 1) * SSM_HEAD_DIM)
            part = _dot(gmat, jnp.where(in_head, xt, 0.0).astype(BF16))
            acc = part if acc is None else acc + part
        tiles.append(acc)
    y = y + jnp.concatenate(tiles, axis=1)

    w = (xdt * jnp.exp(tot_x - acs_x)).astype(BF16)
    new_state = jnp.exp(tot_x) * state + _dot(bt, w)
    return y, new_state


def _ssd_kernel(rev, nchunk, *refs):
    if rev:
        (xs_ref, bt_ref, c_ref, dt_ref, dtT_ref, alr_ref, alc_ref, br_ref, bc_ref,
         yf_ref, z_ref, ng_ref, y_ref, state_ref) = refs
    else:
        (xs_ref, bt_ref, c_ref, dt_ref, dtT_ref, alr_ref, alc_ref, br_ref, bc_ref,
         dx_ref, y_ref, state_ref) = refs

    @pl.when(pl.program_id(2) == 0)
    def _():
        state_ref[...] = jnp.zeros_like(state_ref)

    order = range(nchunk - 1, -1, -1) if rev else range(nchunk)
    for ci in order:
        rows = slice(ci * CHUNK, (ci + 1) * CHUNK)
        x = xs_ref[rows, :]
        y, new_state = _ssd_chunk(rev, x, bt_ref[:, rows], c_ref[rows, :], dt_ref[rows, :HEADS_PER_GROUP], dtT_ref[:, rows],
                                  alr_ref[...], alc_ref[...], br_ref[...], bc_ref[...], state_ref[...])
        state_ref[...] = new_state
        if rev:
            yy = (yf_ref[rows, :] + y) * _silu(z_ref[rows, :])
            ms = jnp.mean(yy * yy, axis=-1, keepdims=True)
            y_ref[rows, :] = yy * lax.rsqrt(ms + RMS_EPS) * ng_ref[...]
        else:
            y_ref[rows, :] = y + dx_ref[...] * x


def _ssd_pass(rev, xs, bt, cm, dt, dtT, a_log, dt_bias, extra, nchunk=4):
    b, s, _ = xs.shape
    tcs = nchunk * CHUNK
    nblk = s // tcs
    blk = (lambda i: nblk - 1 - i) if rev else (lambda i: i)
    dirn = 1 if rev else 0
    gw, hg = GROUP_WIDTH, HEADS_PER_GROUP
    seq_spec = lambda w: pl.BlockSpec((None, tcs, w), lambda bi, g, i: (bi, blk(i), g))
    in_specs = [
        seq_spec(gw),
        pl.BlockSpec((None, None, D_STATE, tcs), lambda bi, g, i: (bi, g, 0, blk(i))),
        seq_spec(D_STATE),
        pl.BlockSpec((tcs, LANES), lambda bi, g, i: (bi * nblk + blk(i), dirn * SSM_GROUPS + g)),
        pl.BlockSpec((None, hg, tcs), lambda bi, g, i: (dirn * SSM_GROUPS + g, 0, bi * nblk + blk(i))),
        pl.BlockSpec((None, 1, hg), lambda bi, g, i: (g, 0, 0)),
        pl.BlockSpec((None, hg, 1), lambda bi, g, i: (g, 0, 0)),
        pl.BlockSpec((None, 1, hg), lambda bi, g, i: (g, 0, 0)),
        pl.BlockSpec((None, hg, 1), lambda bi, g, i: (g, 0, 0)),
    ]
    args = [xs, bt, cm, dt, dtT,
            a_log.reshape(SSM_GROUPS, 1, hg), a_log.reshape(SSM_GROUPS, hg, 1),
            dt_bias.reshape(SSM_GROUPS, 1, hg), dt_bias.reshape(SSM_GROUPS, hg, 1)]
    group_row = pl.BlockSpec((None, 1, gw), lambda bi, g, i: (g, 0, 0))
    if rev:
        y_fwd, z, norm_g = extra
        in_specs += [seq_spec(gw), seq_spec(gw), group_row]
        args += [y_fwd, z, norm_g.reshape(SSM_GROUPS, 1, gw)]
    else:
        (d_skip,) = extra
        in_specs += [group_row]
        args += [jnp.repeat(d_skip, SSM_HEAD_DIM).reshape(SSM_GROUPS, 1, gw)]
    return pl.pallas_call(
        functools.partial(_ssd_kernel, rev, nchunk),
        grid=(b, SSM_GROUPS, nblk),
        in_specs=in_specs,
        out_specs=seq_spec(gw),
        out_shape=jax.ShapeDtypeStruct((b, s, D_SSM), F32),
        scratch_shapes=[pltpu.VMEM((D_STATE, gw), F32)],
        compiler_params=pltpu.CompilerParams(dimension_semantics=("parallel", "parallel", "arbitrary"),
                                             vmem_limit_bytes=VMEM_LIMIT),
        name="ssd_bwd" if rev else "ssd_fwd",
    )(*args)


NEG_BIG = -1e30


SKIP_LOG2 = 80.0
FIXED_SHIFT_MAX_LOG2 = 60.0


def _attn_kernel(tq, tk, nk, slopes_ref, kn_ref, qn_ref, qT_ref, k_ref, vT_ref, lq1_ref, lk1_ref, lq2_ref, lk2_ref,
                 sg_ref, o_ref, e_ref, s_ref, s1_ref, acc_ref, l_ref):
    bi, hi, qi = pl.program_id(0), pl.program_id(1), pl.program_id(2)
    nq = pl.num_programs(2)
    slope = slopes_ref[hi]
    ratio = tk // tq
    i0 = qi * tq
    jd = qi // ratio
    dsel = qi % ratio

    parts = tq // ATTN_QPART
    qns, bases = [], []
    for r in range(2):
        base = (bi * ATTN_HEADS + hi) * 2 + r
        qn = qn_ref[base * (nq * parts) + qi * parts]
        for part in range(1, parts):
            qn = jnp.maximum(qn, qn_ref[base * (nq * parts) + qi * parts + part])
        qns.append(qn)
        bases.append(base * nk)
    jlo, jhi = jd, jd
    kmax = [kn_ref[bases[0]], kn_ref[bases[1]]]
    for jt in range(nk):
        dist = jnp.maximum(jnp.maximum(i0 - (jt * tk + tk - 1), jt * tk - (i0 + tq - 1)), 0).astype(F32)
        need = None
        for r in range(2):
            kn = kn_ref[bases[r] + jt]
            kmax[r] = jnp.maximum(kmax[r], kn)
            reach = SKIP_LOG2 + qns[r] * (kn + kn_ref[bases[r] + jd])
            need_r = slope * dist <= reach
            need = need_r if need is None else (need | need_r)
        jlo = jnp.where(need, jnp.minimum(jlo, jt), jlo)
        jhi = jnp.where(need, jnp.maximum(jhi, jt), jhi)
    score_bound = jnp.maximum(qns[0] * kmax[0], qns[1] * kmax[1])

    @pl.when(qi == 0)
    def _():
        ii = lax.broadcasted_iota(jnp.int32, (tk, tq), 1)
        jj = lax.broadcasted_iota(jnp.int32, (tk, tq), 0)
        rel = (ii - jj).astype(F32)
        e_ref[0] = -slope * (rel + float(tk - 1))
        e_ref[1] = slope * (rel - float(tq - 1))
        for d in range(ratio):
            e_ref[2 + d] = -slope * jnp.abs(rel + float(d * tq))
        s_ref[...] = jnp.zeros_like(s_ref)

    row = lax.broadcasted_iota(jnp.int32, (2 * ATTN_HEAD_DIM, tq), 0)
    qf = qT_ref[...].astype(F32)
    rhs = (jnp.where(row < ATTN_HEAD_DIM, qf, 0.0).astype(BF16),
           jnp.where(row >= ATTN_HEAD_DIM, qf, 0.0).astype(BF16))
    acc_ref[...] = jnp.zeros_like(acc_ref)

    def tile_bias(j):
        off = (i0 - j * tk).astype(F32)
        before, after = j < jd, j > jd
        eidx = jnp.where(before, 0, jnp.where(after, 1, 2 + dsel))
        cj = jnp.where(before, -slope * (off - float(tk - 1)), jnp.where(after, slope * (off + float(tq - 1)), 0.0))
        return eidx, cj

    @pl.when(score_bound <= FIXED_SHIFT_MAX_LOG2)
    def _():
        jl = (jlo // 2) * 2
        jh = (jhi // 2) * 2 + 1

        def stage_a(j, buf):
            kt = k_ref[pl.ds(pl.multiple_of(j * tk, tk), tk), :]
            for r in range(2):
                buf[r] = _dot(kt, rhs[r])

        def stage_b(j, buf, l):
            eidx, cj = tile_bias(j)
            half = jnp.exp2(jnp.full((1, tq), 0.5 * cj, F32))
            vt = vT_ref[:, pl.ds(pl.multiple_of(j * tk, tk), tk)]
            l_new = []
            for r in range(2):
                p = jnp.exp2(buf[r] + e_ref[eidx])
                l_new.append(l[r] + jnp.sum(p, axis=0, keepdims=True) * half * half)
                acc_ref[r] += _dot(vt, p.astype(BF16)) * half * half
            return tuple(l_new)

        def body(t, l):
            j = jl + 2 * t
            stage_a(j + 1, s1_ref)
            l = stage_b(j, s_ref, l)
            stage_a(jnp.minimum(j + 2, jh), s_ref)
            return stage_b(j + 1, s1_ref, l)

        stage_a(jl, s_ref)
        zero = jnp.zeros((1, tq), F32)
        l = lax.fori_loop(0, (jh - jl + 1) // 2, body, (zero, zero))
        l_ref[0] = l[0]
        l_ref[1] = l[1]

    def stage_a(j, m_old):
        eidx, cj = tile_bias(j)
        kt = k_ref[pl.ds(pl.multiple_of(j * tk, tk), tk), :]
        m_new, shift, alpha = [], [], []
        for r in range(2):
            s = _dot(kt, rhs[r]) + e_ref[eidx]
            s_ref[r] = s
            mr = jnp.maximum(m_old[r], jnp.max(s, axis=0, keepdims=True) + cj)
            m_new.append(mr)
            shift.append(mr - cj)
            alpha.append(jnp.exp2(m_old[r] - mr))
        return tuple(m_new), tuple(shift), tuple(alpha)

    def stage_b(j, shift, alpha, l_old):
        vt = vT_ref[:, pl.ds(pl.multiple_of(j * tk, tk), tk)]
        l_new = []
        for r in range(2):
            p = jnp.exp2(s_ref[r] - shift[r])
            l_new.append(alpha[r] * l_old[r] + jnp.sum(p, axis=0, keepdims=True))
            acc_ref[r] = alpha[r] * acc_ref[r] + _dot(vt, p.astype(BF16))
        return tuple(l_new)

    def body(t, carry):
        m, shift, alpha, l = carry
        j = jlo + t
        l = stage_b(jnp.maximum(j - 1, jlo), shift, alpha, l)
        m, shift, alpha = stage_a(jnp.minimum(j, jhi), m)
        return m, shift, alpha, l

    @pl.when(score_bound > FIXED_SHIFT_MAX_LOG2)
    def _():
        neg = jnp.full((1, tq), NEG_BIG, F32)
        big = jnp.full((1, tq), -NEG_BIG, F32)
        one = jnp.ones((1, tq), F32)
        zero = jnp.zeros((1, tq), F32)
        init = ((neg, neg), (big, big), (one, one), (zero, zero))
        _, _, _, l = lax.fori_loop(0, jhi - jlo + 2, body, init)
        l_ref[0] = l[0]
        l_ref[1] = l[1]

    lam = (jnp.exp(jnp.sum(lq1_ref[...] * lk1_ref[...], axis=-1, keepdims=True))
           - jnp.exp(jnp.sum(lq2_ref[...] * lk2_ref[...], axis=-1, keepdims=True)) + LAM_INIT)
    o = acc_ref[0] / l_ref[0] - lam * (acc_ref[1] / l_ref[1])
    ms = jnp.mean(o * o, axis=0, keepdims=True)
    o_ref[...] = (o * lax.rsqrt(ms + RMS_EPS) * sg_ref[...] * (1.0 - LAM_INIT)).T


def _diff_attn(qT, k3, vT, kn, qn, lq1, lk1, lq2, lk2, subln_g, tq=ATTN_TQ, tk=ATTN_TK):
    b, s, _ = k3.shape
    nq, nk = s // tq, s // tk
    ratio = tk // tq
    assert tk % tq == 0 and nk % 2 == 0
    hw = 2 * ATTN_HEAD_DIM
    slopes = jnp.exp2(-8.0 * (jnp.arange(ATTN_HEADS, dtype=F32) + 1.0) / ATTN_HEADS) * LOG2E
    kn_tab = jnp.transpose(kn[:, 0, :N_MAPS].reshape(b, nk, N_MAPS), (0, 2, 1)).reshape(-1)
    kparts = tk // ATTN_QPART
    qn_tab = jnp.transpose(qn[:, :, :kparts].reshape(b, nk, N_MAPS, kparts), (0, 2, 1, 3)).reshape(-1)
    vec = lambda a: a.reshape(1, -1)
    return pl.pallas_call(
        functools.partial(_attn_kernel, tq, tk, nk),
        grid=(b, ATTN_HEADS, nq),
        in_specs=[
            pl.BlockSpec(memory_space=pltpu.SMEM),
            pl.BlockSpec(memory_space=pltpu.SMEM),
            pl.BlockSpec(memory_space=pltpu.SMEM),
            pl.BlockSpec((hw, tq), lambda bi, hi, qi: (hi, bi * nq + qi)),
            pl.BlockSpec((None, s, hw), lambda bi, hi, qi: (bi, 0, hi)),
            pl.BlockSpec((ATTN_V_DIM, s), lambda bi, hi, qi: (hi, bi)),
            _const_spec((1, ATTN_HEAD_DIM)), _const_spec((1, ATTN_HEAD_DIM)),
            _const_spec((1, ATTN_HEAD_DIM)), _const_spec((1, ATTN_HEAD_DIM)),
            _const_spec((ATTN_V_DIM, 1)),
        ],
        out_specs=pl.BlockSpec((None, tq, ATTN_V_DIM), lambda bi, hi, qi: (bi, qi, hi)),
        out_shape=jax.ShapeDtypeStruct((b, s, ATTN_HEADS * ATTN_V_DIM), F32),
        scratch_shapes=[pltpu.VMEM((2 + tk // tq, tk, tq), F32),
                        pltpu.VMEM((2, tk, tq), F32), pltpu.VMEM((2, tk, tq), F32),
                        pltpu.VMEM((2, ATTN_V_DIM, tq), F32), pltpu.VMEM((2, 1, tq), F32)],
        compiler_params=pltpu.CompilerParams(dimension_semantics=("parallel", "parallel", "arbitrary"),
                                             vmem_limit_bytes=VMEM_LIMIT),
        name="diff_attn",
    )(slopes, kn_tab, qn_tab, qT, k3, vT, vec(lq1), vec(lk1), vec(lq2), vec(lk2), subln_g.reshape(-1, 1))


def _out_proj_kernel(x_ref, eg_ref, eb_ref, ys_ref, o_ref, gate_ref, w_ref, g_ref, b_ref, out_ref):
    h = _layer_norm(x_ref[...], eg_ref[...], eb_ref[...])
    ya = o_ref[...] * _silu(gate_ref[...])
    mix = _dot(ys_ref[...].astype(BF16), w_ref[:D_SSM, :]) + _dot(ya.astype(BF16), w_ref[D_SSM:, :])
    out_ref[...] = _layer_norm(ALPHA * h + mix, g_ref[...], b_ref[...])


def _out_proj(x2d, ln_emb_g, ln_emb_b, y_ssm, o, gate, w_out, ln_g, ln_b, tm=512):
    t = x2d.shape[0]
    row = lambda n: pl.BlockSpec((tm, n), lambda i: (i, 0))
    vec = lambda a: a.reshape(1, -1)
    return pl.pallas_call(
        _out_proj_kernel,
        grid=(t // tm,),
        in_specs=[row(D_MODEL), _const_spec((1, D_MODEL)), _const_spec((1, D_MODEL)),
                  row(D_SSM), row(D_ATTN), row(D_ATTN),
                  _const_spec((D_SSM + D_ATTN, D_MODEL)), _const_spec((1, D_MODEL)), _const_spec((1, D_MODEL))],
        out_specs=row(D_MODEL),
        out_shape=jax.ShapeDtypeStruct((t, D_MODEL), F32),
        compiler_params=pltpu.CompilerParams(dimension_semantics=("parallel",), vmem_limit_bytes=VMEM_LIMIT),
        name="out_proj",
    )(x2d, vec(ln_emb_g), vec(ln_emb_b), y_ssm, o, gate, w_out.astype(BF16), vec(ln_g), vec(ln_b))


def kernel(x, ln_emb_g, ln_emb_b, w_in, conv_w, conv_b, A_log_fwd, A_log_bwd, dt_bias_fwd, dt_bias_bwd, D_skip,
           ssm_norm_g, lambda_q1, lambda_k1, lambda_q2, lambda_k2, subln_g, w_out, ln_g, ln_b):
    b, s, _ = x.shape
    t = b * s
    x2d = x.reshape(t, D_MODEL)
    z, xbc, dt, dtT, qT, k, vT, gate, kn, qn = _ln_inproj(x2d, ln_emb_g, ln_emb_b, w_in[0])

    xs, bt, cm = _conv_silu(xbc.reshape(b, s, D_XBC), conv_w[0], conv_b[0])
    y_fwd = _ssd_pass(False, xs, bt, cm, dt, dtT, A_log_fwd[0], dt_bias_fwd[0], (D_skip[0],))
    y_ssm = _ssd_pass(True, xs, bt, cm, dt, dtT, A_log_bwd[0], dt_bias_bwd[0],
                      (y_fwd, z.reshape(b, s, D_SSM), ssm_norm_g[0]))

    o = _diff_attn(qT, k.reshape(b, s, D_ATTN), vT, kn, qn, lambda_q1[0], lambda_k1[0], lambda_q2[0],
                   lambda_k2[0], subln_g[0]).reshape(t, D_ATTN)

    out = _out_proj(x2d, ln_emb_g, ln_emb_b, y_ssm.reshape(t, D_SSM), o, gate, w_out[0], ln_g[0], ln_b[0])
    return out.reshape(b, s, D_MODEL)
```

```python
import functools
import math

import jax
import jax.numpy as jnp
from jax import lax
from jax.experimental import pallas as pl
from jax.experimental.pallas import tpu as pltpu

D_MODEL = 1024
D_SSM = 1024
SSM_HEAD_DIM = 64
SSM_HEADS = 16
SSM_GROUPS = 2
HEADS_PER_GROUP = SSM_HEADS // SSM_GROUPS
D_STATE = 128
D_CONV = 5
CHUNK = 128
D_XBC = D_SSM + 2 * SSM_GROUPS * D_STATE
D_ATTN = 1024
ATTN_HEADS = 8
ATTN_HEAD_DIM = 64
ATTN_V_DIM = 128
GROUP_WIDTH = D_SSM // SSM_GROUPS
DEPTH = 1
ALPHA = (2.0 * DEPTH) ** 0.25
LN_EPS = 1e-5
RMS_EPS = 1e-5
LAM_INIT = 0.8 - 0.6 * math.exp(-0.3 * 0)

LANES = 128
VMEM_LIMIT = 48 * 1024 * 1024

F32 = jnp.float32
BF16 = jnp.bfloat16


LOG2E = math.log2(math.e)


def _dot(a, b):
    return jnp.dot(a, b, preferred_element_type=F32)


def _dot_nt(a, b):
    return lax.dot_general(a, b, (((1,), (1,)), ((), ())), preferred_element_type=F32)


def _layer_norm(x, g, b):
    mu = jnp.mean(x, axis=-1, keepdims=True)
    xc = x - mu
    var = jnp.mean(xc * xc, axis=-1, keepdims=True)
    return xc * lax.rsqrt(var + LN_EPS) * g + b


def _silu(x):
    return x / (1.0 + jnp.exp(-x))


def _softplus(x):
    return jnp.maximum(x, 0.0) + jnp.log1p(jnp.exp(-jnp.abs(x)))


def _split3(x):
    hi = x.astype(BF16)
    r = x - hi.astype(F32)
    mid = r.astype(BF16)
    lo = (r - mid.astype(F32)).astype(BF16)
    return hi, mid, lo


def _dot_exact_rhs(m, x):
    hi, mid, lo = _split3(x)
    return _dot(m, hi) + _dot(m, mid) + _dot(m, lo)


def _dot_exact_lhs(x, m):
    hi, mid, lo = _split3(x)
    return _dot(hi, m) + _dot(mid, m) + _dot(lo, m)


def _expand_heads(x, expand2):
    hi = x.astype(BF16).astype(F32)
    pieces = jnp.concatenate([hi, x - hi], axis=1).astype(BF16)
    return _dot(pieces, expand2)


def _const_spec(shape):
    nd = len(shape)
    return pl.BlockSpec(shape, lambda *_: (0,) * nd)


N_DT_TILES = 2 * SSM_GROUPS


ATTN_TQ = 256
ATTN_QPART = 256
ATTN_TK = 512
N_MAPS = 2 * ATTN_HEADS
NORM_SLACK = 1.01


def _ln_inproj_kernel(x_ref, g_ref, b_ref, wz_ref, wxbc_ref, wdt_ref, wqT_ref, wk_ref, wvT_ref, wg_ref,
                      grp_ref, grpT_ref,
                      z_ref, xbc_ref, dt_ref, dtT_ref, qT_ref, k_ref, vT_ref, gate_ref, kn_ref, qn_ref):
    h = _layer_norm(x_ref[...], g_ref[...], b_ref[...]).astype(BF16)
    z_ref[...] = _dot(h, wz_ref[...])
    xbc_ref[...] = _dot(h, wxbc_ref[...])
    dt = _dot(h, wdt_ref[...])
    dt_ref[...] = dt
    for i in range(N_DT_TILES):
        dtT_ref[i] = dt[:, i * LANES:(i + 1) * LANES].T[:HEADS_PER_GROUP, :]
    qT = _dot_nt(wqT_ref[...], h) * (LOG2E * ATTN_HEAD_DIM ** -0.5)
    qT_ref[...] = qT.astype(BF16)
    k = _dot(h, wk_ref[...])
    k_ref[...] = k.astype(BF16)
    vT_ref[...] = _dot_nt(wvT_ref[...], h).astype(BF16)
    gate_ref[...] = _dot(h, wg_ref[...])
    k2 = _dot((k * k).astype(BF16), grp_ref[...])
    kn_ref[...] = jnp.sqrt(jnp.max(k2, axis=0, keepdims=True)) * NORM_SLACK
    q2 = _dot(grpT_ref[...], (qT * qT).astype(BF16))[:N_MAPS, :]
    lane = lax.broadcasted_iota(jnp.int32, (N_MAPS, LANES), 1)
    qn = jnp.zeros((N_MAPS, LANES), F32)
    for part in range(q2.shape[1] // ATTN_QPART):
        pm = jnp.max(q2[:, part * ATTN_QPART:(part + 1) * ATTN_QPART], axis=1, keepdims=True)
        qn = jnp.where(lane == part, jnp.sqrt(pm) * NORM_SLACK, qn)
    qn_ref[...] = qn


def _ln_inproj(x2d, ln_g, ln_b, w_in, tm=ATTN_TK):
    t = x2d.shape[0]
    grp = (jnp.arange(D_ATTN)[:, None] // ATTN_HEAD_DIM == jnp.arange(LANES)[None, :]).astype(BF16)
    offs = [0, D_SSM, D_SSM + D_XBC, D_SSM + D_XBC + 2 * SSM_HEADS]
    wz = w_in[:, offs[0]:offs[1]].astype(BF16)
    wxbc = w_in[:, offs[1]:offs[2]].astype(BF16)
    wdt = w_in[:, offs[2]:offs[3]].reshape(D_MODEL, N_DT_TILES, HEADS_PER_GROUP)
    wdt = jnp.pad(wdt, ((0, 0), (0, 0), (0, LANES - HEADS_PER_GROUP))).reshape(D_MODEL, -1).astype(BF16)
    o = offs[3]
    wq, wk, wv, wg = (w_in[:, o + i * D_ATTN:o + (i + 1) * D_ATTN].astype(BF16) for i in range(4))
    row = lambda n: pl.BlockSpec((tm, n), lambda i: (i, 0))
    col = lambda n: pl.BlockSpec((n, tm), lambda i: (0, i))
    weights = [wz, wxbc, wdt, wq.T, wk, wv.T, wg, grp, grp.T]
    nt = t // tm
    return pl.pallas_call(
        _ln_inproj_kernel,
        grid=(nt,),
        in_specs=[row(D_MODEL), _const_spec((1, D_MODEL)), _const_spec((1, D_MODEL))]
                 + [pl.BlockSpec(w.shape, lambda i: (0, 0), pipeline_mode=pl.Buffered(1)) for w in weights],
        out_specs=[row(D_SSM), row(D_XBC), row(N_DT_TILES * LANES),
                   pl.BlockSpec((N_DT_TILES, HEADS_PER_GROUP, tm), lambda i: (0, 0, i)),
                   col(D_ATTN), row(D_ATTN), col(D_ATTN), row(D_ATTN),
                   pl.BlockSpec((None, 1, LANES), lambda i: (i, 0, 0)),
                   pl.BlockSpec((None, N_MAPS, LANES), lambda i: (i, 0, 0))],
        out_shape=[jax.ShapeDtypeStruct((t, D_SSM), F32), jax.ShapeDtypeStruct((t, D_XBC), F32),
                   jax.ShapeDtypeStruct((t, N_DT_TILES * LANES), F32),
                   jax.ShapeDtypeStruct((N_DT_TILES, HEADS_PER_GROUP, t), F32),
                   jax.ShapeDtypeStruct((D_ATTN, t), BF16), jax.ShapeDtypeStruct((t, D_ATTN), BF16),
                   jax.ShapeDtypeStruct((D_ATTN, t), BF16), jax.ShapeDtypeStruct((t, D_ATTN), F32),
                   jax.ShapeDtypeStruct((nt, 1, LANES), F32), jax.ShapeDtypeStruct((nt, N_MAPS, LANES), F32)],
        compiler_params=pltpu.CompilerParams(dimension_semantics=("parallel",), vmem_limit_bytes=VMEM_LIMIT),
        name="ln_inproj",
    )(x2d, ln_g.reshape(1, -1), ln_b.reshape(1, -1), *weights)


HALO = 8


def _conv_silu_kernel(prev_ref, cur_ref, next_ref, w_ref, b_ref, xs_ref, bt_ref, c_ref, ext_ref):
    i = pl.program_id(1)
    n = pl.num_programs(1)
    tc = cur_ref.shape[0]
    ext_ref[0:HALO, :] = jnp.where(i == 0, 0.0, prev_ref[...])
    ext_ref[HALO:HALO + tc, :] = cur_ref[...]
    ext_ref[HALO + tc:, :] = jnp.where(i == n - 1, 0.0, next_ref[...])
    acc = jnp.zeros((tc, D_XBC), F32) + b_ref[...]
    for kk in range(D_CONV):
        acc = acc + ext_ref[pl.ds(HALO - D_CONV // 2 + kk, tc), :] * w_ref[kk:kk + 1, :]
    act = _silu(acc)
    xs_ref[...] = act[:, :D_SSM]
    for g in range(SSM_GROUPS):
        lo = D_SSM + g * D_STATE
        bt_ref[g] = act[:, lo:lo + D_STATE].T.astype(BF16)
    c_ref[...] = act[:, D_SSM + SSM_GROUPS * D_STATE:].astype(BF16)


def _conv_silu(xbc, conv_w, conv_b, tc=512):
    b, s, _ = xbc.shape
    nb = s // tc
    hb = tc // HALO
    return pl.pallas_call(
        _conv_silu_kernel,
        grid=(b, nb),
        in_specs=[
            pl.BlockSpec((None, HALO, D_XBC), lambda bi, i: (bi, jnp.maximum(i * hb - 1, 0), 0)),
            pl.BlockSpec((None, tc, D_XBC), lambda bi, i: (bi, i, 0)),
            pl.BlockSpec((None, HALO, D_XBC), lambda bi, i: (bi, jnp.minimum((i + 1) * hb, s // HALO - 1), 0)),
            _const_spec((D_CONV, D_XBC)),
            _const_spec((1, D_XBC)),
        ],
        out_specs=[
            pl.BlockSpec((None, tc, D_SSM), lambda bi, i: (bi, i, 0)),
            pl.BlockSpec((None, SSM_GROUPS, D_STATE, tc), lambda bi, i: (bi, 0, 0, i)),
            pl.BlockSpec((None, tc, SSM_GROUPS * D_STATE), lambda bi, i: (bi, i, 0)),
        ],
        out_shape=[
            jax.ShapeDtypeStruct((b, s, D_SSM), F32),
            jax.ShapeDtypeStruct((b, SSM_GROUPS, D_STATE, s), BF16),
            jax.ShapeDtypeStruct((b, s, SSM_GROUPS * D_STATE), BF16),
        ],
        scratch_shapes=[pltpu.VMEM((tc + 2 * HALO, D_XBC), F32)],
        compiler_params=pltpu.CompilerParams(dimension_semantics=("parallel", "parallel"),
                                             vmem_limit_bytes=VMEM_LIMIT),
        name="conv_silu",
    )(xbc, xbc, xbc, conv_w, conv_b.reshape(1, -1))


def _ssd_chunk(rev, x, bt, cm, dt_raw, dtT_raw, alog_row, alog_col, bias_row, bias_col, state):
    L = CHUNK
    r = lax.broadcasted_iota(jnp.int32, (L, L), 0)
    c = lax.broadcasted_iota(jnp.int32, (L, L), 1)
    keep = (c >= r) if rev else (c <= r)
    cum_l = keep.astype(BF16)
    cum_r = ((r >= c) if rev else (r <= c)).astype(BF16)
    eh = lax.broadcasted_iota(jnp.int32, (2 * HEADS_PER_GROUP, GROUP_WIDTH), 0)
    el = lax.broadcasted_iota(jnp.int32, (2 * HEADS_PER_GROUP, GROUP_WIDTH), 1)
    expand2 = (el // SSM_HEAD_DIM == eh % HEADS_PER_GROUP).astype(BF16)

    dt = _softplus(dt_raw + bias_row)
    a = dt * (-jnp.exp(alog_row))
    dt_t = _softplus(dtT_raw + bias_col)
    a_t = dt_t * (-jnp.exp(alog_col))
    acs = _dot_exact_rhs(cum_l, a)
    acs_t = _dot_exact_lhs(a_t, cum_r)
    dt_x = _expand_heads(dt, expand2)
    acs_x = _expand_heads(acs, expand2)
    edge = 0 if rev else L - 1
    tot_x = acs_x[edge:edge + 1, :]

    xdt = x * dt_x
    y = _dot(cm, state.astype(BF16)) * jnp.exp(acs_x)
    cb = _dot(cm, bt)
    lane = lax.broadcasted_iota(jnp.int32, (L, LANES), 1)
    tiles = []
    for t in range(GROUP_WIDTH // LANES):
        xt = xdt[:, t * LANES:(t + 1) * LANES]
        acc = None
        for half in range(LANES // SSM_HEAD_DIM):
            j = t * (LANES // SSM_HEAD_DIM) + half
            seg = acs[:, j:j + 1] - acs_t[j:j + 1, :]
            gmat = (cb * jnp.exp(jnp.where(keep, seg, -jnp.inf))).astype(BF16)
            in_head = (lane >= half * SSM_HEAD_DIM) & (lane < (half + 1) * SSM_HEAD_DIM)
            part = _dot(gmat, jnp.where(in_head, xt, 0.0).astype(BF16))
            acc = part if acc is None else acc + part
        tiles.append(acc)
    y = y + jnp.concatenate(tiles, axis=1)

    w = (xdt * jnp.exp(tot_x - acs_x)).astype(BF16)
    new_state = jnp.exp(tot_x) * state + _dot(bt, w)
    return y, new_state


def _ssd_kernel(rev, nchunk, *refs):
    if rev:
        (xs_ref, bt_ref, c_ref, dt_ref, dtT_ref, alr_ref, alc_ref, br_ref, bc_ref,
         yf_ref, z_ref, ng_ref, y_ref, state_ref) = refs
    else:
        (xs_ref, bt_ref, c_ref, dt_ref, dtT_ref, alr_ref, alc_ref, br_ref, bc_ref,
         dx_ref, y_ref, state_ref) = refs

    @pl.when(pl.program_id(2) == 0)
    def _():
        state_ref[...] = jnp.zeros_like(state_ref)

    order = range(nchunk - 1, -1, -1) if rev else range(nchunk)
    for ci in order:
        rows = slice(ci * CHUNK, (ci + 1) * CHUNK)
        x = xs_ref[rows, :]
        y, new_state = _ssd_chunk(rev, x, bt_ref[:, rows], c_ref[rows, :], dt_ref[rows, :HEADS_PER_GROUP], dtT_ref[:, rows],
                                  alr_ref[...], alc_ref[...], br_ref[...], bc_ref[...], state_ref[...])
        state_ref[...] = new_state
        if rev:
            yy = (yf_ref[rows, :] + y) * _silu(z_ref[rows, :])
            ms = jnp.mean(yy * yy, axis=-1, keepdims=True)
            y_ref[rows, :] = yy * lax.rsqrt(ms + RMS_EPS) * ng_ref[...]
        else:
            y_ref[rows, :] = y + dx_ref[...] * x


def _ssd_pass(rev, xs, bt, cm, dt, dtT, a_log, dt_bias, extra, nchunk=4):
    b, s, _ = xs.shape
    tcs = nchunk * CHUNK
    nblk = s // tcs
    blk = (lambda i: nblk - 1 - i) if rev else (lambda i: i)
    dirn = 1 if rev else 0
    gw, hg = GROUP_WIDTH, HEADS_PER_GROUP
    seq_spec = lambda w: pl.BlockSpec((None, tcs, w), lambda bi, g, i: (bi, blk(i), g))
    in_specs = [
        seq_spec(gw),
        pl.BlockSpec((None, None, D_STATE, tcs), lambda bi, g, i: (bi, g, 0, blk(i))),
        seq_spec(D_STATE),
        pl.BlockSpec((tcs, LANES), lambda bi, g, i: (bi * nblk + blk(i), dirn * SSM_GROUPS + g)),
        pl.BlockSpec((None, hg, tcs), lambda bi, g, i: (dirn * SSM_GROUPS + g, 0, bi * nblk + blk(i))),
        pl.BlockSpec((None, 1, hg), lambda bi, g, i: (g, 0, 0)),
        pl.BlockSpec((None, hg, 1), lambda bi, g, i: (g, 0, 0)),
        pl.BlockSpec((None, 1, hg), lambda bi, g, i: (g, 0, 0)),
        pl.BlockSpec((None, hg, 1), lambda bi, g, i: (g, 0, 0)),
    ]
    args = [xs, bt, cm, dt, dtT,
            a_log.reshape(SSM_GROUPS, 1, hg), a_log.reshape(SSM_GROUPS, hg, 1),
            dt_bias.reshape(SSM_GROUPS, 1, hg), dt_bias.reshape(SSM_GROUPS, hg, 1)]
    group_row = pl.BlockSpec((None, 1, gw), lambda bi, g, i: (g, 0, 0))
    if rev:
        y_fwd, z, norm_g = extra
        in_specs += [seq_spec(gw), seq_spec(gw), group_row]
        args += [y_fwd, z, norm_g.reshape(SSM_GROUPS, 1, gw)]
    else:
        (d_skip,) = extra
        in_specs += [group_row]
        args += [jnp.repeat(d_skip, SSM_HEAD_DIM).reshape(SSM_GROUPS, 1, gw)]
    return pl.pallas_call(
        functools.partial(_ssd_kernel, rev, nchunk),
        grid=(b, SSM_GROUPS, nblk),
        in_specs=in_specs,
        out_specs=seq_spec(gw),
        out_shape=jax.ShapeDtypeStruct((b, s, D_SSM), F32),
        scratch_shapes=[pltpu.VMEM((D_STATE, gw), F32)],
        compiler_params=pltpu.CompilerParams(dimension_semantics=("parallel", "parallel", "arbitrary"),
                                             vmem_limit_bytes=VMEM_LIMIT),
        name="ssd_bwd" if rev else "ssd_fwd",
    )(*args)


NEG_BIG = -1e30


SKIP_LOG2 = 80.0
FIXED_SHIFT_MAX_LOG2 = 60.0


def _attn_q_tile(tq, tk, nk, nq, bi, hi, qi, slope, kn_ref, qn_ref, qT_ref, k_ref, vT_ref, lq1_ref, lk1_ref, lq2_ref,
                 lk2_ref, sg_ref, o_ref, e_ref, s_ref, s1_ref, acc_ref, l_ref):
    ratio = tk // tq
    i0 = qi * tq
    q0 = pl.multiple_of(i0, tq)
    jd = qi // ratio
    dsel = qi % ratio

    parts = tq // ATTN_QPART
    qns, bases = [], []
    for r in range(2):
        base = (bi * ATTN_HEADS + hi) * 2 + r
        qn = qn_ref[base * (nq * parts) + qi * parts]
        for part in range(1, parts):
            qn = jnp.maximum(qn, qn_ref[base * (nq * parts) + qi * parts + part])
        qns.append(qn)
        bases.append(base * nk)
    jlo, jhi = jd, jd
    kmax = [kn_ref[bases[0]], kn_ref[bases[1]]]
    for jt in range(nk):
        dist = jnp.maximum(jnp.maximum(i0 - (jt * tk + tk - 1), jt * tk - (i0 + tq - 1)), 0).astype(F32)
        need = None
        for r in range(2):
            kn = kn_ref[bases[r] + jt]
            kmax[r] = jnp.maximum(kmax[r], kn)
            reach = SKIP_LOG2 + qns[r] * (kn + kn_ref[bases[r] + jd])
            need_r = slope * dist <= reach
            need = need_r if need is None else (need | need_r)
        jlo = jnp.where(need, jnp.minimum(jlo, jt), jlo)
        jhi = jnp.where(need, jnp.maximum(jhi, jt), jhi)
    score_bound = jnp.maximum(qns[0] * kmax[0], qns[1] * kmax[1])

    row = lax.broadcasted_iota(jnp.int32, (2 * ATTN_HEAD_DIM, tq), 0)
    qf = qT_ref[:, pl.ds(q0, tq)].astype(F32)
    rhs = (jnp.where(row < ATTN_HEAD_DIM, qf, 0.0).astype(BF16),
           jnp.where(row >= ATTN_HEAD_DIM, qf, 0.0).astype(BF16))
    acc_ref[...] = jnp.zeros_like(acc_ref)

    def tile_bias(j):
        off = (i0 - j * tk).astype(F32)
        before, after = j < jd, j > jd
        eidx = jnp.where(before, 0, jnp.where(after, 1, 2 + dsel))
        cj = jnp.where(before, -slope * (off - float(tk - 1)), jnp.where(after, slope * (off + float(tq - 1)), 0.0))
        return eidx, cj

    @pl.when(score_bound <= FIXED_SHIFT_MAX_LOG2)
    def _():
        jl = (jlo // 2) * 2
        jh = (jhi // 2) * 2 + 1

        def stage_a(j, buf):
            kt = k_ref[pl.ds(pl.multiple_of(j * tk, tk), tk), :]
            for r in range(2):
                buf[r] = _dot(kt, rhs[r])

        def stage_b(j, buf, l):
            eidx, cj = tile_bias(j)
            half = jnp.exp2(jnp.full((1, tq), 0.5 * cj, F32))
            vt = vT_ref[:, pl.ds(pl.multiple_of(j * tk, tk), tk)]
            l_new = []
            for r in range(2):
                p = jnp.exp2(buf[r] + e_ref[eidx])
                l_new.append(l[r] + jnp.sum(p, axis=0, keepdims=True) * half * half)
                acc_ref[r] += _dot(vt, p.astype(BF16)) * half * half
            return tuple(l_new)

        def body(t, l):
            j = jl + 2 * t
            stage_a(j + 1, s1_ref)
            l = stage_b(j, s_ref, l)
            stage_a(jnp.minimum(j + 2, jh), s_ref)
            return stage_b(j + 1, s1_ref, l)

        stage_a(jl, s_ref)
        zero = jnp.zeros((1, tq), F32)
        l = lax.fori_loop(0, (jh - jl + 1) // 2, body, (zero, zero))
        l_ref[0] = l[0]
        l_ref[1] = l[1]

    def stage_a(j, m_old):
        eidx, cj = tile_bias(j)
        kt = k_ref[pl.ds(pl.multiple_of(j * tk, tk), tk), :]
        m_new, shift, alpha = [], [], []
        for r in range(2):
            s = _dot(kt, rhs[r]) + e_ref[eidx]
            s_ref[r] = s
            mr = jnp.maximum(m_old[r], jnp.max(s, axis=0, keepdims=True) + cj)
            m_new.append(mr)
            shift.append(mr - cj)
            alpha.append(jnp.exp2(m_old[r] - mr))
        return tuple(m_new), tuple(shift), tuple(alpha)

    def stage_b(j, shift, alpha, l_old):
        vt = vT_ref[:, pl.ds(pl.multiple_of(j * tk, tk), tk)]
        l_new = []
        for r in range(2):
            p = jnp.exp2(s_ref[r] - shift[r])
            l_new.append(alpha[r] * l_old[r] + jnp.sum(p, axis=0, keepdims=True))
            acc_ref[r] = alpha[r] * acc_ref[r] + _dot(vt, p.astype(BF16))
        return tuple(l_new)

    def body(t, carry):
        m, shift, alpha, l = carry
        j = jlo + t
        l = stage_b(jnp.maximum(j - 1, jlo), shift, alpha, l)
        m, shift, alpha = stage_a(jnp.minimum(j, jhi), m)
        return m, shift, alpha, l

    @pl.when(score_bound > FIXED_SHIFT_MAX_LOG2)
    def _():
        neg = jnp.full((1, tq), NEG_BIG, F32)
        big = jnp.full((1, tq), -NEG_BIG, F32)
        one = jnp.ones((1, tq), F32)
        zero = jnp.zeros((1, tq), F32)
        init = ((neg, neg), (big, big), (one, one), (zero, zero))
        _, _, _, l = lax.fori_loop(0, jhi - jlo + 2, body, init)
        l_ref[0] = l[0]
        l_ref[1] = l[1]

    lam = (jnp.exp(jnp.sum(lq1_ref[...] * lk1_ref[...], axis=-1, keepdims=True))
           - jnp.exp(jnp.sum(lq2_ref[...] * lk2_ref[...], axis=-1, keepdims=True)) + LAM_INIT)
    o = acc_ref[0] / l_ref[0] - lam * (acc_ref[1] / l_ref[1])
    ms = jnp.mean(o * o, axis=0, keepdims=True)
    o_ref[pl.ds(q0, tq), :] = (o * lax.rsqrt(ms + RMS_EPS) * sg_ref[...] * (1.0 - LAM_INIT)).T


def _attn_kernel(tq, tk, nk, nq, slopes_ref, kn_ref, qn_ref, qT_ref, k_ref, vT_ref, lq1_ref, lk1_ref, lq2_ref,
                 lk2_ref, sg_ref, o_ref, e_ref, s_ref, s1_ref, acc_ref, l_ref):
    bi, hi = pl.program_id(0), pl.program_id(1)
    slope = slopes_ref[hi]
    ii = lax.broadcasted_iota(jnp.int32, (tk, tq), 1)
    jj = lax.broadcasted_iota(jnp.int32, (tk, tq), 0)
    rel = (ii - jj).astype(F32)
    e_ref[0] = -slope * (rel + float(tk - 1))
    e_ref[1] = slope * (rel - float(tq - 1))
    for d in range(tk // tq):
        e_ref[2 + d] = -slope * jnp.abs(rel + float(d * tq))
    s_ref[...] = jnp.zeros_like(s_ref)

    def q_tile(qi, carry):
        _attn_q_tile(tq, tk, nk, nq, bi, hi, qi, slope, kn_ref, qn_ref, qT_ref, k_ref, vT_ref, lq1_ref, lk1_ref,
                     lq2_ref, lk2_ref, sg_ref, o_ref, e_ref, s_ref, s1_ref, acc_ref, l_ref)
        return carry

    lax.fori_loop(0, nq, q_tile, 0)


def _diff_attn(qT, k3, vT, kn, qn, lq1, lk1, lq2, lk2, subln_g, tq=ATTN_TQ, tk=ATTN_TK):
    b, s, _ = k3.shape
    nq, nk = s // tq, s // tk
    ratio = tk // tq
    assert tk % tq == 0 and nk % 2 == 0
    hw = 2 * ATTN_HEAD_DIM
    slopes = jnp.exp2(-8.0 * (jnp.arange(ATTN_HEADS, dtype=F32) + 1.0) / ATTN_HEADS) * LOG2E
    kn_tab = jnp.transpose(kn[:, 0, :N_MAPS].reshape(b, nk, N_MAPS), (0, 2, 1)).reshape(-1)
    kparts = tk // ATTN_QPART
    qn_tab = jnp.transpose(qn[:, :, :kparts].reshape(b, nk, N_MAPS, kparts), (0, 2, 1, 3)).reshape(-1)
    vec = lambda a: a.reshape(1, -1)
    return pl.pallas_call(
        functools.partial(_attn_kernel, tq, tk, nk, nq),
        grid=(b, ATTN_HEADS),
        in_specs=[
            pl.BlockSpec(memory_space=pltpu.SMEM),
            pl.BlockSpec(memory_space=pltpu.SMEM),
            pl.BlockSpec(memory_space=pltpu.SMEM),
            pl.BlockSpec((hw, s), lambda bi, hi: (hi, bi)),
            pl.BlockSpec((None, s, hw), lambda bi, hi: (bi, 0, hi)),
            pl.BlockSpec((ATTN_V_DIM, s), lambda bi, hi: (hi, bi)),
            _const_spec((1, ATTN_HEAD_DIM)), _const_spec((1, ATTN_HEAD_DIM)),
            _const_spec((1, ATTN_HEAD_DIM)), _const_spec((1, ATTN_HEAD_DIM)),
            _const_spec((ATTN_V_DIM, 1)),
        ],
        out_specs=pl.BlockSpec((None, s, ATTN_V_DIM), lambda bi, hi: (bi, 0, hi)),
        out_shape=jax.ShapeDtypeStruct((b, s, ATTN_HEADS * ATTN_V_DIM), F32),
        scratch_shapes=[pltpu.VMEM((2 + tk // tq, tk, tq), F32),
                        pltpu.VMEM((2, tk, tq), F32), pltpu.VMEM((2, tk, tq), F32),
                        pltpu.VMEM((2, ATTN_V_DIM, tq), F32), pltpu.VMEM((2, 1, tq), F32)],
        compiler_params=pltpu.CompilerParams(dimension_semantics=("parallel", "parallel"),
                                             vmem_limit_bytes=VMEM_LIMIT),
        name="diff_attn",
    )(slopes, kn_tab, qn_tab, qT, k3, vT, vec(lq1), vec(lk1), vec(lq2), vec(lk2), subln_g.reshape(-1, 1))


def _out_proj_kernel(x_ref, eg_ref, eb_ref, ys_ref, o_ref, gate_ref, w_ref, g_ref, b_ref, out_ref):
    h = _layer_norm(x_ref[...], eg_ref[...], eb_ref[...])
    ya = o_ref[...] * _silu(gate_ref[...])
    mix = _dot(ys_ref[...].astype(BF16), w_ref[:D_SSM, :]) + _dot(ya.astype(BF16), w_ref[D_SSM:, :])
    out_ref[...] = _layer_norm(ALPHA * h + mix, g_ref[...], b_ref[...])


def _out_proj(x2d, ln_emb_g, ln_emb_b, y_ssm, o, gate, w_out, ln_g, ln_b, tm=512):
    t = x2d.shape[0]
    row = lambda n: pl.BlockSpec((tm, n), lambda i: (i, 0))
    vec = lambda a: a.reshape(1, -1)
    return pl.pallas_call(
        _out_proj_kernel,
        grid=(t // tm,),
        in_specs=[row(D_MODEL), _const_spec((1, D_MODEL)), _const_spec((1, D_MODEL)),
                  row(D_SSM), row(D_ATTN), row(D_ATTN),
                  _const_spec((D_SSM + D_ATTN, D_MODEL)), _const_spec((1, D_MODEL)), _const_spec((1, D_MODEL))],
        out_specs=row(D_MODEL),
        out_shape=jax.ShapeDtypeStruct((t, D_MODEL), F32),
        compiler_params=pltpu.CompilerParams(dimension_semantics=("parallel",), vmem_limit_bytes=VMEM_LIMIT),
        name="out_proj",
    )(x2d, vec(ln_emb_g), vec(ln_emb_b), y_ssm, o, gate, w_out.astype(BF16), vec(ln_g), vec(ln_b))


def kernel(x, ln_emb_g, ln_emb_b, w_in, conv_w, conv_b, A_log_fwd, A_log_bwd, dt_bias_fwd, dt_bias_bwd, D_skip,
           ssm_norm_g, lambda_q1, lambda_k1, lambda_q2, lambda_k2, subln_g, w_out, ln_g, ln_b):
    b, s, _ = x.shape
    t = b * s
    x2d = x.reshape(t, D_MODEL)
    z, xbc, dt, dtT, qT, k, vT, gate, kn, qn = _ln_inproj(x2d, ln_emb_g, ln_emb_b, w_in[0])

    xs, bt, cm = _conv_silu(xbc.reshape(b, s, D_XBC), conv_w[0], conv_b[0])
    y_fwd = _ssd_pass(False, xs, bt, cm, dt, dtT, A_log_fwd[0], dt_bias_fwd[0], (D_skip[0],))
    y_ssm = _ssd_pass(True, xs, bt, cm, dt, dtT, A_log_bwd[0], dt_bias_bwd[0],
                      (y_fwd, z.reshape(b, s, D_SSM), ssm_norm_g[0]))

    o = _diff_attn(qT, k.reshape(b, s, D_ATTN), vT, kn, qn, lambda_q1[0], lambda_k1[0], lambda_q2[0],
                   lambda_k2[0], subln_g[0]).reshape(t, D_ATTN)

    out = _out_proj(x2d, ln_emb_g, ln_emb_b, y_ssm.reshape(t, D_SSM), o, gate, w_out[0], ln_g[0], ln_b[0])
    return out.reshape(b, s, D_MODEL)
```

```python
import functools
import math

import jax
import jax.numpy as jnp
from jax import lax
from jax.experimental import pallas as pl
from jax.experimental.pallas import tpu as pltpu

D_MODEL = 1024
D_SSM = 1024
SSM_HEAD_DIM = 64
SSM_HEADS = 16
SSM_GROUPS = 2
HEADS_PER_GROUP = SSM_HEADS // SSM_GROUPS
D_STATE = 128
D_CONV = 5
CHUNK = 128
D_XBC = D_SSM + 2 * SSM_GROUPS * D_STATE
D_ATTN = 1024
ATTN_HEADS = 8
ATTN_HEAD_DIM = 64
ATTN_V_DIM = 128
GROUP_WIDTH = D_SSM // SSM_GROUPS
DEPTH = 1
ALPHA = (2.0 * DEPTH) ** 0.25
LN_EPS = 1e-5
RMS_EPS = 1e-5
LAM_INIT = 0.8 - 0.6 * math.exp(-0.3 * 0)

LANES = 128
VMEM_LIMIT = 48 * 1024 * 1024

F32 = jnp.float32
BF16 = jnp.bfloat16


LOG2E = math.log2(math.e)


def _dot(a, b):
    return jnp.dot(a, b, preferred_element_type=F32)


def _dot_nt(a, b):
    return lax.dot_general(a, b, (((1,), (1,)), ((), ())), preferred_element_type=F32)


def _layer_norm(x, g, b):
    mu = jnp.mean(x, axis=-1, keepdims=True)
    xc = x - mu
    var = jnp.mean(xc * xc, axis=-1, keepdims=True)
    return xc * lax.rsqrt(var + LN_EPS) * g + b


def _silu(x):
    return x / (1.0 + jnp.exp(-x))


def _softplus(x):
    return jnp.maximum(x, 0.0) + jnp.log1p(jnp.exp(-jnp.abs(x)))


def _split3(x):
    hi = x.astype(BF16)
    r = x - hi.astype(F32)
    mid = r.astype(BF16)
    lo = (r - mid.astype(F32)).astype(BF16)
    return hi, mid, lo


def _dot_exact_rhs(m, x):
    hi, mid, lo = _split3(x)
    return _dot(m, hi) + _dot(m, mid) + _dot(m, lo)


def _dot_exact_lhs(x, m):
    hi, mid, lo = _split3(x)
    return _dot(hi, m) + _dot(mid, m) + _dot(lo, m)


def _const_spec(shape):
    nd = len(shape)
    return pl.BlockSpec(shape, lambda *_: (0,) * nd)


N_DT_TILES = 2 * SSM_GROUPS


ATTN_TQ = 256
ATTN_QPART = 256
ATTN_TK = 512
N_MAPS = 2 * ATTN_HEADS
NORM_SLACK = 1.01


def _ln_inproj_kernel(x_ref, g_ref, b_ref, wz_ref, wxbc_ref, wdt_ref, wqT_ref, wk_ref, wvT_ref, wg_ref,
                      grp_ref, grpT_ref,
                      z_ref, xbc_ref, dt_ref, dtT_ref, qT_ref, k_ref, vT_ref, gate_ref, kn_ref, qn_ref):
    h = _layer_norm(x_ref[...], g_ref[...], b_ref[...]).astype(BF16)
    z_ref[...] = _dot(h, wz_ref[...])
    xbc_ref[...] = _dot(h, wxbc_ref[...])
    dt = _dot(h, wdt_ref[...])
    dt_ref[...] = dt
    for i in range(N_DT_TILES):
        dtT_ref[i] = dt[:, i * LANES:(i + 1) * LANES].T[:HEADS_PER_GROUP, :]
    qT = _dot_nt(wqT_ref[...], h) * (LOG2E * ATTN_HEAD_DIM ** -0.5)
    qT_ref[...] = qT.astype(BF16)
    k = _dot(h, wk_ref[...])
    k_ref[...] = k.astype(BF16)
    vT_ref[...] = _dot_nt(wvT_ref[...], h).astype(BF16)
    gate_ref[...] = _dot(h, wg_ref[...])
    k2 = _dot((k * k).astype(BF16), grp_ref[...])
    kn_ref[...] = jnp.sqrt(jnp.max(k2, axis=0, keepdims=True)) * NORM_SLACK
    q2 = _dot(grpT_ref[...], (qT * qT).astype(BF16))[:N_MAPS, :]
    lane = lax.broadcasted_iota(jnp.int32, (N_MAPS, LANES), 1)
    qn = jnp.zeros((N_MAPS, LANES), F32)
    for part in range(q2.shape[1] // ATTN_QPART):
        pm = jnp.max(q2[:, part * ATTN_QPART:(part + 1) * ATTN_QPART], axis=1, keepdims=True)
        qn = jnp.where(lane == part, jnp.sqrt(pm) * NORM_SLACK, qn)
    qn_ref[...] = qn


def _ln_inproj(x2d, ln_g, ln_b, w_in, tm=ATTN_TK):
    t = x2d.shape[0]
    grp = (jnp.arange(D_ATTN)[:, None] // ATTN_HEAD_DIM == jnp.arange(LANES)[None, :]).astype(BF16)
    offs = [0, D_SSM, D_SSM + D_XBC, D_SSM + D_XBC + 2 * SSM_HEADS]
    wz = w_in[:, offs[0]:offs[1]].astype(BF16)
    wxbc = w_in[:, offs[1]:offs[2]].astype(BF16)
    wdt = w_in[:, offs[2]:offs[3]].reshape(D_MODEL, N_DT_TILES, HEADS_PER_GROUP)
    wdt = jnp.pad(wdt, ((0, 0), (0, 0), (0, LANES - HEADS_PER_GROUP))).reshape(D_MODEL, -1).astype(BF16)
    o = offs[3]
    wq, wk, wv, wg = (w_in[:, o + i * D_ATTN:o + (i + 1) * D_ATTN].astype(BF16) for i in range(4))
    row = lambda n: pl.BlockSpec((tm, n), lambda i: (i, 0))
    col = lambda n: pl.BlockSpec((n, tm), lambda i: (0, i))
    weights = [wz, wxbc, wdt, wq.T, wk, wv.T, wg, grp, grp.T]
    nt = t // tm
    return pl.pallas_call(
        _ln_inproj_kernel,
        grid=(nt,),
        in_specs=[row(D_MODEL), _const_spec((1, D_MODEL)), _const_spec((1, D_MODEL))]
                 + [pl.BlockSpec(w.shape, lambda i: (0, 0), pipeline_mode=pl.Buffered(1)) for w in weights],
        out_specs=[row(D_SSM), row(D_XBC), row(N_DT_TILES * LANES),
                   pl.BlockSpec((N_DT_TILES, HEADS_PER_GROUP, tm), lambda i: (0, 0, i)),
                   col(D_ATTN), row(D_ATTN), col(D_ATTN), row(D_ATTN),
                   pl.BlockSpec((None, 1, LANES), lambda i: (i, 0, 0)),
                   pl.BlockSpec((None, N_MAPS, LANES), lambda i: (i, 0, 0))],
        out_shape=[jax.ShapeDtypeStruct((t, D_SSM), F32), jax.ShapeDtypeStruct((t, D_XBC), F32),
                   jax.ShapeDtypeStruct((t, N_DT_TILES * LANES), F32),
                   jax.ShapeDtypeStruct((N_DT_TILES, HEADS_PER_GROUP, t), F32),
                   jax.ShapeDtypeStruct((D_ATTN, t), BF16), jax.ShapeDtypeStruct((t, D_ATTN), BF16),
                   jax.ShapeDtypeStruct((D_ATTN, t), BF16), jax.ShapeDtypeStruct((t, D_ATTN), F32),
                   jax.ShapeDtypeStruct((nt, 1, LANES), F32), jax.ShapeDtypeStruct((nt, N_MAPS, LANES), F32)],
        compiler_params=pltpu.CompilerParams(dimension_semantics=("parallel",), vmem_limit_bytes=VMEM_LIMIT),
        name="ln_inproj",
    )(x2d, ln_g.reshape(1, -1), ln_b.reshape(1, -1), *weights)


HALO = 8


def _conv_silu_kernel(prev_ref, cur_ref, next_ref, w_ref, b_ref, xs_ref, bt_ref, c_ref, ext_ref):
    i = pl.program_id(1)
    n = pl.num_programs(1)
    tc = cur_ref.shape[0]
    ext_ref[0:HALO, :] = jnp.where(i == 0, 0.0, prev_ref[...])
    ext_ref[HALO:HALO + tc, :] = cur_ref[...]
    ext_ref[HALO + tc:, :] = jnp.where(i == n - 1, 0.0, next_ref[...])
    acc = jnp.zeros((tc, D_XBC), F32) + b_ref[...]
    for kk in range(D_CONV):
        acc = acc + ext_ref[pl.ds(HALO - D_CONV // 2 + kk, tc), :] * w_ref[kk:kk + 1, :]
    act = _silu(acc)
    xs_ref[...] = act[:, :D_SSM]
    for g in range(SSM_GROUPS):
        lo = D_SSM + g * D_STATE
        bt_ref[g] = act[:, lo:lo + D_STATE].T.astype(BF16)
    c_ref[...] = act[:, D_SSM + SSM_GROUPS * D_STATE:].astype(BF16)


def _conv_silu(xbc, conv_w, conv_b, tc=512):
    b, s, _ = xbc.shape
    nb = s // tc
    hb = tc // HALO
    return pl.pallas_call(
        _conv_silu_kernel,
        grid=(b, nb),
        in_specs=[
            pl.BlockSpec((None, HALO, D_XBC), lambda bi, i: (bi, jnp.maximum(i * hb - 1, 0), 0)),
            pl.BlockSpec((None, tc, D_XBC), lambda bi, i: (bi, i, 0)),
            pl.BlockSpec((None, HALO, D_XBC), lambda bi, i: (bi, jnp.minimum((i + 1) * hb, s // HALO - 1), 0)),
            _const_spec((D_CONV, D_XBC)),
            _const_spec((1, D_XBC)),
        ],
        out_specs=[
            pl.BlockSpec((None, tc, D_SSM), lambda bi, i: (bi, i, 0)),
            pl.BlockSpec((None, SSM_GROUPS, D_STATE, tc), lambda bi, i: (bi, 0, 0, i)),
            pl.BlockSpec((None, tc, SSM_GROUPS * D_STATE), lambda bi, i: (bi, i, 0)),
        ],
        out_shape=[
            jax.ShapeDtypeStruct((b, s, D_SSM), F32),
            jax.ShapeDtypeStruct((b, SSM_GROUPS, D_STATE, s), BF16),
            jax.ShapeDtypeStruct((b, s, SSM_GROUPS * D_STATE), BF16),
        ],
        scratch_shapes=[pltpu.VMEM((tc + 2 * HALO, D_XBC), F32)],
        compiler_params=pltpu.CompilerParams(dimension_semantics=("parallel", "parallel"),
                                             vmem_limit_bytes=VMEM_LIMIT),
        name="conv_silu",
    )(xbc, xbc, xbc, conv_w, conv_b.reshape(1, -1))


def _ssd_chunk(rev, x, bt, cm, dt_raw, dtT_raw, alog_row, alog_col, bias_row, bias_col, state):
    L = CHUNK
    r = lax.broadcasted_iota(jnp.int32, (L, L), 0)
    c = lax.broadcasted_iota(jnp.int32, (L, L), 1)
    keep = (c >= r) if rev else (c <= r)
    cum_l = keep.astype(BF16)
    cum_r = ((r >= c) if rev else (r <= c)).astype(BF16)
    eh = lax.broadcasted_iota(jnp.int32, (HEADS_PER_GROUP, GROUP_WIDTH), 0)
    el = lax.broadcasted_iota(jnp.int32, (HEADS_PER_GROUP, GROUP_WIDTH), 1)
    expand = (el // SSM_HEAD_DIM == eh).astype(BF16)

    dt = _softplus(dt_raw + bias_row)
    a = dt * (-jnp.exp(alog_row))
    dt_t = _softplus(dtT_raw + bias_col)
    a_t = dt_t * (-jnp.exp(alog_col))
    acs = _dot_exact_rhs(cum_l, a)
    acs_t = _dot_exact_lhs(a_t, cum_r)
    dt_x = _dot_exact_lhs(dt, expand)
    acs_x = _dot_exact_lhs(acs, expand)
    edge = 0 if rev else L - 1
    tot_x = acs_x[edge:edge + 1, :]

    xdt = x * dt_x
    y = _dot(cm, state.astype(BF16)) * jnp.exp(acs_x)
    cb = _dot(cm, bt)
    lane = lax.broadcasted_iota(jnp.int32, (L, LANES), 1)
    tiles = []
    for t in range(GROUP_WIDTH // LANES):
        xt = xdt[:, t * LANES:(t + 1) * LANES]
        acc = None
        for half in range(LANES // SSM_HEAD_DIM):
            j = t * (LANES // SSM_HEAD_DIM) + half
            seg = acs[:, j:j + 1] - acs_t[j:j + 1, :]
            gmat = (cb * jnp.exp(jnp.where(keep, seg, -jnp.inf))).astype(BF16)
            in_head = (lane >= half * SSM_HEAD_DIM) & (lane < (half + 1) * SSM_HEAD_DIM)
            part = _dot(gmat, jnp.where(in_head, xt, 0.0).astype(BF16))
            acc = part if acc is None else acc + part
        tiles.append(acc)
    y = y + jnp.concatenate(tiles, axis=1)

    w = (xdt * jnp.exp(tot_x - acs_x)).astype(BF16)
    new_state = jnp.exp(tot_x) * state + _dot(bt, w)
    return y, new_state


def _ssd_kernel(rev, nchunk, *refs):
    if rev:
        (xs_ref, bt_ref, c_ref, dt_ref, dtT_ref, alr_ref, alc_ref, br_ref, bc_ref,
         yf_ref, z_ref, ng_ref, y_ref, state_ref) = refs
    else:
        (xs_ref, bt_ref, c_ref, dt_ref, dtT_ref, alr_ref, alc_ref, br_ref, bc_ref,
         dx_ref, y_ref, state_ref) = refs

    @pl.when(pl.program_id(2) == 0)
    def _():
        state_ref[...] = jnp.zeros_like(state_ref)

    order = range(nchunk - 1, -1, -1) if rev else range(nchunk)
    for ci in order:
        rows = slice(ci * CHUNK, (ci + 1) * CHUNK)
        x = xs_ref[rows, :]
        y, new_state = _ssd_chunk(rev, x, bt_ref[:, rows], c_ref[rows, :], dt_ref[rows, :HEADS_PER_GROUP], dtT_ref[:, rows],
                                  alr_ref[...], alc_ref[...], br_ref[...], bc_ref[...], state_ref[...])
        state_ref[...] = new_state
        if rev:
            yy = (yf_ref[rows, :] + y) * _silu(z_ref[rows, :])
            ms = jnp.mean(yy * yy, axis=-1, keepdims=True)
            y_ref[rows, :] = yy * lax.rsqrt(ms + RMS_EPS) * ng_ref[...]
        else:
            y_ref[rows, :] = y + dx_ref[...] * x


def _ssd_pass(rev, xs, bt, cm, dt, dtT, a_log, dt_bias, extra, nchunk=4):
    b, s, _ = xs.shape
    tcs = nchunk * CHUNK
    nblk = s // tcs
    blk = (lambda i: nblk - 1 - i) if rev else (lambda i: i)
    dirn = 1 if rev else 0
    gw, hg = GROUP_WIDTH, HEADS_PER_GROUP
    seq_spec = lambda w: pl.BlockSpec((None, tcs, w), lambda bi, g, i: (bi, blk(i), g))
    in_specs = [
        seq_spec(gw),
        pl.BlockSpec((None, None, D_STATE, tcs), lambda bi, g, i: (bi, g, 0, blk(i))),
        seq_spec(D_STATE),
        pl.BlockSpec((tcs, LANES), lambda bi, g, i: (bi * nblk + blk(i), dirn * SSM_GROUPS + g)),
        pl.BlockSpec((None, hg, tcs), lambda bi, g, i: (dirn * SSM_GROUPS + g, 0, bi * nblk + blk(i))),
        pl.BlockSpec((None, 1, hg), lambda bi, g, i: (g, 0, 0)),
        pl.BlockSpec((None, hg, 1), lambda bi, g, i: (g, 0, 0)),
        pl.BlockSpec((None, 1, hg), lambda bi, g, i: (g, 0, 0)),
        pl.BlockSpec((None, hg, 1), lambda bi, g, i: (g, 0, 0)),
    ]
    args = [xs, bt, cm, dt, dtT,
            a_log.reshape(SSM_GROUPS, 1, hg), a_log.reshape(SSM_GROUPS, hg, 1),
            dt_bias.reshape(SSM_GROUPS, 1, hg), dt_bias.reshape(SSM_GROUPS, hg, 1)]
    group_row = pl.BlockSpec((None, 1, gw), lambda bi, g, i: (g, 0, 0))
    if rev:
        y_fwd, z, norm_g = extra
        in_specs += [seq_spec(gw), seq_spec(gw), group_row]
        args += [y_fwd, z, norm_g.reshape(SSM_GROUPS, 1, gw)]
    else:
        (d_skip,) = extra
        in_specs += [group_row]
        args += [jnp.repeat(d_skip, SSM_HEAD_DIM).reshape(SSM_GROUPS, 1, gw)]
    return pl.pallas_call(
        functools.partial(_ssd_kernel, rev, nchunk),
        grid=(b, SSM_GROUPS, nblk),
        in_specs=in_specs,
        out_specs=seq_spec(gw),
        out_shape=jax.ShapeDtypeStruct((b, s, D_SSM), F32),
        scratch_shapes=[pltpu.VMEM((D_STATE, gw), F32)],
        compiler_params=pltpu.CompilerParams(dimension_semantics=("parallel", "parallel", "arbitrary"),
                                             vmem_limit_bytes=VMEM_LIMIT),
        name="ssd_bwd" if rev else "ssd_fwd",
    )(*args)


NEG_BIG = -1e30


SKIP_LOG2 = 80.0
FIXED_SHIFT_MAX_LOG2 = 60.0


def _attn_plan(tq, tk, nk, nq, bi, hi, qi, slope, kn_ref, qn_ref, plan_i_ref, plan_f_ref):
    i0 = qi * tq
    jd = qi // (tk // tq)
    parts = tq // ATTN_QPART
    qns, bases = [], []
    for r in range(2):
        base = (bi * ATTN_HEADS + hi) * 2 + r
        qn = qn_ref[base * (nq * parts) + qi * parts]
        for part in range(1, parts):
            qn = jnp.maximum(qn, qn_ref[base * (nq * parts) + qi * parts + part])
        qns.append(qn)
        bases.append(base * nk)
    jlo, jhi = jd, jd
    kmax = [kn_ref[bases[0]], kn_ref[bases[1]]]
    for jt in range(nk):
        dist = jnp.maximum(jnp.maximum(i0 - (jt * tk + tk - 1), jt * tk - (i0 + tq - 1)), 0).astype(F32)
        need = None
        for r in range(2):
            kn = kn_ref[bases[r] + jt]
            kmax[r] = jnp.maximum(kmax[r], kn)
            reach = SKIP_LOG2 + qns[r] * (kn + kn_ref[bases[r] + jd])
            need_r = slope * dist <= reach
            need = need_r if need is None else (need | need_r)
        jlo = jnp.where(need, jnp.minimum(jlo, jt), jlo)
        jhi = jnp.where(need, jnp.maximum(jhi, jt), jhi)
    plan_i_ref[0] = jlo
    plan_i_ref[1] = jhi
    plan_f_ref[0] = jnp.maximum(qns[0] * kmax[0], qns[1] * kmax[1])


def _attn_q_tile(tq, tk, nk, nq, bi, hi, qi, slope, kn_ref, qn_ref, qT_ref, k_ref, vT_ref, lq1_ref, lk1_ref, lq2_ref,
                 lk2_ref, sg_ref, o_ref, e_ref, s_ref, s1_ref, acc_ref, l_ref, plan_i_ref, plan_f_ref):
    ratio = tk // tq
    i0 = qi * tq
    q0 = pl.multiple_of(i0, tq)
    jd = qi // ratio
    dsel = qi % ratio
    jlo, jhi, score_bound = plan_i_ref[0], plan_i_ref[1], plan_f_ref[0]

    def plan_next():
        _attn_plan(tq, tk, nk, nq, bi, hi, jnp.minimum(qi + 1, nq - 1), slope, kn_ref, qn_ref, plan_i_ref, plan_f_ref)

    row = lax.broadcasted_iota(jnp.int32, (2 * ATTN_HEAD_DIM, tq), 0)
    qf = qT_ref[:, pl.ds(q0, tq)].astype(F32)
    rhs = (jnp.where(row < ATTN_HEAD_DIM, qf, 0.0).astype(BF16),
           jnp.where(row >= ATTN_HEAD_DIM, qf, 0.0).astype(BF16))
    def finish_previous():
        _attn_epilogue(tq, jnp.maximum(qi - 1, 0), lq1_ref, lk1_ref, lq2_ref, lk2_ref, sg_ref, o_ref, acc_ref, l_ref)
        acc_ref[...] = jnp.zeros_like(acc_ref)

    def tile_bias(j):
        off = (i0 - j * tk).astype(F32)
        before, after = j < jd, j > jd
        eidx = jnp.where(before, 0, jnp.where(after, 1, 2 + dsel))
        cj = jnp.where(before, -slope * (off - float(tk - 1)), jnp.where(after, slope * (off + float(tq - 1)), 0.0))
        return eidx, cj

    @pl.when(score_bound <= FIXED_SHIFT_MAX_LOG2)
    def _():
        jl = (jlo // 2) * 2
        jh = (jhi // 2) * 2 + 1

        def stage_a(j, buf):
            kt = k_ref[pl.ds(pl.multiple_of(j * tk, tk), tk), :]
            for r in range(2):
                buf[r] = _dot(kt, rhs[r])

        def stage_b(j, buf, l):
            eidx, cj = tile_bias(j)
            half = jnp.exp2(jnp.full((1, tq), 0.5 * cj, F32))
            vt = vT_ref[:, pl.ds(pl.multiple_of(j * tk, tk), tk)]
            l_new = []
            for r in range(2):
                p = jnp.exp2(buf[r] + e_ref[eidx])
                l_new.append(l[r] + jnp.sum(p, axis=0, keepdims=True) * half * half)
                acc_ref[r] += _dot(vt, p.astype(BF16)) * half * half
            return tuple(l_new)

        def body(t, l):
            j = jl + 2 * t
            stage_a(j + 1, s1_ref)
            l = stage_b(j, s_ref, l)
            stage_a(jnp.minimum(j + 2, jh), s_ref)
            return stage_b(j + 1, s1_ref, l)

        stage_a(jl, s_ref)
        finish_previous()
        plan_next()
        zero = jnp.zeros((1, tq), F32)
        l = lax.fori_loop(0, (jh - jl + 1) // 2, body, (zero, zero))
        l_ref[0] = l[0]
        l_ref[1] = l[1]

    def stage_a(j, m_old):
        eidx, cj = tile_bias(j)
        kt = k_ref[pl.ds(pl.multiple_of(j * tk, tk), tk), :]
        m_new, shift, alpha = [], [], []
        for r in range(2):
            s = _dot(kt, rhs[r]) + e_ref[eidx]
            s_ref[r] = s
            mr = jnp.maximum(m_old[r], jnp.max(s, axis=0, keepdims=True) + cj)
            m_new.append(mr)
            shift.append(mr - cj)
            alpha.append(jnp.exp2(m_old[r] - mr))
        return tuple(m_new), tuple(shift), tuple(alpha)

    def stage_b(j, shift, alpha, l_old):
        vt = vT_ref[:, pl.ds(pl.multiple_of(j * tk, tk), tk)]
        l_new = []
        for r in range(2):
            p = jnp.exp2(s_ref[r] - shift[r])
            l_new.append(alpha[r] * l_old[r] + jnp.sum(p, axis=0, keepdims=True))
            acc_ref[r] = alpha[r] * acc_ref[r] + _dot(vt, p.astype(BF16))
        return tuple(l_new)

    def body(t, carry):
        m, shift, alpha, l = carry
        j = jlo + t
        l = stage_b(jnp.maximum(j - 1, jlo), shift, alpha, l)
        m, shift, alpha = stage_a(jnp.minimum(j, jhi), m)
        return m, shift, alpha, l

    @pl.when(score_bound > FIXED_SHIFT_MAX_LOG2)
    def _():
        neg = jnp.full((1, tq), NEG_BIG, F32)
        big = jnp.full((1, tq), -NEG_BIG, F32)
        one = jnp.ones((1, tq), F32)
        zero = jnp.zeros((1, tq), F32)
        finish_previous()
        plan_next()
        init = ((neg, neg), (big, big), (one, one), (zero, zero))
        _, _, _, l = lax.fori_loop(0, jhi - jlo + 2, body, init)
        l_ref[0] = l[0]
        l_ref[1] = l[1]


def _attn_epilogue(tq, qi, lq1_ref, lk1_ref, lq2_ref, lk2_ref, sg_ref, o_ref, acc_ref, l_ref):
    lam = (jnp.exp(jnp.sum(lq1_ref[...] * lk1_ref[...], axis=-1, keepdims=True))
           - jnp.exp(jnp.sum(lq2_ref[...] * lk2_ref[...], axis=-1, keepdims=True)) + LAM_INIT)
    o = acc_ref[0] / l_ref[0] - lam * (acc_ref[1] / l_ref[1])
    ms = jnp.mean(o * o, axis=0, keepdims=True)
    q0 = pl.multiple_of(qi * tq, tq)
    o_ref[pl.ds(q0, tq), :] = (o * lax.rsqrt(ms + RMS_EPS) * sg_ref[...] * (1.0 - LAM_INIT)).T


def _attn_kernel(tq, tk, nk, nq, slopes_ref, kn_ref, qn_ref, qT_ref, k_ref, vT_ref, lq1_ref, lk1_ref, lq2_ref,
                 lk2_ref, sg_ref, o_ref, e_ref, s_ref, s1_ref, acc_ref, l_ref, plan_i_ref, plan_f_ref):
    bi, hi = pl.program_id(0), pl.program_id(1)
    slope = slopes_ref[hi]
    ii = lax.broadcasted_iota(jnp.int32, (tk, tq), 1)
    jj = lax.broadcasted_iota(jnp.int32, (tk, tq), 0)
    rel = (ii - jj).astype(F32)
    e_ref[0] = -slope * (rel + float(tk - 1))
    e_ref[1] = slope * (rel - float(tq - 1))
    for d in range(tk // tq):
        e_ref[2 + d] = -slope * jnp.abs(rel + float(d * tq))
    s_ref[...] = jnp.zeros_like(s_ref)
    acc_ref[...] = jnp.zeros_like(acc_ref)
    l_ref[...] = jnp.ones_like(l_ref)
    _attn_plan(tq, tk, nk, nq, bi, hi, 0, slope, kn_ref, qn_ref, plan_i_ref, plan_f_ref)

    def q_tile(qi, carry):
        _attn_q_tile(tq, tk, nk, nq, bi, hi, qi, slope, kn_ref, qn_ref, qT_ref, k_ref, vT_ref, lq1_ref, lk1_ref,
                     lq2_ref, lk2_ref, sg_ref, o_ref, e_ref, s_ref, s1_ref, acc_ref, l_ref, plan_i_ref, plan_f_ref)
        return carry

    lax.fori_loop(0, nq, q_tile, 0)
    _attn_epilogue(tq, nq - 1, lq1_ref, lk1_ref, lq2_ref, lk2_ref, sg_ref, o_ref, acc_ref, l_ref)


def _diff_attn(qT, k3, vT, kn, qn, lq1, lk1, lq2, lk2, subln_g, tq=ATTN_TQ, tk=ATTN_TK):
    b, s, _ = k3.shape
    nq, nk = s // tq, s // tk
    ratio = tk // tq
    assert tk % tq == 0 and nk % 2 == 0
    hw = 2 * ATTN_HEAD_DIM
    slopes = jnp.exp2(-8.0 * (jnp.arange(ATTN_HEADS, dtype=F32) + 1.0) / ATTN_HEADS) * LOG2E
    kn_tab = jnp.transpose(kn[:, 0, :N_MAPS].reshape(b, nk, N_MAPS), (0, 2, 1)).reshape(-1)
    kparts = tk // ATTN_QPART
    qn_tab = jnp.transpose(qn[:, :, :kparts].reshape(b, nk, N_MAPS, kparts), (0, 2, 1, 3)).reshape(-1)
    vec = lambda a: a.reshape(1, -1)
    return pl.pallas_call(
        functools.partial(_attn_kernel, tq, tk, nk, nq),
        grid=(b, ATTN_HEADS),
        in_specs=[
            pl.BlockSpec(memory_space=pltpu.SMEM),
            pl.BlockSpec(memory_space=pltpu.SMEM),
            pl.BlockSpec(memory_space=pltpu.SMEM),
            pl.BlockSpec((hw, s), lambda bi, hi: (hi, bi)),
            pl.BlockSpec((None, s, hw), lambda bi, hi: (bi, 0, hi)),
            pl.BlockSpec((ATTN_V_DIM, s), lambda bi, hi: (hi, bi)),
            _const_spec((1, ATTN_HEAD_DIM)), _const_spec((1, ATTN_HEAD_DIM)),
            _const_spec((1, ATTN_HEAD_DIM)), _const_spec((1, ATTN_HEAD_DIM)),
            _const_spec((ATTN_V_DIM, 1)),
        ],
        out_specs=pl.BlockSpec((None, s, ATTN_V_DIM), lambda bi, hi: (bi, 0, hi)),
        out_shape=jax.ShapeDtypeStruct((b, s, ATTN_HEADS * ATTN_V_DIM), F32),
        scratch_shapes=[pltpu.VMEM((2 + tk // tq, tk, tq), F32),
                        pltpu.VMEM((2, tk, tq), F32), pltpu.VMEM((2, tk, tq), F32),
                        pltpu.VMEM((2, ATTN_V_DIM, tq), F32), pltpu.VMEM((2, 1, tq), F32),
                        pltpu.SMEM((2,), jnp.int32), pltpu.SMEM((1,), F32)],
        compiler_params=pltpu.CompilerParams(dimension_semantics=("parallel", "parallel"),
                                             vmem_limit_bytes=VMEM_LIMIT),
        name="diff_attn",
    )(slopes, kn_tab, qn_tab, qT, k3, vT, vec(lq1), vec(lk1), vec(lq2), vec(lk2), subln_g.reshape(-1, 1))


def _out_proj_kernel(x_ref, eg_ref, eb_ref, ys_ref, o_ref, gate_ref, w_ref, g_ref, b_ref, out_ref):
    h = _layer_norm(x_ref[...], eg_ref[...], eb_ref[...])
    ya = o_ref[...] * _silu(gate_ref[...])
    mix = _dot(ys_ref[...].astype(BF16), w_ref[:D_SSM, :]) + _dot(ya.astype(BF16), w_ref[D_SSM:, :])
    out_ref[...] = _layer_norm(ALPHA * h + mix, g_ref[...], b_ref[...])


def _out_proj(x2d, ln_emb_g, ln_emb_b, y_ssm, o, gate, w_out, ln_g, ln_b, tm=512):
    t = x2d.shape[0]
    row = lambda n: pl.BlockSpec((tm, n), lambda i: (i, 0))
    vec = lambda a: a.reshape(1, -1)
    return pl.pallas_call(
        _out_proj_kernel,
        grid=(t // tm,),
        in_specs=[row(D_MODEL), _const_spec((1, D_MODEL)), _const_spec((1, D_MODEL)),
                  row(D_SSM), row(D_ATTN), row(D_ATTN),
                  _const_spec((D_SSM + D_ATTN, D_MODEL)), _const_spec((1, D_MODEL)), _const_spec((1, D_MODEL))],
        out_specs=row(D_MODEL),
        out_shape=jax.ShapeDtypeStruct((t, D_MODEL), F32),
        compiler_params=pltpu.CompilerParams(dimension_semantics=("parallel",), vmem_limit_bytes=VMEM_LIMIT),
        name="out_proj",
    )(x2d, vec(ln_emb_g), vec(ln_emb_b), y_ssm, o, gate, w_out.astype(BF16), vec(ln_g), vec(ln_b))


def kernel(x, ln_emb_g, ln_emb_b, w_in, conv_w, conv_b, A_log_fwd, A_log_bwd, dt_bias_fwd, dt_bias_bwd, D_skip,
           ssm_norm_g, lambda_q1, lambda_k1, lambda_q2, lambda_k2, subln_g, w_out, ln_g, ln_b):
    b, s, _ = x.shape
    t = b * s
    x2d = x.reshape(t, D_MODEL)
    z, xbc, dt, dtT, qT, k, vT, gate, kn, qn = _ln_inproj(x2d, ln_emb_g, ln_emb_b, w_in[0])

    xs, bt, cm = _conv_silu(xbc.reshape(b, s, D_XBC), conv_w[0], conv_b[0])
    y_fwd = _ssd_pass(False, xs, bt, cm, dt, dtT, A_log_fwd[0], dt_bias_fwd[0], (D_skip[0],))
    y_ssm = _ssd_pass(True, xs, bt, cm, dt, dtT, A_log_bwd[0], dt_bias_bwd[0],
                      (y_fwd, z.reshape(b, s, D_SSM), ssm_norm_g[0]))

    o = _diff_attn(qT, k.reshape(b, s, D_ATTN), vT, kn, qn, lambda_q1[0], lambda_k1[0], lambda_q2[0],
                   lambda_k2[0], subln_g[0]).reshape(t, D_ATTN)

    out = _out_proj(x2d, ln_emb_g, ln_emb_b, y_ssm.reshape(t, D_SSM), o, gate, w_out[0], ln_g[0], ln_b[0])
    return out.reshape(b, s, D_MODEL)
```

```python
import functools
import math

import jax
import jax.numpy as jnp
from jax import lax
from jax.experimental import pallas as pl
from jax.experimental.pallas import tpu as pltpu

D_MODEL = 1024
D_SSM = 1024
SSM_HEAD_DIM = 64
SSM_HEADS = 16
SSM_GROUPS = 2
HEADS_PER_GROUP = SSM_HEADS // SSM_GROUPS
D_STATE = 128
D_CONV = 5
CHUNK = 128
D_XBC = D_SSM + 2 * SSM_GROUPS * D_STATE
D_ATTN = 1024
ATTN_HEADS = 8
ATTN_HEAD_DIM = 64
ATTN_V_DIM = 128
GROUP_WIDTH = D_SSM // SSM_GROUPS
DEPTH = 1
ALPHA = (2.0 * DEPTH) ** 0.25
LN_EPS = 1e-5
RMS_EPS = 1e-5
LAM_INIT = 0.8 - 0.6 * math.exp(-0.3 * 0)

LANES = 128
VMEM_LIMIT = 48 * 1024 * 1024

F32 = jnp.float32
BF16 = jnp.bfloat16


LOG2E = math.log2(math.e)


def _dot(a, b):
    return jnp.dot(a, b, preferred_element_type=F32)


def _dot_nt(a, b):
    return lax.dot_general(a, b, (((1,), (1,)), ((), ())), preferred_element_type=F32)


def _layer_norm(x, g, b):
    mu = jnp.mean(x, axis=-1, keepdims=True)
    xc = x - mu
    var = jnp.mean(xc * xc, axis=-1, keepdims=True)
    return xc * lax.rsqrt(var + LN_EPS) * g + b


def _silu(x):
    return x / (1.0 + jnp.exp(-x))


def _softplus(x):
    return jnp.maximum(x, 0.0) + jnp.log1p(jnp.exp(-jnp.abs(x)))


def _split3(x):
    hi = x.astype(BF16)
    r = x - hi.astype(F32)
    mid = r.astype(BF16)
    lo = (r - mid.astype(F32)).astype(BF16)
    return hi, mid, lo


def _dot_exact_rhs(m, x):
    hi, mid, lo = _split3(x)
    return _dot(m, hi) + _dot(m, mid) + _dot(m, lo)


def _dot_exact_lhs(x, m):
    hi, mid, lo = _split3(x)
    return _dot(hi, m) + _dot(mid, m) + _dot(lo, m)


def _const_spec(shape):
    nd = len(shape)
    return pl.BlockSpec(shape, lambda *_: (0,) * nd)


N_DT_TILES = 2 * SSM_GROUPS


ATTN_TQ = 256
ATTN_QPART = 256
ATTN_TK = 512
N_MAPS = 2 * ATTN_HEADS
NORM_SLACK = 1.01


def _ln_inproj_kernel(x_ref, g_ref, b_ref, wz_ref, wxbc_ref, wdt_ref, wqT_ref, wk_ref, wvT_ref, wg_ref,
                      grp_ref, grpT_ref,
                      z_ref, xbc_ref, dt_ref, dtT_ref, qT_ref, k_ref, vT_ref, gate_ref, kn_ref, qn_ref):
    h = _layer_norm(x_ref[...], g_ref[...], b_ref[...]).astype(BF16)
    z_ref[...] = _dot(h, wz_ref[...])
    xbc_ref[...] = _dot(h, wxbc_ref[...])
    dt = _dot(h, wdt_ref[...])
    dt_ref[...] = dt
    for i in range(N_DT_TILES):
        dtT_ref[i] = dt[:, i * LANES:(i + 1) * LANES].T[:HEADS_PER_GROUP, :]
    qT = _dot_nt(wqT_ref[...], h) * (LOG2E * ATTN_HEAD_DIM ** -0.5)
    qT_ref[...] = qT.astype(BF16)
    k = _dot(h, wk_ref[...])
    k_ref[...] = k.astype(BF16)
    vT_ref[...] = _dot_nt(wvT_ref[...], h).astype(BF16)
    gate_ref[...] = _dot(h, wg_ref[...])
    k2 = _dot((k * k).astype(BF16), grp_ref[...])
    kn_ref[...] = jnp.sqrt(jnp.max(k2, axis=0, keepdims=True)) * NORM_SLACK
    q2 = _dot(grpT_ref[...], (qT * qT).astype(BF16))[:N_MAPS, :]
    lane = lax.broadcasted_iota(jnp.int32, (N_MAPS, LANES), 1)
    qn = jnp.zeros((N_MAPS, LANES), F32)
    for part in range(q2.shape[1] // ATTN_QPART):
        pm = jnp.max(q2[:, part * ATTN_QPART:(part + 1) * ATTN_QPART], axis=1, keepdims=True)
        qn = jnp.where(lane == part, jnp.sqrt(pm) * NORM_SLACK, qn)
    qn_ref[...] = qn


def _ln_inproj(x2d, ln_g, ln_b, w_in, tm=ATTN_TK):
    t = x2d.shape[0]
    grp = (jnp.arange(D_ATTN)[:, None] // ATTN_HEAD_DIM == jnp.arange(LANES)[None, :]).astype(BF16)
    offs = [0, D_SSM, D_SSM + D_XBC, D_SSM + D_XBC + 2 * SSM_HEADS]
    wz = w_in[:, offs[0]:offs[1]].astype(BF16)
    wxbc = w_in[:, offs[1]:offs[2]].astype(BF16)
    wdt = w_in[:, offs[2]:offs[3]].reshape(D_MODEL, N_DT_TILES, HEADS_PER_GROUP)
    wdt = jnp.pad(wdt, ((0, 0), (0, 0), (0, LANES - HEADS_PER_GROUP))).reshape(D_MODEL, -1).astype(BF16)
    o = offs[3]
    wq, wk, wv, wg = (w_in[:, o + i * D_ATTN:o + (i + 1) * D_ATTN].astype(BF16) for i in range(4))
    row = lambda n: pl.BlockSpec((tm, n), lambda i: (i, 0))
    col = lambda n: pl.BlockSpec((n, tm), lambda i: (0, i))
    weights = [wz, wxbc, wdt, wq.T, wk, wv.T, wg, grp, grp.T]
    nt = t // tm
    return pl.pallas_call(
        _ln_inproj_kernel,
        grid=(nt,),
        in_specs=[row(D_MODEL), _const_spec((1, D_MODEL)), _const_spec((1, D_MODEL))]
                 + [pl.BlockSpec(w.shape, lambda i: (0, 0), pipeline_mode=pl.Buffered(1)) for w in weights],
        out_specs=[row(D_SSM), row(D_XBC), row(N_DT_TILES * LANES),
                   pl.BlockSpec((N_DT_TILES, HEADS_PER_GROUP, tm), lambda i: (0, 0, i)),
                   col(D_ATTN), row(D_ATTN), col(D_ATTN), row(D_ATTN),
                   pl.BlockSpec((None, 1, LANES), lambda i: (i, 0, 0)),
                   pl.BlockSpec((None, N_MAPS, LANES), lambda i: (i, 0, 0))],
        out_shape=[jax.ShapeDtypeStruct((t, D_SSM), F32), jax.ShapeDtypeStruct((t, D_XBC), F32),
                   jax.ShapeDtypeStruct((t, N_DT_TILES * LANES), F32),
                   jax.ShapeDtypeStruct((N_DT_TILES, HEADS_PER_GROUP, t), F32),
                   jax.ShapeDtypeStruct((D_ATTN, t), BF16), jax.ShapeDtypeStruct((t, D_ATTN), BF16),
                   jax.ShapeDtypeStruct((D_ATTN, t), BF16), jax.ShapeDtypeStruct((t, D_ATTN), F32),
                   jax.ShapeDtypeStruct((nt, 1, LANES), F32), jax.ShapeDtypeStruct((nt, N_MAPS, LANES), F32)],
        compiler_params=pltpu.CompilerParams(dimension_semantics=("parallel",), vmem_limit_bytes=VMEM_LIMIT),
        name="ln_inproj",
    )(x2d, ln_g.reshape(1, -1), ln_b.reshape(1, -1), *weights)


HALO = 8


def _conv_silu_kernel(prev_ref, cur_ref, next_ref, w_ref, b_ref, xs_ref, bt_ref, c_ref, ext_ref):
    i = pl.program_id(1)
    n = pl.num_programs(1)
    tc = cur_ref.shape[0]
    ext_ref[0:HALO, :] = jnp.where(i == 0, 0.0, prev_ref[...])
    ext_ref[HALO:HALO + tc, :] = cur_ref[...]
    ext_ref[HALO + tc:, :] = jnp.where(i == n - 1, 0.0, next_ref[...])
    acc = jnp.zeros((tc, D_XBC), F32) + b_ref[...]
    for kk in range(D_CONV):
        acc = acc + ext_ref[pl.ds(HALO - D_CONV // 2 + kk, tc), :] * w_ref[kk:kk + 1, :]
    act = _silu(acc)
    xs_ref[...] = act[:, :D_SSM]
    for g in range(SSM_GROUPS):
        lo = D_SSM + g * D_STATE
        bt_ref[g] = act[:, lo:lo + D_STATE].T.astype(BF16)
    c_ref[...] = act[:, D_SSM + SSM_GROUPS * D_STATE:].astype(BF16)


def _conv_silu(xbc, conv_w, conv_b, tc=512):
    b, s, _ = xbc.shape
    nb = s // tc
    hb = tc // HALO
    return pl.pallas_call(
        _conv_silu_kernel,
        grid=(b, nb),
        in_specs=[
            pl.BlockSpec((None, HALO, D_XBC), lambda bi, i: (bi, jnp.maximum(i * hb - 1, 0), 0)),
            pl.BlockSpec((None, tc, D_XBC), lambda bi, i: (bi, i, 0)),
            pl.BlockSpec((None, HALO, D_XBC), lambda bi, i: (bi, jnp.minimum((i + 1) * hb, s // HALO - 1), 0)),
            _const_spec((D_CONV, D_XBC)),
            _const_spec((1, D_XBC)),
        ],
        out_specs=[
            pl.BlockSpec((None, tc, D_SSM), lambda bi, i: (bi, i, 0)),
            pl.BlockSpec((None, SSM_GROUPS, D_STATE, tc), lambda bi, i: (bi, 0, 0, i)),
            pl.BlockSpec((None, tc, SSM_GROUPS * D_STATE), lambda bi, i: (bi, i, 0)),
        ],
        out_shape=[
            jax.ShapeDtypeStruct((b, s, D_SSM), F32),
            jax.ShapeDtypeStruct((b, SSM_GROUPS, D_STATE, s), BF16),
            jax.ShapeDtypeStruct((b, s, SSM_GROUPS * D_STATE), BF16),
        ],
        scratch_shapes=[pltpu.VMEM((tc + 2 * HALO, D_XBC), F32)],
        compiler_params=pltpu.CompilerParams(dimension_semantics=("parallel", "parallel"),
                                             vmem_limit_bytes=VMEM_LIMIT),
        name="conv_silu",
    )(xbc, xbc, xbc, conv_w, conv_b.reshape(1, -1))


def _ssd_chunk(rev, x, bt, cm, dt_raw, dtT_raw, alog_row, alog_col, bias_row, bias_col, state):
    L = CHUNK
    r = lax.broadcasted_iota(jnp.int32, (L, L), 0)
    c = lax.broadcasted_iota(jnp.int32, (L, L), 1)
    keep = (c >= r) if rev else (c <= r)
    cum_l = keep.astype(BF16)
    cum_r = ((r >= c) if rev else (r <= c)).astype(BF16)
    eh = lax.broadcasted_iota(jnp.int32, (HEADS_PER_GROUP, GROUP_WIDTH), 0)
    el = lax.broadcasted_iota(jnp.int32, (HEADS_PER_GROUP, GROUP_WIDTH), 1)
    expand = (el // SSM_HEAD_DIM == eh).astype(BF16)

    dt = _softplus(dt_raw + bias_row)
    a = dt * (-jnp.exp(alog_row))
    dt_t = _softplus(dtT_raw + bias_col)
    a_t = dt_t * (-jnp.exp(alog_col))
    acs = _dot_exact_rhs(cum_l, a)
    acs_t = _dot_exact_lhs(a_t, cum_r)
    dt_x = _dot_exact_lhs(dt, expand)
    acs_x = _dot_exact_lhs(acs, expand)
    edge = 0 if rev else L - 1
    tot_x = acs_x[edge:edge + 1, :]

    xdt = x * dt_x
    y = _dot(cm, state.astype(BF16)) * jnp.exp(acs_x)
    cb = _dot(cm, bt)
    lane = lax.broadcasted_iota(jnp.int32, (L, LANES), 1)
    tiles = []
    for t in range(GROUP_WIDTH // LANES):
        xt = xdt[:, t * LANES:(t + 1) * LANES]
        acc = None
        for half in range(LANES // SSM_HEAD_DIM):
            j = t * (LANES // SSM_HEAD_DIM) + half
            seg = acs[:, j:j + 1] - acs_t[j:j + 1, :]
            gmat = (cb * jnp.exp(jnp.where(keep, seg, -jnp.inf))).astype(BF16)
            in_head = (lane >= half * SSM_HEAD_DIM) & (lane < (half + 1) * SSM_HEAD_DIM)
            part = _dot(gmat, jnp.where(in_head, xt, 0.0).astype(BF16))
            acc = part if acc is None else acc + part
        tiles.append(acc)
    y = y + jnp.concatenate(tiles, axis=1)

    w = (xdt * jnp.exp(tot_x - acs_x)).astype(BF16)
    new_state = jnp.exp(tot_x) * state + _dot(bt, w)
    return y, new_state


def _ssd_kernel(rev, nchunk, *refs):
    if rev:
        (xs_ref, bt_ref, c_ref, dt_ref, dtT_ref, alr_ref, alc_ref, br_ref, bc_ref,
         yf_ref, z_ref, ng_ref, y_ref, state_ref) = refs
    else:
        (xs_ref, bt_ref, c_ref, dt_ref, dtT_ref, alr_ref, alc_ref, br_ref, bc_ref,
         dx_ref, y_ref, state_ref) = refs

    @pl.when(pl.program_id(2) == 0)
    def _():
        state_ref[...] = jnp.zeros_like(state_ref)

    order = range(nchunk - 1, -1, -1) if rev else range(nchunk)
    for ci in order:
        rows = slice(ci * CHUNK, (ci + 1) * CHUNK)
        x = xs_ref[rows, :]
        y, new_state = _ssd_chunk(rev, x, bt_ref[:, rows], c_ref[rows, :], dt_ref[rows, :HEADS_PER_GROUP], dtT_ref[:, rows],
                                  alr_ref[...], alc_ref[...], br_ref[...], bc_ref[...], state_ref[...])
        state_ref[...] = new_state
        if rev:
            yy = (yf_ref[rows, :] + y) * _silu(z_ref[rows, :])
            ms = jnp.mean(yy * yy, axis=-1, keepdims=True)
            y_ref[rows, :] = yy * lax.rsqrt(ms + RMS_EPS) * ng_ref[...]
        else:
            y_ref[rows, :] = y + dx_ref[...] * x


def _ssd_pass(rev, xs, bt, cm, dt, dtT, a_log, dt_bias, extra, nchunk=4):
    b, s, _ = xs.shape
    tcs = nchunk * CHUNK
    nblk = s // tcs
    blk = (lambda i: nblk - 1 - i) if rev else (lambda i: i)
    dirn = 1 if rev else 0
    gw, hg = GROUP_WIDTH, HEADS_PER_GROUP
    seq_spec = lambda w: pl.BlockSpec((None, tcs, w), lambda bi, g, i: (bi, blk(i), g))
    in_specs = [
        seq_spec(gw),
        pl.BlockSpec((None, None, D_STATE, tcs), lambda bi, g, i: (bi, g, 0, blk(i))),
        seq_spec(D_STATE),
        pl.BlockSpec((tcs, LANES), lambda bi, g, i: (bi * nblk + blk(i), dirn * SSM_GROUPS + g)),
        pl.BlockSpec((None, hg, tcs), lambda bi, g, i: (dirn * SSM_GROUPS + g, 0, bi * nblk + blk(i))),
        pl.BlockSpec((None, 1, hg), lambda bi, g, i: (g, 0, 0)),
        pl.BlockSpec((None, hg, 1), lambda bi, g, i: (g, 0, 0)),
        pl.BlockSpec((None, 1, hg), lambda bi, g, i: (g, 0, 0)),
        pl.BlockSpec((None, hg, 1), lambda bi, g, i: (g, 0, 0)),
    ]
    args = [xs, bt, cm, dt, dtT,
            a_log.reshape(SSM_GROUPS, 1, hg), a_log.reshape(SSM_GROUPS, hg, 1),
            dt_bias.reshape(SSM_GROUPS, 1, hg), dt_bias.reshape(SSM_GROUPS, hg, 1)]
    group_row = pl.BlockSpec((None, 1, gw), lambda bi, g, i: (g, 0, 0))
    if rev:
        y_fwd, z, norm_g = extra
        in_specs += [seq_spec(gw), seq_spec(gw), group_row]
        args += [y_fwd, z, norm_g.reshape(SSM_GROUPS, 1, gw)]
    else:
        (d_skip,) = extra
        in_specs += [group_row]
        args += [jnp.repeat(d_skip, SSM_HEAD_DIM).reshape(SSM_GROUPS, 1, gw)]
    return pl.pallas_call(
        functools.partial(_ssd_kernel, rev, nchunk),
        grid=(b, SSM_GROUPS, nblk),
        in_specs=in_specs,
        out_specs=seq_spec(gw),
        out_shape=jax.ShapeDtypeStruct((b, s, D_SSM), F32),
        scratch_shapes=[pltpu.VMEM((D_STATE, gw), F32)],
        compiler_params=pltpu.CompilerParams(dimension_semantics=("parallel", "parallel", "arbitrary"),
                                             vmem_limit_bytes=VMEM_LIMIT),
        name="ssd_bwd" if rev else "ssd_fwd",
    )(*args)


NEG_BIG = -1e30


SKIP_LOG2 = 80.0
FIXED_SHIFT_MAX_LOG2 = 60.0


def _attn_plan(tq, tk, nk, nq, bi, hi, qi, slope, kn_ref, qn_ref, plan_i_ref, plan_f_ref):
    i0 = qi * tq
    jd = qi // (tk // tq)
    parts = tq // ATTN_QPART
    qns, bases = [], []
    for r in range(2):
        base = (bi * ATTN_HEADS + hi) * 2 + r
        qn = qn_ref[base * (nq * parts) + qi * parts]
        for part in range(1, parts):
            qn = jnp.maximum(qn, qn_ref[base * (nq * parts) + qi * parts + part])
        qns.append(qn)
        bases.append(base * nk)
    jlo, jhi = jd, jd
    kmax = [kn_ref[bases[0]], kn_ref[bases[1]]]
    for jt in range(nk):
        dist = jnp.maximum(jnp.maximum(i0 - (jt * tk + tk - 1), jt * tk - (i0 + tq - 1)), 0).astype(F32)
        need = None
        for r in range(2):
            kn = kn_ref[bases[r] + jt]
            kmax[r] = jnp.maximum(kmax[r], kn)
            reach = SKIP_LOG2 + qns[r] * (kn + kn_ref[bases[r] + jd])
            need_r = slope * dist <= reach
            need = need_r if need is None else (need | need_r)
        jlo = jnp.where(need, jnp.minimum(jlo, jt), jlo)
        jhi = jnp.where(need, jnp.maximum(jhi, jt), jhi)
    plan_i_ref[0] = jlo
    plan_i_ref[1] = jhi
    plan_f_ref[0] = jnp.maximum(qns[0] * kmax[0], qns[1] * kmax[1])


def _attn_q_tile(tq, tk, nk, nq, bi, hi, qi, slope, kn_ref, qn_ref, qT_ref, k_ref, vT_ref, lq1_ref, lk1_ref, lq2_ref,
                 lk2_ref, sg_ref, o_ref, e_ref, s_ref, s1_ref, acc_ref, l_ref, plan_i_ref, plan_f_ref):
    ratio = tk // tq
    i0 = qi * tq
    q0 = pl.multiple_of(i0, tq)
    jd = qi // ratio
    dsel = qi % ratio
    jlo, jhi, score_bound = plan_i_ref[0], plan_i_ref[1], plan_f_ref[0]

    def plan_next():
        _attn_plan(tq, tk, nk, nq, bi, hi, jnp.minimum(qi + 1, nq - 1), slope, kn_ref, qn_ref, plan_i_ref, plan_f_ref)

    row = lax.broadcasted_iota(jnp.int32, (2 * ATTN_HEAD_DIM, tq), 0)
    qf = qT_ref[:, pl.ds(q0, tq)].astype(F32)
    rhs = (jnp.where(row < ATTN_HEAD_DIM, qf, 0.0).astype(BF16),
           jnp.where(row >= ATTN_HEAD_DIM, qf, 0.0).astype(BF16))
    def finish_previous():
        _attn_epilogue(tq, jnp.maximum(qi - 1, 0), lq1_ref, lk1_ref, lq2_ref, lk2_ref, sg_ref, o_ref, acc_ref, l_ref)
        acc_ref[...] = jnp.zeros_like(acc_ref)

    def tile_bias(j):
        off = (i0 - j * tk).astype(F32)
        before, after = j < jd, j > jd
        eidx = jnp.where(before, 0, jnp.where(after, 1, 2 + dsel))
        cj = jnp.where(before, -slope * (off - float(tk - 1)), jnp.where(after, slope * (off + float(tq - 1)), 0.0))
        return eidx, cj

    @pl.when(score_bound <= FIXED_SHIFT_MAX_LOG2)
    def _():
        jl = (jlo // 2) * 2
        jh = (jhi // 2) * 2 + 1

        def stage_a(j, buf):
            kt = k_ref[pl.ds(pl.multiple_of(j * tk, tk), tk), :]
            for r in range(2):
                buf[r] = _dot(kt, rhs[r])

        def stage_b(j, buf, l):
            eidx, cj = tile_bias(j)
            half = jnp.exp2(jnp.full((1, tq), 0.5 * cj, F32))
            vt = vT_ref[:, pl.ds(pl.multiple_of(j * tk, tk), tk)]
            l_new = []
            for r in range(2):
                p = jnp.exp2(buf[r] + e_ref[eidx])
                l_new.append(l[r] + jnp.sum(p, axis=0, keepdims=True) * half * half)
                acc_ref[r] += _dot(vt, p.astype(BF16)) * half * half
            return tuple(l_new)

        def pair(j, l):
            stage_a(j + 1, s1_ref)
            l = stage_b(j, s_ref, l)
            stage_a(jnp.minimum(j + 2, jh), s_ref)
            return stage_b(j + 1, s1_ref, l)

        def body(t, l):
            j = jl + 4 * t
            return pair(j + 2, pair(j, l))

        stage_a(jl, s_ref)
        finish_previous()
        plan_next()
        zero = jnp.zeros((1, tq), F32)
        npairs = (jh - jl + 1) // 2
        l = lax.fori_loop(0, npairs // 2, body, (zero, zero))
        l_ref[0] = l[0]
        l_ref[1] = l[1]

        @pl.when(npairs % 2 == 1)
        def _():
            l = pair(jh - 1, (l_ref[0], l_ref[1]))
            l_ref[0] = l[0]
            l_ref[1] = l[1]

    def stage_a(j, m_old):
        eidx, cj = tile_bias(j)
        kt = k_ref[pl.ds(pl.multiple_of(j * tk, tk), tk), :]
        m_new, shift, alpha = [], [], []
        for r in range(2):
            s = _dot(kt, rhs[r]) + e_ref[eidx]
            s_ref[r] = s
            mr = jnp.maximum(m_old[r], jnp.max(s, axis=0, keepdims=True) + cj)
            m_new.append(mr)
            shift.append(mr - cj)
            alpha.append(jnp.exp2(m_old[r] - mr))
        return tuple(m_new), tuple(shift), tuple(alpha)

    def stage_b(j, shift, alpha, l_old):
        vt = vT_ref[:, pl.ds(pl.multiple_of(j * tk, tk), tk)]
        l_new = []
        for r in range(2):
            p = jnp.exp2(s_ref[r] - shift[r])
            l_new.append(alpha[r] * l_old[r] + jnp.sum(p, axis=0, keepdims=True))
            acc_ref[r] = alpha[r] * acc_ref[r] + _dot(vt, p.astype(BF16))
        return tuple(l_new)

    def body(t, carry):
        m, shift, alpha, l = carry
        j = jlo + t
        l = stage_b(jnp.maximum(j - 1, jlo), shift, alpha, l)
        m, shift, alpha = stage_a(jnp.minimum(j, jhi), m)
        return m, shift, alpha, l

    @pl.when(score_bound > FIXED_SHIFT_MAX_LOG2)
    def _():
        neg = jnp.full((1, tq), NEG_BIG, F32)
        big = jnp.full((1, tq), -NEG_BIG, F32)
        one = jnp.ones((1, tq), F32)
        zero = jnp.zeros((1, tq), F32)
        finish_previous()
        plan_next()
        init = ((neg, neg), (big, big), (one, one), (zero, zero))
        _, _, _, l = lax.fori_loop(0, jhi - jlo + 2, body, init)
        l_ref[0] = l[0]
        l_ref[1] = l[1]


def _attn_epilogue(tq, qi, lq1_ref, lk1_ref, lq2_ref, lk2_ref, sg_ref, o_ref, acc_ref, l_ref):
    lam = (jnp.exp(jnp.sum(lq1_ref[...] * lk1_ref[...], axis=-1, keepdims=True))
           - jnp.exp(jnp.sum(lq2_ref[...] * lk2_ref[...], axis=-1, keepdims=True)) + LAM_INIT)
    o = acc_ref[0] / l_ref[0] - lam * (acc_ref[1] / l_ref[1])
    ms = jnp.mean(o * o, axis=0, keepdims=True)
    q0 = pl.multiple_of(qi * tq, tq)
    o_ref[pl.ds(q0, tq), :] = (o * lax.rsqrt(ms + RMS_EPS) * sg_ref[...] * (1.0 - LAM_INIT)).T


def _attn_kernel(tq, tk, nk, nq, slopes_ref, kn_ref, qn_ref, qT_ref, k_ref, vT_ref, lq1_ref, lk1_ref, lq2_ref,
                 lk2_ref, sg_ref, o_ref, e_ref, s_ref, s1_ref, acc_ref, l_ref, plan_i_ref, plan_f_ref):
    bi, hi = pl.program_id(0), pl.program_id(1)
    slope = slopes_ref[hi]
    ii = lax.broadcasted_iota(jnp.int32, (tk, tq), 1)
    jj = lax.broadcasted_iota(jnp.int32, (tk, tq), 0)
    rel = (ii - jj).astype(F32)
    e_ref[0] = -slope * (rel + float(tk - 1))
    e_ref[1] = slope * (rel - float(tq - 1))
    for d in range(tk // tq):
        e_ref[2 + d] = -slope * jnp.abs(rel + float(d * tq))
    s_ref[...] = jnp.zeros_like(s_ref)
    acc_ref[...] = jnp.zeros_like(acc_ref)
    l_ref[...] = jnp.ones_like(l_ref)
    _attn_plan(tq, tk, nk, nq, bi, hi, 0, slope, kn_ref, qn_ref, plan_i_ref, plan_f_ref)

    def q_tile(qi, carry):
        _attn_q_tile(tq, tk, nk, nq, bi, hi, qi, slope, kn_ref, qn_ref, qT_ref, k_ref, vT_ref, lq1_ref, lk1_ref,
                     lq2_ref, lk2_ref, sg_ref, o_ref, e_ref, s_ref, s1_ref, acc_ref, l_ref, plan_i_ref, plan_f_ref)
        return carry

    lax.fori_loop(0, nq, q_tile, 0)
    _attn_epilogue(tq, nq - 1, lq1_ref, lk1_ref, lq2_ref, lk2_ref, sg_ref, o_ref, acc_ref, l_ref)


def _diff_attn(qT, k3, vT, kn, qn, lq1, lk1, lq2, lk2, subln_g, tq=ATTN_TQ, tk=ATTN_TK):
    b, s, _ = k3.shape
    nq, nk = s // tq, s // tk
    ratio = tk // tq
    assert tk % tq == 0 and nk % 2 == 0
    hw = 2 * ATTN_HEAD_DIM
    slopes = jnp.exp2(-8.0 * (jnp.arange(ATTN_HEADS, dtype=F32) + 1.0) / ATTN_HEADS) * LOG2E
    kn_tab = jnp.transpose(kn[:, 0, :N_MAPS].reshape(b, nk, N_MAPS), (0, 2, 1)).reshape(-1)
    kparts = tk // ATTN_QPART
    qn_tab = jnp.transpose(qn[:, :, :kparts].reshape(b, nk, N_MAPS, kparts), (0, 2, 1, 3)).reshape(-1)
    vec = lambda a: a.reshape(1, -1)
    return pl.pallas_call(
        functools.partial(_attn_kernel, tq, tk, nk, nq),
        grid=(b, ATTN_HEADS),
        in_specs=[
            pl.BlockSpec(memory_space=pltpu.SMEM),
            pl.BlockSpec(memory_space=pltpu.SMEM),
            pl.BlockSpec(memory_space=pltpu.SMEM),
            pl.BlockSpec((hw, s), lambda bi, hi: (hi, bi)),
            pl.BlockSpec((None, s, hw), lambda bi, hi: (bi, 0, hi)),
            pl.BlockSpec((ATTN_V_DIM, s), lambda bi, hi: (hi, bi)),
            _const_spec((1, ATTN_HEAD_DIM)), _const_spec((1, ATTN_HEAD_DIM)),
            _const_spec((1, ATTN_HEAD_DIM)), _const_spec((1, ATTN_HEAD_DIM)),
            _const_spec((ATTN_V_DIM, 1)),
        ],
        out_specs=pl.BlockSpec((None, s, ATTN_V_DIM), lambda bi, hi: (bi, 0, hi)),
        out_shape=jax.ShapeDtypeStruct((b, s, ATTN_HEADS * ATTN_V_DIM), F32),
        scratch_shapes=[pltpu.VMEM((2 + tk // tq, tk, tq), F32),
                        pltpu.VMEM((2, tk, tq), F32), pltpu.VMEM((2, tk, tq), F32),
                        pltpu.VMEM((2, ATTN_V_DIM, tq), F32), pltpu.VMEM((2, 1, tq), F32),
                        pltpu.SMEM((2,), jnp.int32), pltpu.SMEM((1,), F32)],
        compiler_params=pltpu.CompilerParams(dimension_semantics=("parallel", "parallel"),
                                             vmem_limit_bytes=VMEM_LIMIT),
        name="diff_attn",
    )(slopes, kn_tab, qn_tab, qT, k3, vT, vec(lq1), vec(lk1), vec(lq2), vec(lk2), subln_g.reshape(-1, 1))


def _out_proj_kernel(x_ref, eg_ref, eb_ref, ys_ref, o_ref, gate_ref, w_ref, g_ref, b_ref, out_ref):
    h = _layer_norm(x_ref[...], eg_ref[...], eb_ref[...])
    ya = o_ref[...] * _silu(gate_ref[...])
    mix = _dot(ys_ref[...].astype(BF16), w_ref[:D_SSM, :]) + _dot(ya.astype(BF16), w_ref[D_SSM:, :])
    out_ref[...] = _layer_norm(ALPHA * h + mix, g_ref[...], b_ref[...])


def _out_proj(x2d, ln_emb_g, ln_emb_b, y_ssm, o, gate, w_out, ln_g, ln_b, tm=512):
    t = x2d.shape[0]
    row = lambda n: pl.BlockSpec((tm, n), lambda i: (i, 0))
    vec = lambda a: a.reshape(1, -1)
    return pl.pallas_call(
        _out_proj_kernel,
        grid=(t // tm,),
        in_specs=[row(D_MODEL), _const_spec((1, D_MODEL)), _const_spec((1, D_MODEL)),
                  row(D_SSM), row(D_ATTN), row(D_ATTN),
                  _const_spec((D_SSM + D_ATTN, D_MODEL)), _const_spec((1, D_MODEL)), _const_spec((1, D_MODEL))],
        out_specs=row(D_MODEL),
        out_shape=jax.ShapeDtypeStruct((t, D_MODEL), F32),
        compiler_params=pltpu.CompilerParams(dimension_semantics=("parallel",), vmem_limit_bytes=VMEM_LIMIT),
        name="out_proj",
    )(x2d, vec(ln_emb_g), vec(ln_emb_b), y_ssm, o, gate, w_out.astype(BF16), vec(ln_g), vec(ln_b))


def kernel(x, ln_emb_g, ln_emb_b, w_in, conv_w, conv_b, A_log_fwd, A_log_bwd, dt_bias_fwd, dt_bias_bwd, D_skip,
           ssm_norm_g, lambda_q1, lambda_k1, lambda_q2, lambda_k2, subln_g, w_out, ln_g, ln_b):
    b, s, _ = x.shape
    t = b * s
    x2d = x.reshape(t, D_MODEL)
    z, xbc, dt, dtT, qT, k, vT, gate, kn, qn = _ln_inproj(x2d, ln_emb_g, ln_emb_b, w_in[0])

    xs, bt, cm = _conv_silu(xbc.reshape(b, s, D_XBC), conv_w[0], conv_b[0])
    y_fwd = _ssd_pass(False, xs, bt, cm, dt, dtT, A_log_fwd[0], dt_bias_fwd[0], (D_skip[0],))
    y_ssm = _ssd_pass(True, xs, bt, cm, dt, dtT, A_log_bwd[0], dt_bias_bwd[0],
                      (y_fwd, z.reshape(b, s, D_SSM), ssm_norm_g[0]))

    o = _diff_attn(qT, k.reshape(b, s, D_ATTN), vT, kn, qn, lambda_q1[0], lambda_k1[0], lambda_q2[0],
                   lambda_k2[0], subln_g[0]).reshape(t, D_ATTN)

    out = _out_proj(x2d, ln_emb_g, ln_emb_b, y_ssm.reshape(t, D_SSM), o, gate, w_out[0], ln_g[0], ln_b[0])
    return out.reshape(b, s, D_MODEL)
```

```python
import functools
import math

import jax
import jax.numpy as jnp
from jax import lax
from jax.experimental import pallas as pl
from jax.experimental.pallas import tpu as pltpu

D_MODEL = 1024
D_SSM = 1024
SSM_HEAD_DIM = 64
SSM_HEADS = 16
SSM_GROUPS = 2
HEADS_PER_GROUP = SSM_HEADS // SSM_GROUPS
D_STATE = 128
D_CONV = 5
CHUNK = 128
D_XBC = D_SSM + 2 * SSM_GROUPS * D_STATE
D_ATTN = 1024
ATTN_HEADS = 8
ATTN_HEAD_DIM = 64
ATTN_V_DIM = 128
GROUP_WIDTH = D_SSM // SSM_GROUPS
DEPTH = 1
ALPHA = (2.0 * DEPTH) ** 0.25
LN_EPS = 1e-5
RMS_EPS = 1e-5
LAM_INIT = 0.8 - 0.6 * math.exp(-0.3 * 0)

LANES = 128
VMEM_LIMIT = 48 * 1024 * 1024

F32 = jnp.float32
BF16 = jnp.bfloat16


LOG2E = math.log2(math.e)


def _dot(a, b):
    return jnp.dot(a, b, preferred_element_type=F32)


def _dot_nt(a, b):
    return lax.dot_general(a, b, (((1,), (1,)), ((), ())), preferred_element_type=F32)


def _layer_norm(x, g, b):
    mu = jnp.mean(x, axis=-1, keepdims=True)
    xc = x - mu
    var = jnp.mean(xc * xc, axis=-1, keepdims=True)
    return xc * lax.rsqrt(var + LN_EPS) * g + b


def _silu(x):
    return x / (1.0 + jnp.exp(-x))


def _softplus(x):
    return jnp.maximum(x, 0.0) + jnp.log1p(jnp.exp(-jnp.abs(x)))


def _split3(x):
    hi = x.astype(BF16)
    r = x - hi.astype(F32)
    mid = r.astype(BF16)
    lo = (r - mid.astype(F32)).astype(BF16)
    return hi, mid, lo


def _dot_exact_rhs(m, x):
    hi, mid, lo = _split3(x)
    return _dot(m, hi) + _dot(m, mid) + _dot(m, lo)


def _dot_exact_lhs(x, m):
    hi, mid, lo = _split3(x)
    return _dot(hi, m) + _dot(mid, m) + _dot(lo, m)


def _const_spec(shape):
    nd = len(shape)
    return pl.BlockSpec(shape, lambda *_: (0,) * nd)


N_DT_TILES = 2 * SSM_GROUPS


ATTN_TQ = 256
ATTN_QPART = 256
ATTN_TK = 512
N_MAPS = 2 * ATTN_HEADS
NORM_SLACK = 1.01


def _ln_inproj_kernel(x_ref, g_ref, b_ref, wz_ref, wxbc_ref, wdt_ref, wqT_ref, wk_ref, wvT_ref, wg_ref,
                      grp_ref, grpT_ref,
                      z_ref, xbc_ref, dtT_ref, qT_ref, k_ref, vT_ref, gate_ref, kn_ref, qn_ref):
    h = _layer_norm(x_ref[...], g_ref[...], b_ref[...]).astype(BF16)
    z_ref[...] = _dot(h, wz_ref[...])
    xbc_ref[...] = _dot(h, wxbc_ref[...])
    dt = _dot(h, wdt_ref[...])
    for i in range(N_DT_TILES):
        dtT_ref[i] = dt[:, i * LANES:(i + 1) * LANES].T[:HEADS_PER_GROUP, :]
    qT = _dot_nt(wqT_ref[...], h) * (LOG2E * ATTN_HEAD_DIM ** -0.5)
    qT_ref[...] = qT.astype(BF16)
    k = _dot(h, wk_ref[...])
    k_ref[...] = k.astype(BF16)
    vT_ref[...] = _dot_nt(wvT_ref[...], h).astype(BF16)
    gate_ref[...] = _dot(h, wg_ref[...])
    k2 = _dot((k * k).astype(BF16), grp_ref[...])
    kn_ref[...] = jnp.sqrt(jnp.max(k2, axis=0, keepdims=True)) * NORM_SLACK
    q2 = _dot(grpT_ref[...], (qT * qT).astype(BF16))[:N_MAPS, :]
    lane = lax.broadcasted_iota(jnp.int32, (N_MAPS, LANES), 1)
    qn = jnp.zeros((N_MAPS, LANES), F32)
    for part in range(q2.shape[1] // ATTN_QPART):
        pm = jnp.max(q2[:, part * ATTN_QPART:(part + 1) * ATTN_QPART], axis=1, keepdims=True)
        qn = jnp.where(lane == part, jnp.sqrt(pm) * NORM_SLACK, qn)
    qn_ref[...] = qn


def _ln_inproj(x2d, ln_g, ln_b, w_in, tm=ATTN_TK):
    t = x2d.shape[0]
    grp = (jnp.arange(D_ATTN)[:, None] // ATTN_HEAD_DIM == jnp.arange(LANES)[None, :]).astype(BF16)
    offs = [0, D_SSM, D_SSM + D_XBC, D_SSM + D_XBC + 2 * SSM_HEADS]
    wz = w_in[:, offs[0]:offs[1]].astype(BF16)
    wxbc = w_in[:, offs[1]:offs[2]].astype(BF16)
    wdt = w_in[:, offs[2]:offs[3]].reshape(D_MODEL, N_DT_TILES, HEADS_PER_GROUP)
    wdt = jnp.pad(wdt, ((0, 0), (0, 0), (0, LANES - HEADS_PER_GROUP))).reshape(D_MODEL, -1).astype(BF16)
    o = offs[3]
    wq, wk, wv, wg = (w_in[:, o + i * D_ATTN:o + (i + 1) * D_ATTN].astype(BF16) for i in range(4))
    row = lambda n: pl.BlockSpec((tm, n), lambda i: (i, 0))
    col = lambda n: pl.BlockSpec((n, tm), lambda i: (0, i))
    weights = [wz, wxbc, wdt, wq.T, wk, wv.T, wg, grp, grp.T]
    nt = t // tm
    return pl.pallas_call(
        _ln_inproj_kernel,
        grid=(nt,),
        in_specs=[row(D_MODEL), _const_spec((1, D_MODEL)), _const_spec((1, D_MODEL))]
                 + [pl.BlockSpec(w.shape, lambda i: (0, 0), pipeline_mode=pl.Buffered(1)) for w in weights],
        out_specs=[row(D_SSM), row(D_XBC),
                   pl.BlockSpec((N_DT_TILES, HEADS_PER_GROUP, tm), lambda i: (0, 0, i)),
                   col(D_ATTN), row(D_ATTN), col(D_ATTN), row(D_ATTN),
                   pl.BlockSpec((None, 1, LANES), lambda i: (i, 0, 0)),
                   pl.BlockSpec((None, N_MAPS, LANES), lambda i: (i, 0, 0))],
        out_shape=[jax.ShapeDtypeStruct((t, D_SSM), F32), jax.ShapeDtypeStruct((t, D_XBC), F32),
                   jax.ShapeDtypeStruct((N_DT_TILES, HEADS_PER_GROUP, t), F32),
                   jax.ShapeDtypeStruct((D_ATTN, t), BF16), jax.ShapeDtypeStruct((t, D_ATTN), BF16),
                   jax.ShapeDtypeStruct((D_ATTN, t), BF16), jax.ShapeDtypeStruct((t, D_ATTN), F32),
                   jax.ShapeDtypeStruct((nt, 1, LANES), F32), jax.ShapeDtypeStruct((nt, N_MAPS, LANES), F32)],
        compiler_params=pltpu.CompilerParams(dimension_semantics=("parallel",), vmem_limit_bytes=VMEM_LIMIT),
        name="ln_inproj",
    )(x2d, ln_g.reshape(1, -1), ln_b.reshape(1, -1), *weights)


HALO = 8


def _conv_silu_kernel(prev_ref, cur_ref, next_ref, w_ref, b_ref, xs_ref, bt_ref, c_ref, ext_ref):
    i = pl.program_id(1)
    n = pl.num_programs(1)
    tc = cur_ref.shape[0]
    ext_ref[0:HALO, :] = jnp.where(i == 0, 0.0, prev_ref[...])
    ext_ref[HALO:HALO + tc, :] = cur_ref[...]
    ext_ref[HALO + tc:, :] = jnp.where(i == n - 1, 0.0, next_ref[...])
    acc = jnp.zeros((tc, D_XBC), F32) + b_ref[...]
    for kk in range(D_CONV):
        acc = acc + ext_ref[pl.ds(HALO - D_CONV // 2 + kk, tc), :] * w_ref[kk:kk + 1, :]
    act = _silu(acc)
    xs_ref[...] = act[:, :D_SSM]
    for g in range(SSM_GROUPS):
        lo = D_SSM + g * D_STATE
        bt_ref[g] = act[:, lo:lo + D_STATE].T.astype(BF16)
    c_ref[...] = act[:, D_SSM + SSM_GROUPS * D_STATE:].astype(BF16)


def _conv_silu(xbc, conv_w, conv_b, tc=512):
    b, s, _ = xbc.shape
    nb = s // tc
    hb = tc // HALO
    return pl.pallas_call(
        _conv_silu_kernel,
        grid=(b, nb),
        in_specs=[
            pl.BlockSpec((None, HALO, D_XBC), lambda bi, i: (bi, jnp.maximum(i * hb - 1, 0), 0)),
            pl.BlockSpec((None, tc, D_XBC), lambda bi, i: (bi, i, 0)),
            pl.BlockSpec((None, HALO, D_XBC), lambda bi, i: (bi, jnp.minimum((i + 1) * hb, s // HALO - 1), 0)),
            _const_spec((D_CONV, D_XBC)),
            _const_spec((1, D_XBC)),
        ],
        out_specs=[
            pl.BlockSpec((None, tc, D_SSM), lambda bi, i: (bi, i, 0)),
            pl.BlockSpec((None, SSM_GROUPS, D_STATE, tc), lambda bi, i: (bi, 0, 0, i)),
            pl.BlockSpec((None, tc, SSM_GROUPS * D_STATE), lambda bi, i: (bi, i, 0)),
        ],
        out_shape=[
            jax.ShapeDtypeStruct((b, s, D_SSM), F32),
            jax.ShapeDtypeStruct((b, SSM_GROUPS, D_STATE, s), BF16),
            jax.ShapeDtypeStruct((b, s, SSM_GROUPS * D_STATE), BF16),
        ],
        scratch_shapes=[pltpu.VMEM((tc + 2 * HALO, D_XBC), F32)],
        compiler_params=pltpu.CompilerParams(dimension_semantics=("parallel", "parallel"),
                                             vmem_limit_bytes=VMEM_LIMIT),
        name="conv_silu",
    )(xbc, xbc, xbc, conv_w, conv_b.reshape(1, -1))


SSD_CHUNKS_IN_FLIGHT = 2


def _ssd_bodies(rev, xs, bts, cms, dtTs, alogs, biases, groups, states):
    L = CHUNK
    r = lax.broadcasted_iota(jnp.int32, (L, L), 0)
    c = lax.broadcasted_iota(jnp.int32, (L, L), 1)
    keep = (c >= r) if rev else (c <= r)
    cum_r = ((r >= c) if rev else (r <= c)).astype(BF16)
    hg = HEADS_PER_GROUP
    eh = lax.broadcasted_iota(jnp.int32, (6 * hg, 2 * GROUP_WIDTH), 0)
    el = lax.broadcasted_iota(jnp.int32, (6 * hg, 2 * GROUP_WIDTH), 1)
    spread = ((el // GROUP_WIDTH == eh // (3 * hg)) & ((el % GROUP_WIDTH) // SSM_HEAD_DIM == eh % hg)).astype(BF16)
    lane = lax.broadcasted_iota(jnp.int32, (L, LANES), 1)
    edge = 0 if rev else L - 1
    gs = range(len(xs))

    dt_t = [_softplus(dtTs[g] + biases[g]) for g in gs]
    acs_t = [_dot_exact_lhs(dt_t[g] * (-jnp.exp(alogs[g])), cum_r) for g in gs]
    pieces_t = [jnp.concatenate([p.astype(F32) for p in _split3(dt_t[g]) + _split3(acs_t[g])], axis=0) for g in gs]
    both_x = [_dot(pieces_t[g].T.astype(BF16), spread) for g in gs]
    acs = [acs_t[g].T for g in gs]
    cb = [_dot(cms[g], bts[g]) for g in gs]
    dt_x = [both_x[g][:, :GROUP_WIDTH] for g in gs]
    acs_x = [both_x[g][:, GROUP_WIDTH:] for g in gs]
    tot_x = [acs_x[g][edge:edge + 1, :] for g in gs]
    xdt = [xs[g] * dt_x[g] for g in gs]

    y_diag = []
    for g in gs:
        tiles = []
        for t in range(GROUP_WIDTH // LANES):
            xt = xdt[g][:, t * LANES:(t + 1) * LANES]
            gmats, xparts = [], []
            for half in range(LANES // SSM_HEAD_DIM):
                j = t * (LANES // SSM_HEAD_DIM) + half
                seg = acs[g][:, j:j + 1] - acs_t[g][j:j + 1, :]
                gmats.append((cb[g] * jnp.exp(jnp.where(keep, seg, -jnp.inf))).astype(BF16))
                in_head = (lane >= half * SSM_HEAD_DIM) & (lane < (half + 1) * SSM_HEAD_DIM)
                xparts.append(jnp.where(in_head, xt, 0.0).astype(BF16))
            tiles.append(_dot(jnp.concatenate(gmats, axis=1), jnp.concatenate(xparts, axis=0)))
        y_diag.append(jnp.concatenate(tiles, axis=1))

    states = list(states)
    ys = []
    for g in gs:
        grp = groups[g]
        y_off = _dot(cms[g], states[grp].astype(BF16)) * jnp.exp(acs_x[g])
        ys.append(y_off + y_diag[g])
        w = (xdt[g] * jnp.exp(tot_x[g] - acs_x[g])).astype(BF16)
        states[grp] = jnp.exp(tot_x[g]) * states[grp] + _dot(bts[g], w)
    return ys, states


def _ssd_kernel(rev, nchunk, *refs):
    if rev:
        (xs_ref, bt_ref, c_ref, dtT_ref, alc_ref, bc_ref, yf_ref, z_ref, ng_ref, y_ref, state_ref) = refs
    else:
        (xs_ref, bt_ref, c_ref, dtT_ref, alc_ref, bc_ref, dx_ref, y_ref, state_ref) = refs

    @pl.when(pl.program_id(1) == 0)
    def _():
        state_ref[...] = jnp.zeros_like(state_ref)

    gw = GROUP_WIDTH
    gs = range(SSM_GROUPS)
    cols = [slice(g * gw, (g + 1) * gw) for g in gs]
    order = list(range(nchunk - 1, -1, -1) if rev else range(nchunk))
    states = [state_ref[g] for g in gs]
    for c0 in range(0, nchunk, SSD_CHUNKS_IN_FLIGHT):
        bodies = [(slice(ci * CHUNK, (ci + 1) * CHUNK), g) for ci in order[c0:c0 + SSD_CHUNKS_IN_FLIGHT] for g in gs]
        x = [xs_ref[rows, cols[g]] for rows, g in bodies]
        ys, states = _ssd_bodies(rev, x, [bt_ref[g, :, rows] for rows, g in bodies],
                                 [c_ref[rows, g * D_STATE:(g + 1) * D_STATE] for rows, g in bodies],
                                 [dtT_ref[g, :, rows] for rows, g in bodies],
                                 [alc_ref[g] for _, g in bodies], [bc_ref[g] for _, g in bodies],
                                 [g for _, g in bodies], states)
        for (rows, g), xb, y in zip(bodies, x, ys):
            if rev:
                yy = (yf_ref[rows, cols[g]] + y) * _silu(z_ref[rows, cols[g]])
                ms = jnp.mean(yy * yy, axis=-1, keepdims=True)
                y_ref[rows, cols[g]] = yy * lax.rsqrt(ms + RMS_EPS) * ng_ref[g]
            else:
                y_ref[rows, cols[g]] = y + dx_ref[g] * xb
    for g in gs:
        state_ref[g] = states[g]


def _ssd_pass(rev, xs, bt, cm, dtT, a_log, dt_bias, extra, nchunk=4):
    b, s, _ = xs.shape
    tcs = nchunk * CHUNK
    nblk = s // tcs
    blk = (lambda i: nblk - 1 - i) if rev else (lambda i: i)
    dirn = 1 if rev else 0
    gw, hg, ng = GROUP_WIDTH, HEADS_PER_GROUP, SSM_GROUPS
    seq_spec = lambda w: pl.BlockSpec((None, tcs, w), lambda bi, i: (bi, blk(i), 0))
    in_specs = [
        seq_spec(D_SSM),
        pl.BlockSpec((None, ng, D_STATE, tcs), lambda bi, i: (bi, 0, 0, blk(i))),
        seq_spec(ng * D_STATE),
        pl.BlockSpec((ng, hg, tcs), lambda bi, i: (dirn, 0, bi * nblk + blk(i))),
        _const_spec((ng, hg, 1)), _const_spec((ng, hg, 1)),
    ]
    args = [xs, bt, cm, dtT, a_log.reshape(ng, hg, 1), dt_bias.reshape(ng, hg, 1)]
    if rev:
        y_fwd, z, norm_g = extra
        in_specs += [seq_spec(D_SSM), seq_spec(D_SSM), _const_spec((ng, 1, gw))]
        args += [y_fwd, z, norm_g.reshape(ng, 1, gw)]
    else:
        (d_skip,) = extra
        in_specs += [_const_spec((ng, 1, gw))]
        args += [jnp.repeat(d_skip, SSM_HEAD_DIM).reshape(ng, 1, gw)]
    return pl.pallas_call(
        functools.partial(_ssd_kernel, rev, nchunk),
        grid=(b, nblk),
        in_specs=in_specs,
        out_specs=seq_spec(D_SSM),
        out_shape=jax.ShapeDtypeStruct((b, s, D_SSM), F32),
        scratch_shapes=[pltpu.VMEM((ng, D_STATE, gw), F32)],
        compiler_params=pltpu.CompilerParams(dimension_semantics=("parallel", "arbitrary"),
                                             vmem_limit_bytes=VMEM_LIMIT),
        name="ssd_bwd" if rev else "ssd_fwd",
    )(*args)


NEG_BIG = -1e30


SKIP_LOG2 = 80.0
FIXED_SHIFT_MAX_LOG2 = 60.0


def _attn_plan(tq, tk, nk, nq, bi, hi, qi, slope, kn_ref, qn_ref, plan_i_ref, plan_f_ref):
    i0 = qi * tq
    jd = qi // (tk // tq)
    parts = tq // ATTN_QPART
    qns, bases = [], []
    for r in range(2):
        base = (bi * ATTN_HEADS + hi) * 2 + r
        qn = qn_ref[base * (nq * parts) + qi * parts]
        for part in range(1, parts):
            qn = jnp.maximum(qn, qn_ref[base * (nq * parts) + qi * parts + part])
        qns.append(qn)
        bases.append(base * nk)
    jlo, jhi = jd, jd
    kmax = [kn_ref[bases[0]], kn_ref[bases[1]]]
    for jt in range(nk):
        dist = jnp.maximum(jnp.maximum(i0 - (jt * tk + tk - 1), jt * tk - (i0 + tq - 1)), 0).astype(F32)
        need = None
        for r in range(2):
            kn = kn_ref[bases[r] + jt]
            kmax[r] = jnp.maximum(kmax[r], kn)
            reach = SKIP_LOG2 + qns[r] * (kn + kn_ref[bases[r] + jd])
            need_r = slope * dist <= reach
            need = need_r if need is None else (need | need_r)
        jlo = jnp.where(need, jnp.minimum(jlo, jt), jlo)
        jhi = jnp.where(need, jnp.maximum(jhi, jt), jhi)
    plan_i_ref[0] = jlo
    plan_i_ref[1] = jhi
    plan_f_ref[0] = jnp.maximum(qns[0] * kmax[0], qns[1] * kmax[1])


def _attn_q_tile(tq, tk, nk, nq, bi, hi, qi, slope, kn_ref, qn_ref, qT_ref, k_ref, vT_ref, lq1_ref, lk1_ref, lq2_ref,
                 lk2_ref, sg_ref, o_ref, e_ref, s_ref, s1_ref, acc_ref, l_ref, plan_i_ref, plan_f_ref):
    ratio = tk // tq
    i0 = qi * tq
    q0 = pl.multiple_of(i0, tq)
    jd = qi // ratio
    dsel = qi % ratio
    jlo, jhi, score_bound = plan_i_ref[0], plan_i_ref[1], plan_f_ref[0]

    def plan_next():
        _attn_plan(tq, tk, nk, nq, bi, hi, jnp.minimum(qi + 1, nq - 1), slope, kn_ref, qn_ref, plan_i_ref, plan_f_ref)

    row = lax.broadcasted_iota(jnp.int32, (2 * ATTN_HEAD_DIM, tq), 0)
    qf = qT_ref[:, pl.ds(q0, tq)].astype(F32)
    rhs = (jnp.where(row < ATTN_HEAD_DIM, qf, 0.0).astype(BF16),
           jnp.where(row >= ATTN_HEAD_DIM, qf, 0.0).astype(BF16))
    def finish_previous():
        _attn_epilogue(tq, jnp.maximum(qi - 1, 0), lq1_ref, lk1_ref, lq2_ref, lk2_ref, sg_ref, o_ref, acc_ref, l_ref)
        acc_ref[...] = jnp.zeros_like(acc_ref)

    def tile_bias(j):
        off = (i0 - j * tk).astype(F32)
        before, after = j < jd, j > jd
        eidx = jnp.where(before, 0, jnp.where(after, 1, 2 + dsel))
        cj = jnp.where(before, -slope * (off - float(tk - 1)), jnp.where(after, slope * (off + float(tq - 1)), 0.0))
        return eidx, cj

    @pl.when(score_bound <= FIXED_SHIFT_MAX_LOG2)
    def _():
        jl = (jlo // 2) * 2
        jh = (jhi // 2) * 2 + 1

        def stage_a(j, buf):
            kt = k_ref[pl.ds(pl.multiple_of(j * tk, tk), tk), :]
            for r in range(2):
                buf[r] = _dot(kt, rhs[r])

        def stage_b(j, buf, l):
            eidx, cj = tile_bias(j)
            half = jnp.exp2(jnp.full((1, tq), 0.5 * cj, F32))
            vt = vT_ref[:, pl.ds(pl.multiple_of(j * tk, tk), tk)]
            l_new = []
            for r in range(2):
                p = jnp.exp2(buf[r] + e_ref[eidx])
                l_new.append(l[r] + jnp.sum(p, axis=0, keepdims=True) * half * half)
                acc_ref[r] += _dot(vt, p.astype(BF16)) * half * half
            return tuple(l_new)

        def pair(j, l):
            stage_a(j + 1, s1_ref)
            l = stage_b(j, s_ref, l)
            stage_a(jnp.minimum(j + 2, jh), s_ref)
            return stage_b(j + 1, s1_ref, l)

        def body(t, l):
            j = jl + 4 * t
            return pair(j + 2, pair(j, l))

        stage_a(jl, s_ref)
        finish_previous()
        plan_next()
        zero = jnp.zeros((1, tq), F32)
        npairs = (jh - jl + 1) // 2
        l = lax.fori_loop(0, npairs // 2, body, (zero, zero))
        l_ref[0] = l[0]
        l_ref[1] = l[1]

        @pl.when(npairs % 2 == 1)
        def _():
            l = pair(jh - 1, (l_ref[0], l_ref[1]))
            l_ref[0] = l[0]
            l_ref[1] = l[1]

    def stage_a(j, m_old):
        eidx, cj = tile_bias(j)
        kt = k_ref[pl.ds(pl.multiple_of(j * tk, tk), tk), :]
        m_new, shift, alpha = [], [], []
        for r in range(2):
            s = _dot(kt, rhs[r]) + e_ref[eidx]
            s_ref[r] = s
            mr = jnp.maximum(m_old[r], jnp.max(s, axis=0, keepdims=True) + cj)
            m_new.append(mr)
            shift.append(mr - cj)
            alpha.append(jnp.exp2(m_old[r] - mr))
        return tuple(m_new), tuple(shift), tuple(alpha)

    def stage_b(j, shift, alpha, l_old):
        vt = vT_ref[:, pl.ds(pl.multiple_of(j * tk, tk), tk)]
        l_new = []
        for r in range(2):
            p = jnp.exp2(s_ref[r] - shift[r])
            l_new.append(alpha[r] * l_old[r] + jnp.sum(p, axis=0, keepdims=True))
            acc_ref[r] = alpha[r] * acc_ref[r] + _dot(vt, p.astype(BF16))
        return tuple(l_new)

    def body(t, carry):
        m, shift, alpha, l = carry
        j = jlo + t
        l = stage_b(jnp.maximum(j - 1, jlo), shift, alpha, l)
        m, shift, alpha = stage_a(jnp.minimum(j, jhi), m)
        return m, shift, alpha, l

    @pl.when(score_bound > FIXED_SHIFT_MAX_LOG2)
    def _():
        neg = jnp.full((1, tq), NEG_BIG, F32)
        big = jnp.full((1, tq), -NEG_BIG, F32)
        one = jnp.ones((1, tq), F32)
        zero = jnp.zeros((1, tq), F32)
        finish_previous()
        plan_next()
        init = ((neg, neg), (big, big), (one, one), (zero, zero))
        _, _, _, l = lax.fori_loop(0, jhi - jlo + 2, body, init)
        l_ref[0] = l[0]
        l_ref[1] = l[1]


def _attn_epilogue(tq, qi, lq1_ref, lk1_ref, lq2_ref, lk2_ref, sg_ref, o_ref, acc_ref, l_ref):
    lam = (jnp.exp(jnp.sum(lq1_ref[...] * lk1_ref[...], axis=-1, keepdims=True))
           - jnp.exp(jnp.sum(lq2_ref[...] * lk2_ref[...], axis=-1, keepdims=True)) + LAM_INIT)
    o = acc_ref[0] / l_ref[0] - lam * (acc_ref[1] / l_ref[1])
    ms = jnp.mean(o * o, axis=0, keepdims=True)
    q0 = pl.multiple_of(qi * tq, tq)
    o_ref[pl.ds(q0, tq), :] = (o * lax.rsqrt(ms + RMS_EPS) * sg_ref[...] * (1.0 - LAM_INIT)).T


def _attn_kernel(tq, tk, nk, nq, slopes_ref, kn_ref, qn_ref, qT_ref, k_ref, vT_ref, lq1_ref, lk1_ref, lq2_ref,
                 lk2_ref, sg_ref, o_ref, e_ref, s_ref, s1_ref, acc_ref, l_ref, plan_i_ref, plan_f_ref):
    bi, hi = pl.program_id(0), pl.program_id(1)
    slope = slopes_ref[hi]
    ii = lax.broadcasted_iota(jnp.int32, (tk, tq), 1)
    jj = lax.broadcasted_iota(jnp.int32, (tk, tq), 0)
    rel = (ii - jj).astype(F32)
    e_ref[0] = -slope * (rel + float(tk - 1))
    e_ref[1] = slope * (rel - float(tq - 1))
    for d in range(tk // tq):
        e_ref[2 + d] = -slope * jnp.abs(rel + float(d * tq))
    s_ref[...] = jnp.zeros_like(s_ref)
    acc_ref[...] = jnp.zeros_like(acc_ref)
    l_ref[...] = jnp.ones_like(l_ref)
    _attn_plan(tq, tk, nk, nq, bi, hi, 0, slope, kn_ref, qn_ref, plan_i_ref, plan_f_ref)

    def q_tile(qi, carry):
        _attn_q_tile(tq, tk, nk, nq, bi, hi, qi, slope, kn_ref, qn_ref, qT_ref, k_ref, vT_ref, lq1_ref, lk1_ref,
                     lq2_ref, lk2_ref, sg_ref, o_ref, e_ref, s_ref, s1_ref, acc_ref, l_ref, plan_i_ref, plan_f_ref)
        return carry

    lax.fori_loop(0, nq, q_tile, 0)
    _attn_epilogue(tq, nq - 1, lq1_ref, lk1_ref, lq2_ref, lk2_ref, sg_ref, o_ref, acc_ref, l_ref)


def _diff_attn(qT, k3, vT, kn, qn, lq1, lk1, lq2, lk2, subln_g, tq=ATTN_TQ, tk=ATTN_TK):
    b, s, _ = k3.shape
    nq, nk = s // tq, s // tk
    ratio = tk // tq
    assert tk % tq == 0 and nk % 2 == 0
    hw = 2 * ATTN_HEAD_DIM
    slopes = jnp.exp2(-8.0 * (jnp.arange(ATTN_HEADS, dtype=F32) + 1.0) / ATTN_HEADS) * LOG2E
    kn_tab = jnp.transpose(kn[:, 0, :N_MAPS].reshape(b, nk, N_MAPS), (0, 2, 1)).reshape(-1)
    kparts = tk // ATTN_QPART
    qn_tab = jnp.transpose(qn[:, :, :kparts].reshape(b, nk, N_MAPS, kparts), (0, 2, 1, 3)).reshape(-1)
    vec = lambda a: a.reshape(1, -1)
    return pl.pallas_call(
        functools.partial(_attn_kernel, tq, tk, nk, nq),
        grid=(b, ATTN_HEADS),
        in_specs=[
            pl.BlockSpec(memory_space=pltpu.SMEM),
            pl.BlockSpec(memory_space=pltpu.SMEM),
            pl.BlockSpec(memory_space=pltpu.SMEM),
            pl.BlockSpec((hw, s), lambda bi, hi: (hi, bi)),
            pl.BlockSpec((None, s, hw), lambda bi, hi: (bi, 0, hi)),
            pl.BlockSpec((ATTN_V_DIM, s), lambda bi, hi: (hi, bi)),
            _const_spec((1, ATTN_HEAD_DIM)), _const_spec((1, ATTN_HEAD_DIM)),
            _const_spec((1, ATTN_HEAD_DIM)), _const_spec((1, ATTN_HEAD_DIM)),
            _const_spec((ATTN_V_DIM, 1)),
        ],
        out_specs=pl.BlockSpec((None, s, ATTN_V_DIM), lambda bi, hi: (bi, 0, hi)),
        out_shape=jax.ShapeDtypeStruct((b, s, ATTN_HEADS * ATTN_V_DIM), F32),
        scratch_shapes=[pltpu.VMEM((2 + tk // tq, tk, tq), F32),
                        pltpu.VMEM((2, tk, tq), F32), pltpu.VMEM((2, tk, tq), F32),
                        pltpu.VMEM((2, ATTN_V_DIM, tq), F32), pltpu.VMEM((2, 1, tq), F32),
                        pltpu.SMEM((2,), jnp.int32), pltpu.SMEM((1,), F32)],
        compiler_params=pltpu.CompilerParams(dimension_semantics=("parallel", "parallel"),
                                             vmem_limit_bytes=VMEM_LIMIT),
        name="diff_attn",
    )(slopes, kn_tab, qn_tab, qT, k3, vT, vec(lq1), vec(lk1), vec(lq2), vec(lk2), subln_g.reshape(-1, 1))


def _out_proj_kernel(x_ref, eg_ref, eb_ref, ys_ref, o_ref, gate_ref, w_ref, g_ref, b_ref, out_ref):
    h = _layer_norm(x_ref[...], eg_ref[...], eb_ref[...])
    ya = o_ref[...] * _silu(gate_ref[...])
    mix = _dot(ys_ref[...].astype(BF16), w_ref[:D_SSM, :]) + _dot(ya.astype(BF16), w_ref[D_SSM:, :])
    out_ref[...] = _layer_norm(ALPHA * h + mix, g_ref[...], b_ref[...])


def _out_proj(x2d, ln_emb_g, ln_emb_b, y_ssm, o, gate, w_out, ln_g, ln_b, tm=512):
    t = x2d.shape[0]
    row = lambda n: pl.BlockSpec((tm, n), lambda i: (i, 0))
    vec = lambda a: a.reshape(1, -1)
    return pl.pallas_call(
        _out_proj_kernel,
        grid=(t // tm,),
        in_specs=[row(D_MODEL), _const_spec((1, D_MODEL)), _const_spec((1, D_MODEL)),
                  row(D_SSM), row(D_ATTN), row(D_ATTN),
                  _const_spec((D_SSM + D_ATTN, D_MODEL)), _const_spec((1, D_MODEL)), _const_spec((1, D_MODEL))],
        out_specs=row(D_MODEL),
        out_shape=jax.ShapeDtypeStruct((t, D_MODEL), F32),
        compiler_params=pltpu.CompilerParams(dimension_semantics=("parallel",), vmem_limit_bytes=VMEM_LIMIT),
        name="out_proj",
    )(x2d, vec(ln_emb_g), vec(ln_emb_b), y_ssm, o, gate, w_out.astype(BF16), vec(ln_g), vec(ln_b))


def kernel(x, ln_emb_g, ln_emb_b, w_in, conv_w, conv_b, A_log_fwd, A_log_bwd, dt_bias_fwd, dt_bias_bwd, D_skip,
           ssm_norm_g, lambda_q1, lambda_k1, lambda_q2, lambda_k2, subln_g, w_out, ln_g, ln_b):
    b, s, _ = x.shape
    t = b * s
    x2d = x.reshape(t, D_MODEL)
    z, xbc, dtT, qT, k, vT, gate, kn, qn = _ln_inproj(x2d, ln_emb_g, ln_emb_b, w_in[0])

    xs, bt, cm = _conv_silu(xbc.reshape(b, s, D_XBC), conv_w[0], conv_b[0])
    y_fwd = _ssd_pass(False, xs, bt, cm, dtT, A_log_fwd[0], dt_bias_fwd[0], (D_skip[0],))
    y_ssm = _ssd_pass(True, xs, bt, cm, dtT, A_log_bwd[0], dt_bias_bwd[0],
                      (y_fwd, z.reshape(b, s, D_SSM), ssm_norm_g[0]))

    o = _diff_attn(qT, k.reshape(b, s, D_ATTN), vT, kn, qn, lambda_q1[0], lambda_k1[0], lambda_q2[0],
                   lambda_k2[0], subln_g[0]).reshape(t, D_ATTN)

    out = _out_proj(x2d, ln_emb_g, ln_emb_b, y_ssm.reshape(t, D_SSM), o, gate, w_out[0], ln_g[0], ln_b[0])
    return out.reshape(b, s, D_MODEL)
```

```python
import functools
import math

import jax
import jax.numpy as jnp
from jax import lax
from jax.experimental import pallas as pl
from jax.experimental.pallas import tpu as pltpu

D_MODEL = 1024
D_SSM = 1024
SSM_HEAD_DIM = 64
SSM_HEADS = 16
SSM_GROUPS = 2
HEADS_PER_GROUP = SSM_HEADS // SSM_GROUPS
D_STATE = 128
D_CONV = 5
CHUNK = 128
D_XBC = D_SSM + 2 * SSM_GROUPS * D_STATE
D_ATTN = 1024
ATTN_HEADS = 8
ATTN_HEAD_DIM = 64
ATTN_V_DIM = 128
GROUP_WIDTH = D_SSM // SSM_GROUPS
DEPTH = 1
ALPHA = (2.0 * DEPTH) ** 0.25
LN_EPS = 1e-5
RMS_EPS = 1e-5
LAM_INIT = 0.8 - 0.6 * math.exp(-0.3 * 0)

LANES = 128
VMEM_LIMIT = 48 * 1024 * 1024

F32 = jnp.float32
BF16 = jnp.bfloat16


LOG2E = math.log2(math.e)


def _dot(a, b):
    return jnp.dot(a, b, preferred_element_type=F32)


def _dot_nt(a, b):
    return lax.dot_general(a, b, (((1,), (1,)), ((), ())), preferred_element_type=F32)


def _layer_norm(x, g, b):
    mu = jnp.mean(x, axis=-1, keepdims=True)
    xc = x - mu
    var = jnp.mean(xc * xc, axis=-1, keepdims=True)
    return xc * lax.rsqrt(var + LN_EPS) * g + b


def _silu(x):
    return x / (1.0 + jnp.exp(-x))


def _softplus(x):
    return jnp.maximum(x, 0.0) + jnp.log1p(jnp.exp(-jnp.abs(x)))


def _split3(x):
    hi = x.astype(BF16)
    r = x - hi.astype(F32)
    mid = r.astype(BF16)
    lo = (r - mid.astype(F32)).astype(BF16)
    return hi, mid, lo


def _dot_exact_rhs(m, x):
    hi, mid, lo = _split3(x)
    return _dot(m, hi) + _dot(m, mid) + _dot(m, lo)


def _dot_exact_lhs(x, m):
    hi, mid, lo = _split3(x)
    return _dot(hi, m) + _dot(mid, m) + _dot(lo, m)


def _const_spec(shape):
    nd = len(shape)
    return pl.BlockSpec(shape, lambda *_: (0,) * nd)


N_DT_TILES = 2 * SSM_GROUPS


ATTN_TQ = 256
ATTN_QPART = 256
ATTN_TK = 512
N_MAPS = 2 * ATTN_HEADS
NORM_SLACK = 1.01


def _ln_inproj_kernel(x_ref, g_ref, b_ref, wz_ref, wxbc_ref, wdtT_ref, wqT_ref, wk_ref, wvT_ref, wg_ref,
                      grp_ref, grpT_ref,
                      z_ref, xbc_ref, dtT_ref, qT_ref, k_ref, vT_ref, gate_ref, kn_ref, qn_ref):
    h = _layer_norm(x_ref[...], g_ref[...], b_ref[...]).astype(BF16)
    z_ref[...] = _dot(h, wz_ref[...])
    xbc_ref[...] = _dot(h, wxbc_ref[...])
    dtT_ref[...] = _dot_nt(wdtT_ref[...], h).reshape(dtT_ref.shape)
    qT = _dot_nt(wqT_ref[...], h) * (LOG2E * ATTN_HEAD_DIM ** -0.5)
    qT_ref[...] = qT.astype(BF16)
    k = _dot(h, wk_ref[...])
    k_ref[...] = k.astype(BF16)
    vT_ref[...] = _dot_nt(wvT_ref[...], h).astype(BF16)
    gate_ref[...] = _dot(h, wg_ref[...])
    k2 = _dot((k * k).astype(BF16), grp_ref[...])
    kn_ref[...] = jnp.sqrt(jnp.max(k2, axis=0, keepdims=True)) * NORM_SLACK
    q2 = _dot(grpT_ref[...], (qT * qT).astype(BF16))[:N_MAPS, :]
    lane = lax.broadcasted_iota(jnp.int32, (N_MAPS, LANES), 1)
    qn = jnp.zeros((N_MAPS, LANES), F32)
    for part in range(q2.shape[1] // ATTN_QPART):
        pm = jnp.max(q2[:, part * ATTN_QPART:(part + 1) * ATTN_QPART], axis=1, keepdims=True)
        qn = jnp.where(lane == part, jnp.sqrt(pm) * NORM_SLACK, qn)
    qn_ref[...] = qn


def _ln_inproj(x2d, ln_g, ln_b, w_in, tm=ATTN_TK):
    t = x2d.shape[0]
    grp = (jnp.arange(D_ATTN)[:, None] // ATTN_HEAD_DIM == jnp.arange(LANES)[None, :]).astype(BF16)
    offs = [0, D_SSM, D_SSM + D_XBC, D_SSM + D_XBC + 2 * SSM_HEADS]
    wz = w_in[:, offs[0]:offs[1]].astype(BF16)
    wxbc = w_in[:, offs[1]:offs[2]].astype(BF16)
    wdt = w_in[:, offs[2]:offs[3]].astype(BF16)
    o = offs[3]
    wq, wk, wv, wg = (w_in[:, o + i * D_ATTN:o + (i + 1) * D_ATTN].astype(BF16) for i in range(4))
    row = lambda n: pl.BlockSpec((tm, n), lambda i: (i, 0))
    col = lambda n: pl.BlockSpec((n, tm), lambda i: (0, i))
    weights = [wz, wxbc, wdt.T, wq.T, wk, wv.T, wg, grp, grp.T]
    nt = t // tm
    return pl.pallas_call(
        _ln_inproj_kernel,
        grid=(nt,),
        in_specs=[row(D_MODEL), _const_spec((1, D_MODEL)), _const_spec((1, D_MODEL))]
                 + [pl.BlockSpec(w.shape, lambda i: (0, 0), pipeline_mode=pl.Buffered(1)) for w in weights],
        out_specs=[row(D_SSM), row(D_XBC),
                   pl.BlockSpec((N_DT_TILES, HEADS_PER_GROUP, tm), lambda i: (0, 0, i)),
                   col(D_ATTN), row(D_ATTN), col(D_ATTN), row(D_ATTN),
                   pl.BlockSpec((None, 1, LANES), lambda i: (i, 0, 0)),
                   pl.BlockSpec((None, N_MAPS, LANES), lambda i: (i, 0, 0))],
        out_shape=[jax.ShapeDtypeStruct((t, D_SSM), F32), jax.ShapeDtypeStruct((t, D_XBC), F32),
                   jax.ShapeDtypeStruct((N_DT_TILES, HEADS_PER_GROUP, t), F32),
                   jax.ShapeDtypeStruct((D_ATTN, t), BF16), jax.ShapeDtypeStruct((t, D_ATTN), BF16),
                   jax.ShapeDtypeStruct((D_ATTN, t), BF16), jax.ShapeDtypeStruct((t, D_ATTN), F32),
                   jax.ShapeDtypeStruct((nt, 1, LANES), F32), jax.ShapeDtypeStruct((nt, N_MAPS, LANES), F32)],
        compiler_params=pltpu.CompilerParams(dimension_semantics=("parallel",), vmem_limit_bytes=VMEM_LIMIT),
        name="ln_inproj",
    )(x2d, ln_g.reshape(1, -1), ln_b.reshape(1, -1), *weights)


HALO = 8


CONV_ROWS = 256


def _conv_silu_kernel(prev_ref, cur_ref, next_ref, w_ref, b_ref, xs_ref, bt_ref, c_ref, ext_ref):
    i = pl.program_id(1)
    n = pl.num_programs(1)
    tc = cur_ref.shape[0]
    ext_ref[0:HALO, :] = jnp.where(i == 0, 0.0, prev_ref[...])
    ext_ref[HALO:HALO + tc, :] = cur_ref[...]
    ext_ref[HALO + tc:, :] = jnp.where(i == n - 1, 0.0, next_ref[...])
    for cb in range(D_XBC // LANES):
        cols = slice(cb * LANES, (cb + 1) * LANES)
        for r0 in range(0, tc, CONV_ROWS):
            acc = b_ref[:, cols]
            for kk in range(D_CONV):
                acc = acc + ext_ref[pl.ds(r0 + HALO - D_CONV // 2 + kk, CONV_ROWS), cols] * w_ref[kk:kk + 1, cols]
            act = _silu(acc)
            rows = slice(r0, r0 + CONV_ROWS)
            if cb < D_SSM // LANES:
                xs_ref[rows, cols] = act
            elif cb < D_SSM // LANES + SSM_GROUPS:
                bt_ref[cb - D_SSM // LANES, :, rows] = act.T.astype(BF16)
            else:
                c_ref[rows, (cb - D_SSM // LANES - SSM_GROUPS) * LANES:(cb - D_SSM // LANES - SSM_GROUPS + 1) * LANES] = (
                    act.astype(BF16))


def _conv_silu(xbc, conv_w, conv_b, tc=512):
    b, s, _ = xbc.shape
    nb = s // tc
    hb = tc // HALO
    return pl.pallas_call(
        _conv_silu_kernel,
        grid=(b, nb),
        in_specs=[
            pl.BlockSpec((None, HALO, D_XBC), lambda bi, i: (bi, jnp.maximum(i * hb - 1, 0), 0)),
            pl.BlockSpec((None, tc, D_XBC), lambda bi, i: (bi, i, 0)),
            pl.BlockSpec((None, HALO, D_XBC), lambda bi, i: (bi, jnp.minimum((i + 1) * hb, s // HALO - 1), 0)),
            _const_spec((D_CONV, D_XBC)),
            _const_spec((1, D_XBC)),
        ],
        out_specs=[
            pl.BlockSpec((None, tc, D_SSM), lambda bi, i: (bi, i, 0)),
            pl.BlockSpec((None, SSM_GROUPS, D_STATE, tc), lambda bi, i: (bi, 0, 0, i)),
            pl.BlockSpec((None, tc, SSM_GROUPS * D_STATE), lambda bi, i: (bi, i, 0)),
        ],
        out_shape=[
            jax.ShapeDtypeStruct((b, s, D_SSM), F32),
            jax.ShapeDtypeStruct((b, SSM_GROUPS, D_STATE, s), BF16),
            jax.ShapeDtypeStruct((b, s, SSM_GROUPS * D_STATE), BF16),
        ],
        scratch_shapes=[pltpu.VMEM((tc + 2 * HALO, D_XBC), F32)],
        compiler_params=pltpu.CompilerParams(dimension_semantics=("parallel", "parallel"),
                                             vmem_limit_bytes=VMEM_LIMIT),
        name="conv_silu",
    )(xbc, xbc, xbc, conv_w, conv_b.reshape(1, -1))


SSD_CHUNKS_IN_FLIGHT = 2


def _ssd_bodies(rev, xs, bts, cms, dtTs, alogs, biases, groups, states):
    L = CHUNK
    r = lax.broadcasted_iota(jnp.int32, (L, L), 0)
    c = lax.broadcasted_iota(jnp.int32, (L, L), 1)
    keep = (c >= r) if rev else (c <= r)
    cum_r = ((r >= c) if rev else (r <= c)).astype(BF16)
    hg = HEADS_PER_GROUP
    eh = lax.broadcasted_iota(jnp.int32, (6 * hg, 2 * GROUP_WIDTH), 0)
    el = lax.broadcasted_iota(jnp.int32, (6 * hg, 2 * GROUP_WIDTH), 1)
    spread = ((el // GROUP_WIDTH == eh // (3 * hg)) & ((el % GROUP_WIDTH) // SSM_HEAD_DIM == eh % hg)).astype(BF16)
    lane = lax.broadcasted_iota(jnp.int32, (L, LANES), 1)
    edge = 0 if rev else L - 1
    gs = range(len(xs))

    dt_t = [_softplus(dtTs[g] + biases[g]) for g in gs]
    acs_t = [_dot_exact_lhs(dt_t[g] * (-jnp.exp(alogs[g])), cum_r) for g in gs]
    pieces_t = [jnp.concatenate([p.astype(F32) for p in _split3(dt_t[g]) + _split3(acs_t[g])], axis=0) for g in gs]
    both_x = [_dot(pieces_t[g].T.astype(BF16), spread) for g in gs]
    acs = [acs_t[g].T for g in gs]
    cb = [_dot(cms[g], bts[g]) for g in gs]
    dt_x = [both_x[g][:, :GROUP_WIDTH] for g in gs]
    acs_x = [both_x[g][:, GROUP_WIDTH:] for g in gs]
    tot_x = [acs_x[g][edge:edge + 1, :] for g in gs]
    xdt = [xs[g] * dt_x[g] for g in gs]

    y_diag = []
    for g in gs:
        tiles = []
        for t in range(GROUP_WIDTH // LANES):
            xt = xdt[g][:, t * LANES:(t + 1) * LANES]
            gmats, xparts = [], []
            for half in range(LANES // SSM_HEAD_DIM):
                j = t * (LANES // SSM_HEAD_DIM) + half
                seg = acs[g][:, j:j + 1] - acs_t[g][j:j + 1, :]
                gmats.append((cb[g] * jnp.exp(jnp.where(keep, seg, -jnp.inf))).astype(BF16))
                in_head = (lane >= half * SSM_HEAD_DIM) & (lane < (half + 1) * SSM_HEAD_DIM)
                xparts.append(jnp.where(in_head, xt, 0.0).astype(BF16))
            tiles.append(_dot(jnp.concatenate(gmats, axis=1), jnp.concatenate(xparts, axis=0)))
        y_diag.append(jnp.concatenate(tiles, axis=1))

    states = list(states)
    ys = []
    for g in gs:
        grp = groups[g]
        y_off = _dot(cms[g], states[grp].astype(BF16)) * jnp.exp(acs_x[g])
        ys.append(y_off + y_diag[g])
        w = (xdt[g] * jnp.exp(tot_x[g] - acs_x[g])).astype(BF16)
        states[grp] = jnp.exp(tot_x[g]) * states[grp] + _dot(bts[g], w)
    return ys, states


def _ssd_kernel(rev, nchunk, *refs):
    if rev:
        (xs_ref, bt_ref, c_ref, dtT_ref, alc_ref, bc_ref, yf_ref, z_ref, ng_ref, y_ref, state_ref) = refs
    else:
        (xs_ref, bt_ref, c_ref, dtT_ref, alc_ref, bc_ref, dx_ref, y_ref, state_ref) = refs

    @pl.when(pl.program_id(1) == 0)
    def _():
        state_ref[...] = jnp.zeros_like(state_ref)

    gw = GROUP_WIDTH
    gs = range(SSM_GROUPS)
    cols = [slice(g * gw, (g + 1) * gw) for g in gs]
    order = list(range(nchunk - 1, -1, -1) if rev else range(nchunk))
    states = [state_ref[g] for g in gs]
    for c0 in range(0, nchunk, SSD_CHUNKS_IN_FLIGHT):
        bodies = [(slice(ci * CHUNK, (ci + 1) * CHUNK), g) for ci in order[c0:c0 + SSD_CHUNKS_IN_FLIGHT] for g in gs]
        x = [xs_ref[rows, cols[g]] for rows, g in bodies]
        ys, states = _ssd_bodies(rev, x, [bt_ref[g, :, rows] for rows, g in bodies],
                                 [c_ref[rows, g * D_STATE:(g + 1) * D_STATE] for rows, g in bodies],
                                 [dtT_ref[g, :, rows] for rows, g in bodies],
                                 [alc_ref[g] for _, g in bodies], [bc_ref[g] for _, g in bodies],
                                 [g for _, g in bodies], states)
        for (rows, g), xb, y in zip(bodies, x, ys):
            if rev:
                yy = (yf_ref[rows, cols[g]] + y) * _silu(z_ref[rows, cols[g]])
                ms = jnp.mean(yy * yy, axis=-1, keepdims=True)
                y_ref[rows, cols[g]] = yy * lax.rsqrt(ms + RMS_EPS) * ng_ref[g]
            else:
                y_ref[rows, cols[g]] = y + dx_ref[g] * xb
    for g in gs:
        state_ref[g] = states[g]


def _ssd_pass(rev, xs, bt, cm, dtT, a_log, dt_bias, extra, nchunk=4):
    b, s, _ = xs.shape
    tcs = nchunk * CHUNK
    nblk = s // tcs
    blk = (lambda i: nblk - 1 - i) if rev else (lambda i: i)
    dirn = 1 if rev else 0
    gw, hg, ng = GROUP_WIDTH, HEADS_PER_GROUP, SSM_GROUPS
    seq_spec = lambda w: pl.BlockSpec((None, tcs, w), lambda bi, i: (bi, blk(i), 0))
    in_specs = [
        seq_spec(D_SSM),
        pl.BlockSpec((None, ng, D_STATE, tcs), lambda bi, i: (bi, 0, 0, blk(i))),
        seq_spec(ng * D_STATE),
        pl.BlockSpec((ng, hg, tcs), lambda bi, i: (dirn, 0, bi * nblk + blk(i))),
        _const_spec((ng, hg, 1)), _const_spec((ng, hg, 1)),
    ]
    args = [xs, bt, cm, dtT, a_log.reshape(ng, hg, 1), dt_bias.reshape(ng, hg, 1)]
    if rev:
        y_fwd, z, norm_g = extra
        in_specs += [seq_spec(D_SSM), seq_spec(D_SSM), _const_spec((ng, 1, gw))]
        args += [y_fwd, z, norm_g.reshape(ng, 1, gw)]
    else:
        (d_skip,) = extra
        in_specs += [_const_spec((ng, 1, gw))]
        args += [jnp.repeat(d_skip, SSM_HEAD_DIM).reshape(ng, 1, gw)]
    return pl.pallas_call(
        functools.partial(_ssd_kernel, rev, nchunk),
        grid=(b, nblk),
        in_specs=in_specs,
        out_specs=seq_spec(D_SSM),
        out_shape=jax.ShapeDtypeStruct((b, s, D_SSM), F32),
        scratch_shapes=[pltpu.VMEM((ng, D_STATE, gw), F32)],
        compiler_params=pltpu.CompilerParams(dimension_semantics=("parallel", "arbitrary"),
                                             vmem_limit_bytes=VMEM_LIMIT),
        name="ssd_bwd" if rev else "ssd_fwd",
    )(*args)


NEG_BIG = -1e30


SKIP_LOG2 = 80.0
FIXED_SHIFT_MAX_LOG2 = 60.0


def _attn_plan(tq, tk, nk, nq, bi, hi, qi, slope, kn_ref, qn_ref, plan_i_ref, plan_f_ref):
    i0 = qi * tq
    jd = qi // (tk // tq)
    parts = tq // ATTN_QPART
    qns, bases = [], []
    for r in range(2):
        base = (bi * ATTN_HEADS + hi) * 2 + r
        qn = qn_ref[base * (nq * parts) + qi * parts]
        for part in range(1, parts):
            qn = jnp.maximum(qn, qn_ref[base * (nq * parts) + qi * parts + part])
        qns.append(qn)
        bases.append(base * nk)
    jlo, jhi = jd, jd
    kmax = [kn_ref[bases[0]], kn_ref[bases[1]]]
    for jt in range(nk):
        dist = jnp.maximum(jnp.maximum(i0 - (jt * tk + tk - 1), jt * tk - (i0 + tq - 1)), 0).astype(F32)
        need = None
        for r in range(2):
            kn = kn_ref[bases[r] + jt]
            kmax[r] = jnp.maximum(kmax[r], kn)
            reach = SKIP_LOG2 + qns[r] * (kn + kn_ref[bases[r] + jd])
            need_r = slope * dist <= reach
            need = need_r if need is None else (need | need_r)
        jlo = jnp.where(need, jnp.minimum(jlo, jt), jlo)
        jhi = jnp.where(need, jnp.maximum(jhi, jt), jhi)
    plan_i_ref[0] = jlo
    plan_i_ref[1] = jhi
    plan_f_ref[0] = jnp.maximum(qns[0] * kmax[0], qns[1] * kmax[1])


def _attn_q_tile(tq, tk, nk, nq, bi, hi, qi, slope, kn_ref, qn_ref, qT_ref, k_ref, vT_ref, lq1_ref, lk1_ref, lq2_ref,
                 lk2_ref, sg_ref, o_ref, e_ref, s_ref, s1_ref, acc_ref, l_ref, plan_i_ref, plan_f_ref):
    ratio = tk // tq
    i0 = qi * tq
    q0 = pl.multiple_of(i0, tq)
    jd = qi // ratio
    dsel = qi % ratio
    jlo, jhi, score_bound = plan_i_ref[0], plan_i_ref[1], plan_f_ref[0]

    def plan_next():
        _attn_plan(tq, tk, nk, nq, bi, hi, jnp.minimum(qi + 1, nq - 1), slope, kn_ref, qn_ref, plan_i_ref, plan_f_ref)

    row = lax.broadcasted_iota(jnp.int32, (2 * ATTN_HEAD_DIM, tq), 0)
    qf = qT_ref[:, pl.ds(q0, tq)].astype(F32)
    rhs = (jnp.where(row < ATTN_HEAD_DIM, qf, 0.0).astype(BF16),
           jnp.where(row >= ATTN_HEAD_DIM, qf, 0.0).astype(BF16))
    def finish_previous():
        _attn_epilogue(tq, jnp.maximum(qi - 1, 0), lq1_ref, lk1_ref, lq2_ref, lk2_ref, sg_ref, o_ref, acc_ref, l_ref)
        acc_ref[...] = jnp.zeros_like(acc_ref)

    def tile_bias(j):
        off = (i0 - j * tk).astype(F32)
        before, after = j < jd, j > jd
        eidx = jnp.where(before, 0, jnp.where(after, 1, 2 + dsel))
        cj = jnp.where(before, -slope * (off - float(tk - 1)), jnp.where(after, slope * (off + float(tq - 1)), 0.0))
        return eidx, cj

    @pl.when(score_bound <= FIXED_SHIFT_MAX_LOG2)
    def _():
        odd = (jhi - jlo + 1) % 2
        room_above = (jhi < nk - 1).astype(jnp.int32)
        jl = jlo - odd * (1 - room_above)
        jh = jhi + odd * room_above

        def stage_a(j, buf):
            kt = k_ref[pl.ds(pl.multiple_of(j * tk, tk), tk), :]
            for r in range(2):
                buf[r] = _dot(kt, rhs[r])

        def stage_b(j, buf, l):
            eidx, cj = tile_bias(j)
            half = jnp.exp2(jnp.full((1, tq), 0.5 * cj, F32))
            vt = vT_ref[:, pl.ds(pl.multiple_of(j * tk, tk), tk)]
            l_new = []
            for r in range(2):
                p = jnp.exp2(buf[r] + e_ref[eidx])
                l_new.append(l[r] + jnp.sum(p, axis=0, keepdims=True) * half * half)
                acc_ref[r] += _dot(vt, p.astype(BF16)) * half * half
            return tuple(l_new)

        def pair(j, l):
            stage_a(j + 1, s1_ref)
            l = stage_b(j, s_ref, l)
            stage_a(jnp.minimum(j + 2, jh), s_ref)
            return stage_b(j + 1, s1_ref, l)

        def body(t, l):
            j = jl + 4 * t
            return pair(j + 2, pair(j, l))

        stage_a(jl, s_ref)
        finish_previous()
        plan_next()
        zero = jnp.zeros((1, tq), F32)
        npairs = (jh - jl + 1) // 2
        l = lax.fori_loop(0, npairs // 2, body, (zero, zero))
        l_ref[0] = l[0]
        l_ref[1] = l[1]

        @pl.when(npairs % 2 == 1)
        def _():
            l = pair(jh - 1, (l_ref[0], l_ref[1]))
            l_ref[0] = l[0]
            l_ref[1] = l[1]

    def stage_a(j, m_old):
        eidx, cj = tile_bias(j)
        kt = k_ref[pl.ds(pl.multiple_of(j * tk, tk), tk), :]
        m_new, shift, alpha = [], [], []
        for r in range(2):
            s = _dot(kt, rhs[r]) + e_ref[eidx]
            s_ref[r] = s
            mr = jnp.maximum(m_old[r], jnp.max(s, axis=0, keepdims=True) + cj)
            m_new.append(mr)
            shift.append(mr - cj)
            alpha.append(jnp.exp2(m_old[r] - mr))
        return tuple(m_new), tuple(shift), tuple(alpha)

    def stage_b(j, shift, alpha, l_old):
        vt = vT_ref[:, pl.ds(pl.multiple_of(j * tk, tk), tk)]
        l_new = []
        for r in range(2):
            p = jnp.exp2(s_ref[r] - shift[r])
            l_new.append(alpha[r] * l_old[r] + jnp.sum(p, axis=0, keepdims=True))
            acc_ref[r] = alpha[r] * acc_ref[r] + _dot(vt, p.astype(BF16))
        return tuple(l_new)

    def body(t, carry):
        m, shift, alpha, l = carry
        j = jlo + t
        l = stage_b(jnp.maximum(j - 1, jlo), shift, alpha, l)
        m, shift, alpha = stage_a(jnp.minimum(j, jhi), m)
        return m, shift, alpha, l

    @pl.when(score_bound > FIXED_SHIFT_MAX_LOG2)
    def _():
        neg = jnp.full((1, tq), NEG_BIG, F32)
        big = jnp.full((1, tq), -NEG_BIG, F32)
        one = jnp.ones((1, tq), F32)
        zero = jnp.zeros((1, tq), F32)
        finish_previous()
        plan_next()
        init = ((neg, neg), (big, big), (one, one), (zero, zero))
        _, _, _, l = lax.fori_loop(0, jhi - jlo + 2, body, init)
        l_ref[0] = l[0]
        l_ref[1] = l[1]


def _attn_epilogue(tq, qi, lq1_ref, lk1_ref, lq2_ref, lk2_ref, sg_ref, o_ref, acc_ref, l_ref):
    lam = (jnp.exp(jnp.sum(lq1_ref[...] * lk1_ref[...], axis=-1, keepdims=True))
           - jnp.exp(jnp.sum(lq2_ref[...] * lk2_ref[...], axis=-1, keepdims=True)) + LAM_INIT)
    o = acc_ref[0] / l_ref[0] - lam * (acc_ref[1] / l_ref[1])
    ms = jnp.mean(o * o, axis=0, keepdims=True)
    q0 = pl.multiple_of(qi * tq, tq)
    o_ref[pl.ds(q0, tq), :] = (o * lax.rsqrt(ms + RMS_EPS) * sg_ref[...] * (1.0 - LAM_INIT)).T


def _attn_kernel(tq, tk, nk, nq, slopes_ref, kn_ref, qn_ref, qT_ref, k_ref, vT_ref, lq1_ref, lk1_ref, lq2_ref,
                 lk2_ref, sg_ref, o_ref, e_ref, s_ref, s1_ref, acc_ref, l_ref, plan_i_ref, plan_f_ref):
    bi, hi = pl.program_id(0), pl.program_id(1)
    slope = slopes_ref[hi]
    ii = lax.broadcasted_iota(jnp.int32, (tk, tq), 1)
    jj = lax.broadcasted_iota(jnp.int32, (tk, tq), 0)
    rel = (ii - jj).astype(F32)
    e_ref[0] = -slope * (rel + float(tk - 1))
    e_ref[1] = slope * (rel - float(tq - 1))
    for d in range(tk // tq):
        e_ref[2 + d] = -slope * jnp.abs(rel + float(d * tq))
    s_ref[...] = jnp.zeros_like(s_ref)
    acc_ref[...] = jnp.zeros_like(acc_ref)
    l_ref[...] = jnp.ones_like(l_ref)
    _attn_plan(tq, tk, nk, nq, bi, hi, 0, slope, kn_ref, qn_ref, plan_i_ref, plan_f_ref)

    def q_tile(qi, carry):
        _attn_q_tile(tq, tk, nk, nq, bi, hi, qi, slope, kn_ref, qn_ref, qT_ref, k_ref, vT_ref, lq1_ref, lk1_ref,
                     lq2_ref, lk2_ref, sg_ref, o_ref, e_ref, s_ref, s1_ref, acc_ref, l_ref, plan_i_ref, plan_f_ref)
        return carry

    lax.fori_loop(0, nq, q_tile, 0)
    _attn_epilogue(tq, nq - 1, lq1_ref, lk1_ref, lq2_ref, lk2_ref, sg_ref, o_ref, acc_ref, l_ref)


def _diff_attn(qT, k3, vT, kn, qn, lq1, lk1, lq2, lk2, subln_g, tq=ATTN_TQ, tk=ATTN_TK):
    b, s, _ = k3.shape
    nq, nk = s // tq, s // tk
    ratio = tk // tq
    assert tk % tq == 0 and nk % 2 == 0
    hw = 2 * ATTN_HEAD_DIM
    slopes = jnp.exp2(-8.0 * (jnp.arange(ATTN_HEADS, dtype=F32) + 1.0) / ATTN_HEADS) * LOG2E
    kn_tab = jnp.transpose(kn[:, 0, :N_MAPS].reshape(b, nk, N_MAPS), (0, 2, 1)).reshape(-1)
    kparts = tk // ATTN_QPART
    qn_tab = jnp.transpose(qn[:, :, :kparts].reshape(b, nk, N_MAPS, kparts), (0, 2, 1, 3)).reshape(-1)
    vec = lambda a: a.reshape(1, -1)
    return pl.pallas_call(
        functools.partial(_attn_kernel, tq, tk, nk, nq),
        grid=(b, ATTN_HEADS),
        in_specs=[
            pl.BlockSpec(memory_space=pltpu.SMEM),
            pl.BlockSpec(memory_space=pltpu.SMEM),
            pl.BlockSpec(memory_space=pltpu.SMEM),
            pl.BlockSpec((hw, s), lambda bi, hi: (hi, bi)),
            pl.BlockSpec((None, s, hw), lambda bi, hi: (bi, 0, hi)),
            pl.BlockSpec((ATTN_V_DIM, s), lambda bi, hi: (hi, bi)),
            _const_spec((1, ATTN_HEAD_DIM)), _const_spec((1, ATTN_HEAD_DIM)),
            _const_spec((1, ATTN_HEAD_DIM)), _const_spec((1, ATTN_HEAD_DIM)),
            _const_spec((ATTN_V_DIM, 1)),
        ],
        out_specs=pl.BlockSpec((None, s, ATTN_V_DIM), lambda bi, hi: (bi, 0, hi)),
        out_shape=jax.ShapeDtypeStruct((b, s, ATTN_HEADS * ATTN_V_DIM), F32),
        scratch_shapes=[pltpu.VMEM((2 + tk // tq, tk, tq), F32),
                        pltpu.VMEM((2, tk, tq), F32), pltpu.VMEM((2, tk, tq), F32),
                        pltpu.VMEM((2, ATTN_V_DIM, tq), F32), pltpu.VMEM((2, 1, tq), F32),
                        pltpu.SMEM((2,), jnp.int32), pltpu.SMEM((1,), F32)],
        compiler_params=pltpu.CompilerParams(dimension_semantics=("parallel", "parallel"),
                                             vmem_limit_bytes=VMEM_LIMIT),
        name="diff_attn",
    )(slopes, kn_tab, qn_tab, qT, k3, vT, vec(lq1), vec(lk1), vec(lq2), vec(lk2), subln_g.reshape(-1, 1))


def _out_proj_kernel(x_ref, eg_ref, eb_ref, ys_ref, o_ref, gate_ref, w_ref, g_ref, b_ref, out_ref):
    h = _layer_norm(x_ref[...], eg_ref[...], eb_ref[...])
    ya = o_ref[...] * _silu(gate_ref[...])
    mix = _dot(ys_ref[...].astype(BF16), w_ref[:D_SSM, :]) + _dot(ya.astype(BF16), w_ref[D_SSM:, :])
    out_ref[...] = _layer_norm(ALPHA * h + mix, g_ref[...], b_ref[...])


def _out_proj(x2d, ln_emb_g, ln_emb_b, y_ssm, o, gate, w_out, ln_g, ln_b, tm=512):
    t = x2d.shape[0]
    row = lambda n: pl.BlockSpec((tm, n), lambda i: (i, 0))
    vec = lambda a: a.reshape(1, -1)
    return pl.pallas_call(
        _out_proj_kernel,
        grid=(t // tm,),
        in_specs=[row(D_MODEL), _const_spec((1, D_MODEL)), _const_spec((1, D_MODEL)),
                  row(D_SSM), row(D_ATTN), row(D_ATTN),
                  _const_spec((D_SSM + D_ATTN, D_MODEL)), _const_spec((1, D_MODEL)), _const_spec((1, D_MODEL))],
        out_specs=row(D_MODEL),
        out_shape=jax.ShapeDtypeStruct((t, D_MODEL), F32),
        compiler_params=pltpu.CompilerParams(dimension_semantics=("parallel",), vmem_limit_bytes=VMEM_LIMIT),
        name="out_proj",
    )(x2d, vec(ln_emb_g), vec(ln_emb_b), y_ssm, o, gate, w_out.astype(BF16), vec(ln_g), vec(ln_b))


def kernel(x, ln_emb_g, ln_emb_b, w_in, conv_w, conv_b, A_log_fwd, A_log_bwd, dt_bias_fwd, dt_bias_bwd, D_skip,
           ssm_norm_g, lambda_q1, lambda_k1, lambda_q2, lambda_k2, subln_g, w_out, ln_g, ln_b):
    b, s, _ = x.shape
    t = b * s
    x2d = x.reshape(t, D_MODEL)
    z, xbc, dtT, qT, k, vT, gate, kn, qn = _ln_inproj(x2d, ln_emb_g, ln_emb_b, w_in[0])

    xs, bt, cm = _conv_silu(xbc.reshape(b, s, D_XBC), conv_w[0], conv_b[0])
    y_fwd = _ssd_pass(False, xs, bt, cm, dtT, A_log_fwd[0], dt_bias_fwd[0], (D_skip[0],))
    y_ssm = _ssd_pass(True, xs, bt, cm, dtT, A_log_bwd[0], dt_bias_bwd[0],
                      (y_fwd, z.reshape(b, s, D_SSM), ssm_norm_g[0]))

    o = _diff_attn(qT, k.reshape(b, s, D_ATTN), vT, kn, qn, lambda_q1[0], lambda_k1[0], lambda_q2[0],
                   lambda_k2[0], subln_g[0]).reshape(t, D_ATTN)

    out = _out_proj(x2d, ln_emb_g, ln_emb_b, y_ssm.reshape(t, D_SSM), o, gate, w_out[0], ln_g[0], ln_b[0])
    return out.reshape(b, s, D_MODEL)
```

```python
import functools
import math

import jax
import jax.numpy as jnp
from jax import lax
from jax.experimental import pallas as pl
from jax.experimental.pallas import tpu as pltpu

D_MODEL = 1024
D_SSM = 1024
SSM_HEAD_DIM = 64
SSM_HEADS = 16
SSM_GROUPS = 2
HEADS_PER_GROUP = SSM_HEADS // SSM_GROUPS
D_STATE = 128
D_CONV = 5
CHUNK = 128
D_XBC = D_SSM + 2 * SSM_GROUPS * D_STATE
D_ATTN = 1024
ATTN_HEADS = 8
ATTN_HEAD_DIM = 64
ATTN_V_DIM = 128
GROUP_WIDTH = D_SSM // SSM_GROUPS
DEPTH = 1
ALPHA = (2.0 * DEPTH) ** 0.25
LN_EPS = 1e-5
RMS_EPS = 1e-5
LAM_INIT = 0.8 - 0.6 * math.exp(-0.3 * 0)

LANES = 128
VMEM_LIMIT = 48 * 1024 * 1024

F32 = jnp.float32
BF16 = jnp.bfloat16


LOG2E = math.log2(math.e)


def _dot(a, b):
    return jnp.dot(a, b, preferred_element_type=F32)


def _dot_nt(a, b):
    return lax.dot_general(a, b, (((1,), (1,)), ((), ())), preferred_element_type=F32)


def _layer_norm(x, g, b):
    mu = jnp.mean(x, axis=-1, keepdims=True)
    xc = x - mu
    var = jnp.mean(xc * xc, axis=-1, keepdims=True)
    return xc * lax.rsqrt(var + LN_EPS) * g + b


def _silu(x):
    return x / (1.0 + jnp.exp(-x))


def _softplus(x):
    return jnp.maximum(x, 0.0) + jnp.log1p(jnp.exp(-jnp.abs(x)))


def _split3(x):
    hi = x.astype(BF16)
    r = x - hi.astype(F32)
    mid = r.astype(BF16)
    lo = (r - mid.astype(F32)).astype(BF16)
    return hi, mid, lo


def _dot_exact_rhs(m, x):
    hi, mid, lo = _split3(x)
    return _dot(m, hi) + _dot(m, mid) + _dot(m, lo)


def _dot_exact_lhs(x, m):
    hi, mid, lo = _split3(x)
    return _dot(hi, m) + _dot(mid, m) + _dot(lo, m)


def _const_spec(shape):
    nd = len(shape)
    return pl.BlockSpec(shape, lambda *_: (0,) * nd)


N_DT_TILES = 2 * SSM_GROUPS


ATTN_TQ = 256
ATTN_QPART = 256
ATTN_TK = 512
N_MAPS = 2 * ATTN_HEADS
NORM_SLACK = 1.01


def _ln_inproj_kernel(x_ref, g_ref, b_ref, wz_ref, wxbc_ref, wdtT_ref, wqT_ref, wk_ref, wvT_ref, wg_ref,
                      grp_ref, grpT_ref,
                      z_ref, xbc_ref, dtT_ref, qT_ref, k_ref, vT_ref, gate_ref, kn_ref, qn_ref):
    h = _layer_norm(x_ref[...], g_ref[...], b_ref[...]).astype(BF16)
    z_ref[...] = _dot(h, wz_ref[...])
    xbc_ref[...] = _dot(h, wxbc_ref[...])
    dtT_ref[...] = _dot_nt(wdtT_ref[...], h).reshape(dtT_ref.shape)
    qT = _dot_nt(wqT_ref[...], h) * (LOG2E * ATTN_HEAD_DIM ** -0.5)
    qT_ref[...] = qT.astype(BF16)
    k = _dot(h, wk_ref[...])
    k_ref[...] = k.astype(BF16)
    vT_ref[...] = _dot_nt(wvT_ref[...], h).astype(BF16)
    gate_ref[...] = _dot(h, wg_ref[...])
    k2 = _dot((k * k).astype(BF16), grp_ref[...])
    kn_ref[...] = jnp.sqrt(jnp.max(k2, axis=0, keepdims=True)) * NORM_SLACK
    q2 = _dot(grpT_ref[...], (qT * qT).astype(BF16))[:N_MAPS, :]
    lane = lax.broadcasted_iota(jnp.int32, (N_MAPS, LANES), 1)
    qn = jnp.zeros((N_MAPS, LANES), F32)
    for part in range(q2.shape[1] // ATTN_QPART):
        pm = jnp.max(q2[:, part * ATTN_QPART:(part + 1) * ATTN_QPART], axis=1, keepdims=True)
        qn = jnp.where(lane == part, jnp.sqrt(pm) * NORM_SLACK, qn)
    qn_ref[...] = qn


def _ln_inproj(x2d, ln_g, ln_b, w_in, tm=ATTN_TK):
    t = x2d.shape[0]
    grp = (jnp.arange(D_ATTN)[:, None] // ATTN_HEAD_DIM == jnp.arange(LANES)[None, :]).astype(BF16)
    offs = [0, D_SSM, D_SSM + D_XBC, D_SSM + D_XBC + 2 * SSM_HEADS]
    wz = w_in[:, offs[0]:offs[1]].astype(BF16)
    wxbc = w_in[:, offs[1]:offs[2]].astype(BF16)
    wdt = w_in[:, offs[2]:offs[3]].astype(BF16)
    o = offs[3]
    wq, wk, wv, wg = (w_in[:, o + i * D_ATTN:o + (i + 1) * D_ATTN].astype(BF16) for i in range(4))
    row = lambda n: pl.BlockSpec((tm, n), lambda i: (i, 0))
    col = lambda n: pl.BlockSpec((n, tm), lambda i: (0, i))
    weights = [wz, wxbc, wdt.T, wq.T, wk, wv.T, wg, grp, grp.T]
    nt = t // tm
    return pl.pallas_call(
        _ln_inproj_kernel,
        grid=(nt,),
        in_specs=[row(D_MODEL), _const_spec((1, D_MODEL)), _const_spec((1, D_MODEL))]
                 + [pl.BlockSpec(w.shape, lambda i: (0, 0), pipeline_mode=pl.Buffered(1)) for w in weights],
        out_specs=[row(D_SSM), row(D_XBC),
                   pl.BlockSpec((N_DT_TILES, HEADS_PER_GROUP, tm), lambda i: (0, 0, i)),
                   col(D_ATTN), row(D_ATTN), col(D_ATTN), row(D_ATTN),
                   pl.BlockSpec((None, 1, LANES), lambda i: (i, 0, 0)),
                   pl.BlockSpec((None, N_MAPS, LANES), lambda i: (i, 0, 0))],
        out_shape=[jax.ShapeDtypeStruct((t, D_SSM), F32), jax.ShapeDtypeStruct((t, D_XBC), F32),
                   jax.ShapeDtypeStruct((N_DT_TILES, HEADS_PER_GROUP, t), F32),
                   jax.ShapeDtypeStruct((D_ATTN, t), BF16), jax.ShapeDtypeStruct((t, D_ATTN), BF16),
                   jax.ShapeDtypeStruct((D_ATTN, t), BF16), jax.ShapeDtypeStruct((t, D_ATTN), F32),
                   jax.ShapeDtypeStruct((nt, 1, LANES), F32), jax.ShapeDtypeStruct((nt, N_MAPS, LANES), F32)],
        compiler_params=pltpu.CompilerParams(dimension_semantics=("parallel",), vmem_limit_bytes=VMEM_LIMIT),
        name="ln_inproj",
    )(x2d, ln_g.reshape(1, -1), ln_b.reshape(1, -1), *weights)


HALO = 8


CONV_ROWS = 256


def _conv_silu_kernel(prev_ref, cur_ref, next_ref, w_ref, b_ref, xs_ref, bt_ref, c_ref, ext_ref):
    i = pl.program_id(1)
    n = pl.num_programs(1)
    tc = cur_ref.shape[0]
    ext_ref[0:HALO, :] = jnp.where(i == 0, 0.0, prev_ref[...])
    ext_ref[HALO:HALO + tc, :] = cur_ref[...]
    ext_ref[HALO + tc:, :] = jnp.where(i == n - 1, 0.0, next_ref[...])
    for cb in range(D_XBC // LANES):
        cols = slice(cb * LANES, (cb + 1) * LANES)
        for r0 in range(0, tc, CONV_ROWS):
            acc = b_ref[:, cols]
            for kk in range(D_CONV):
                acc = acc + ext_ref[pl.ds(r0 + HALO - D_CONV // 2 + kk, CONV_ROWS), cols] * w_ref[kk:kk + 1, cols]
            act = _silu(acc)
            rows = slice(r0, r0 + CONV_ROWS)
            if cb < D_SSM // LANES:
                xs_ref[rows, cols] = act
            elif cb < D_SSM // LANES + SSM_GROUPS:
                bt_ref[cb - D_SSM // LANES, :, rows] = act.T.astype(BF16)
            else:
                c_ref[rows, (cb - D_SSM // LANES - SSM_GROUPS) * LANES:(cb - D_SSM // LANES - SSM_GROUPS + 1) * LANES] = (
                    act.astype(BF16))


def _conv_silu(xbc, conv_w, conv_b, tc=512):
    b, s, _ = xbc.shape
    nb = s // tc
    hb = tc // HALO
    return pl.pallas_call(
        _conv_silu_kernel,
        grid=(b, nb),
        in_specs=[
            pl.BlockSpec((None, HALO, D_XBC), lambda bi, i: (bi, jnp.maximum(i * hb - 1, 0), 0)),
            pl.BlockSpec((None, tc, D_XBC), lambda bi, i: (bi, i, 0)),
            pl.BlockSpec((None, HALO, D_XBC), lambda bi, i: (bi, jnp.minimum((i + 1) * hb, s // HALO - 1), 0)),
            _const_spec((D_CONV, D_XBC)),
            _const_spec((1, D_XBC)),
        ],
        out_specs=[
            pl.BlockSpec((None, tc, D_SSM), lambda bi, i: (bi, i, 0)),
            pl.BlockSpec((None, SSM_GROUPS, D_STATE, tc), lambda bi, i: (bi, 0, 0, i)),
            pl.BlockSpec((None, tc, SSM_GROUPS * D_STATE), lambda bi, i: (bi, i, 0)),
        ],
        out_shape=[
            jax.ShapeDtypeStruct((b, s, D_SSM), F32),
            jax.ShapeDtypeStruct((b, SSM_GROUPS, D_STATE, s), BF16),
            jax.ShapeDtypeStruct((b, s, SSM_GROUPS * D_STATE), BF16),
        ],
        scratch_shapes=[pltpu.VMEM((tc + 2 * HALO, D_XBC), F32)],
        compiler_params=pltpu.CompilerParams(dimension_semantics=("parallel", "parallel"),
                                             vmem_limit_bytes=VMEM_LIMIT),
        name="conv_silu",
    )(xbc, xbc, xbc, conv_w, conv_b.reshape(1, -1))


SSD_CHUNKS_IN_FLIGHT = 2


def _ssd_bodies(rev, xs, bts, cms, dtTs, alogs, biases, groups, states):
    L = CHUNK
    r = lax.broadcasted_iota(jnp.int32, (L, L), 0)
    c = lax.broadcasted_iota(jnp.int32, (L, L), 1)
    keep = (c >= r) if rev else (c <= r)
    cum_r = ((r >= c) if rev else (r <= c)).astype(BF16)
    hg = HEADS_PER_GROUP
    eh = lax.broadcasted_iota(jnp.int32, (6 * hg, 2 * GROUP_WIDTH), 0)
    el = lax.broadcasted_iota(jnp.int32, (6 * hg, 2 * GROUP_WIDTH), 1)
    spread = ((el // GROUP_WIDTH == eh // (3 * hg)) & ((el % GROUP_WIDTH) // SSM_HEAD_DIM == eh % hg)).astype(BF16)
    lane = lax.broadcasted_iota(jnp.int32, (L, LANES), 1)
    edge = 0 if rev else L - 1
    gs = range(len(xs))

    dt_t = [_softplus(dtTs[g] + biases[g]) for g in gs]
    acs_t = [_dot_exact_lhs(dt_t[g] * (-jnp.exp(alogs[g])), cum_r) for g in gs]
    pieces_t = [jnp.concatenate([p.astype(F32) for p in _split3(dt_t[g]) + _split3(acs_t[g])], axis=0) for g in gs]
    both_x = [_dot(pieces_t[g].T.astype(BF16), spread) for g in gs]
    acs = [acs_t[g].T for g in gs]
    cb = [_dot(cms[g], bts[g]) for g in gs]
    dt_x = [both_x[g][:, :GROUP_WIDTH] for g in gs]
    acs_x = [both_x[g][:, GROUP_WIDTH:] for g in gs]
    tot_x = [acs_x[g][edge:edge + 1, :] for g in gs]
    xdt = [xs[g] * dt_x[g] for g in gs]

    y_diag = []
    for g in gs:
        tiles = []
        for t in range(GROUP_WIDTH // LANES):
            xt = xdt[g][:, t * LANES:(t + 1) * LANES]
            gmats, xparts = [], []
            for half in range(LANES // SSM_HEAD_DIM):
                j = t * (LANES // SSM_HEAD_DIM) + half
                seg = acs[g][:, j:j + 1] - acs_t[g][j:j + 1, :]
                gmats.append((cb[g] * jnp.exp(jnp.where(keep, seg, -jnp.inf))).astype(BF16))
                in_head = (lane >= half * SSM_HEAD_DIM) & (lane < (half + 1) * SSM_HEAD_DIM)
                xparts.append(jnp.where(in_head, xt, 0.0).astype(BF16))
            tiles.append(_dot(jnp.concatenate(gmats, axis=1), jnp.concatenate(xparts, axis=0)))
        y_diag.append(jnp.concatenate(tiles, axis=1))

    states = list(states)
    ys = []
    for g in gs:
        grp = groups[g]
        y_off = _dot(cms[g], states[grp].astype(BF16)) * jnp.exp(acs_x[g])
        ys.append(y_off + y_diag[g])
        w = (xdt[g] * jnp.exp(tot_x[g] - acs_x[g])).astype(BF16)
        states[grp] = jnp.exp(tot_x[g]) * states[grp] + _dot(bts[g], w)
    return ys, states


def _ssd_kernel(rev, nchunk, *refs):
    if rev:
        (xs_ref, bt_ref, c_ref, dtT_ref, alc_ref, bc_ref, yf_ref, z_ref, ng_ref, y_ref, state_ref) = refs
    else:
        (xs_ref, bt_ref, c_ref, dtT_ref, alc_ref, bc_ref, dx_ref, y_ref, state_ref) = refs

    @pl.when(pl.program_id(1) == 0)
    def _():
        state_ref[...] = jnp.zeros_like(state_ref)

    gw = GROUP_WIDTH
    gs = range(SSM_GROUPS)
    cols = [slice(g * gw, (g + 1) * gw) for g in gs]
    order = list(range(nchunk - 1, -1, -1) if rev else range(nchunk))
    states = [state_ref[g] for g in gs]
    for c0 in range(0, nchunk, SSD_CHUNKS_IN_FLIGHT):
        bodies = [(slice(ci * CHUNK, (ci + 1) * CHUNK), g) for ci in order[c0:c0 + SSD_CHUNKS_IN_FLIGHT] for g in gs]
        x = [xs_ref[rows, cols[g]] for rows, g in bodies]
        ys, states = _ssd_bodies(rev, x, [bt_ref[g, :, rows] for rows, g in bodies],
                                 [c_ref[rows, g * D_STATE:(g + 1) * D_STATE] for rows, g in bodies],
                                 [dtT_ref[g, :, rows] for rows, g in bodies],
                                 [alc_ref[g] for _, g in bodies], [bc_ref[g] for _, g in bodies],
                                 [g for _, g in bodies], states)
        for (rows, g), xb, y in zip(bodies, x, ys):
            if rev:
                yy = (yf_ref[rows, cols[g]] + y) * _silu(z_ref[rows, cols[g]])
                ms = jnp.mean(yy * yy, axis=-1, keepdims=True)
                y_ref[rows, cols[g]] = yy * lax.rsqrt(ms + RMS_EPS) * ng_ref[g]
            else:
                y_ref[rows, cols[g]] = y + dx_ref[g] * xb
    for g in gs:
        state_ref[g] = states[g]


def _ssd_pass(rev, xs, bt, cm, dtT, a_log, dt_bias, extra, nchunk=4):
    b, s, _ = xs.shape
    tcs = nchunk * CHUNK
    nblk = s // tcs
    blk = (lambda i: nblk - 1 - i) if rev else (lambda i: i)
    dirn = 1 if rev else 0
    gw, hg, ng = GROUP_WIDTH, HEADS_PER_GROUP, SSM_GROUPS
    seq_spec = lambda w: pl.BlockSpec((None, tcs, w), lambda bi, i: (bi, blk(i), 0))
    in_specs = [
        seq_spec(D_SSM),
        pl.BlockSpec((None, ng, D_STATE, tcs), lambda bi, i: (bi, 0, 0, blk(i))),
        seq_spec(ng * D_STATE),
        pl.BlockSpec((ng, hg, tcs), lambda bi, i: (dirn, 0, bi * nblk + blk(i))),
        _const_spec((ng, hg, 1)), _const_spec((ng, hg, 1)),
    ]
    args = [xs, bt, cm, dtT, a_log.reshape(ng, hg, 1), dt_bias.reshape(ng, hg, 1)]
    if rev:
        y_fwd, z, norm_g = extra
        in_specs += [seq_spec(D_SSM), seq_spec(D_SSM), _const_spec((ng, 1, gw))]
        args += [y_fwd, z, norm_g.reshape(ng, 1, gw)]
    else:
        (d_skip,) = extra
        in_specs += [_const_spec((ng, 1, gw))]
        args += [jnp.repeat(d_skip, SSM_HEAD_DIM).reshape(ng, 1, gw)]
    return pl.pallas_call(
        functools.partial(_ssd_kernel, rev, nchunk),
        grid=(b, nblk),
        in_specs=in_specs,
        out_specs=seq_spec(D_SSM),
        out_shape=jax.ShapeDtypeStruct((b, s, D_SSM), F32),
        scratch_shapes=[pltpu.VMEM((ng, D_STATE, gw), F32)],
        compiler_params=pltpu.CompilerParams(dimension_semantics=("parallel", "arbitrary"),
                                             vmem_limit_bytes=VMEM_LIMIT),
        name="ssd_bwd" if rev else "ssd_fwd",
    )(*args)


NEG_BIG = -1e30


SKIP_LOG2 = 80.0
FIXED_SHIFT_MAX_LOG2 = 60.0
PV_DEPTH = 256


def _attn_plan(tq, tk, nk, nq, bi, hi, qi, slope, kn_ref, qn_ref, plan_i_ref, plan_f_ref):
    i0 = qi * tq
    jd = qi // (tk // tq)
    parts = tq // ATTN_QPART
    qns, bases = [], []
    for r in range(2):
        base = (bi * ATTN_HEADS + hi) * 2 + r
        qn = qn_ref[base * (nq * parts) + qi * parts]
        for part in range(1, parts):
            qn = jnp.maximum(qn, qn_ref[base * (nq * parts) + qi * parts + part])
        qns.append(qn)
        bases.append(base * nk)
    jlo, jhi = jd, jd
    kmax = [kn_ref[bases[0]], kn_ref[bases[1]]]
    for jt in range(nk):
        dist = jnp.maximum(jnp.maximum(i0 - (jt * tk + tk - 1), jt * tk - (i0 + tq - 1)), 0).astype(F32)
        need = None
        for r in range(2):
            kn = kn_ref[bases[r] + jt]
            kmax[r] = jnp.maximum(kmax[r], kn)
            reach = SKIP_LOG2 + qns[r] * (kn + kn_ref[bases[r] + jd])
            need_r = slope * dist <= reach
            need = need_r if need is None else (need | need_r)
        jlo = jnp.where(need, jnp.minimum(jlo, jt), jlo)
        jhi = jnp.where(need, jnp.maximum(jhi, jt), jhi)
    plan_i_ref[0] = jlo
    plan_i_ref[1] = jhi
    plan_f_ref[0] = jnp.maximum(qns[0] * kmax[0], qns[1] * kmax[1])


def _attn_q_tile(tq, tk, nk, nq, bi, hi, qi, slope, kn_ref, qn_ref, qT_ref, k_ref, vT_ref, lq1_ref, lk1_ref, lq2_ref,
                 lk2_ref, sg_ref, o_ref, e_ref, s_ref, s1_ref, acc_ref, l_ref, plan_i_ref, plan_f_ref):
    ratio = tk // tq
    i0 = qi * tq
    q0 = pl.multiple_of(i0, tq)
    jd = qi // ratio
    dsel = qi % ratio
    jlo, jhi, score_bound = plan_i_ref[0], plan_i_ref[1], plan_f_ref[0]

    def plan_next():
        _attn_plan(tq, tk, nk, nq, bi, hi, jnp.minimum(qi + 1, nq - 1), slope, kn_ref, qn_ref, plan_i_ref, plan_f_ref)

    row = lax.broadcasted_iota(jnp.int32, (2 * ATTN_HEAD_DIM, tq), 0)
    qf = qT_ref[:, pl.ds(q0, tq)].astype(F32)
    rhs = (jnp.where(row < ATTN_HEAD_DIM, qf, 0.0).astype(BF16),
           jnp.where(row >= ATTN_HEAD_DIM, qf, 0.0).astype(BF16))
    def finish_previous():
        _attn_epilogue(tq, jnp.maximum(qi - 1, 0), lq1_ref, lk1_ref, lq2_ref, lk2_ref, sg_ref, o_ref, acc_ref, l_ref)
        acc_ref[...] = jnp.zeros_like(acc_ref)

    def tile_bias(j):
        off = (i0 - j * tk).astype(F32)
        before, after = j < jd, j > jd
        eidx = jnp.where(before, 0, jnp.where(after, 1, 2 + dsel))
        cj = jnp.where(before, -slope * (off - float(tk - 1)), jnp.where(after, slope * (off + float(tq - 1)), 0.0))
        return eidx, cj

    @pl.when(score_bound <= FIXED_SHIFT_MAX_LOG2)
    def _():
        odd = (jhi - jlo + 1) % 2
        room_above = (jhi < nk - 1).astype(jnp.int32)
        jl = jlo - odd * (1 - room_above)
        jh = jhi + odd * room_above

        def stage_a(j, buf):
            kt = k_ref[pl.ds(pl.multiple_of(j * tk, tk), tk), :]
            for r in range(2):
                buf[r] = _dot(kt, rhs[r])

        def stage_b(j, buf, l):
            eidx, cj = tile_bias(j)
            half = jnp.exp2(jnp.full((1, tq), 0.5 * cj, F32))
            vt = vT_ref[:, pl.ds(pl.multiple_of(j * tk, tk), tk)]
            l_new = []
            for r in range(2):
                psum, pv = None, None
                for k0 in range(0, tk, PV_DEPTH):
                    p = jnp.exp2(buf[r, k0:k0 + PV_DEPTH, :] + e_ref[eidx, k0:k0 + PV_DEPTH, :])
                    ps = jnp.sum(p, axis=0, keepdims=True)
                    d = _dot(vt[:, k0:k0 + PV_DEPTH], p.astype(BF16))
                    psum, pv = (ps, d) if pv is None else (psum + ps, pv + d)
                l_new.append(l[r] + psum * half * half)
                acc_ref[r] += pv * half * half
            return tuple(l_new)

        def pair(j, l):
            stage_a(j + 1, s1_ref)
            l = stage_b(j, s_ref, l)
            stage_a(jnp.minimum(j + 2, jh), s_ref)
            return stage_b(j + 1, s1_ref, l)

        def body(t, l):
            j = jl + 4 * t
            return pair(j + 2, pair(j, l))

        stage_a(jl, s_ref)
        finish_previous()
        plan_next()
        zero = jnp.zeros((1, tq), F32)
        npairs = (jh - jl + 1) // 2
        l = lax.fori_loop(0, npairs // 2, body, (zero, zero))
        l_ref[0] = l[0]
        l_ref[1] = l[1]

        @pl.when(npairs % 2 == 1)
        def _():
            l = pair(jh - 1, (l_ref[0], l_ref[1]))
            l_ref[0] = l[0]
            l_ref[1] = l[1]

    def stage_a(j, m_old):
        eidx, cj = tile_bias(j)
        kt = k_ref[pl.ds(pl.multiple_of(j * tk, tk), tk), :]
        m_new, shift, alpha = [], [], []
        for r in range(2):
            s = _dot(kt, rhs[r]) + e_ref[eidx]
            s_ref[r] = s
            mr = jnp.maximum(m_old[r], jnp.max(s, axis=0, keepdims=True) + cj)
            m_new.append(mr)
            shift.append(mr - cj)
            alpha.append(jnp.exp2(m_old[r] - mr))
        return tuple(m_new), tuple(shift), tuple(alpha)

    def stage_b(j, shift, alpha, l_old):
        vt = vT_ref[:, pl.ds(pl.multiple_of(j * tk, tk), tk)]
        l_new = []
        for r in range(2):
            p = jnp.exp2(s_ref[r] - shift[r])
            l_new.append(alpha[r] * l_old[r] + jnp.sum(p, axis=0, keepdims=True))
            acc_ref[r] = alpha[r] * acc_ref[r] + _dot(vt, p.astype(BF16))
        return tuple(l_new)

    def body(t, carry):
        m, shift, alpha, l = carry
        j = jlo + t
        l = stage_b(jnp.maximum(j - 1, jlo), shift, alpha, l)
        m, shift, alpha = stage_a(jnp.minimum(j, jhi), m)
        return m, shift, alpha, l

    @pl.when(score_bound > FIXED_SHIFT_MAX_LOG2)
    def _():
        neg = jnp.full((1, tq), NEG_BIG, F32)
        big = jnp.full((1, tq), -NEG_BIG, F32)
        one = jnp.ones((1, tq), F32)
        zero = jnp.zeros((1, tq), F32)
        finish_previous()
        plan_next()
        init = ((neg, neg), (big, big), (one, one), (zero, zero))
        _, _, _, l = lax.fori_loop(0, jhi - jlo + 2, body, init)
        l_ref[0] = l[0]
        l_ref[1] = l[1]


def _attn_epilogue(tq, qi, lq1_ref, lk1_ref, lq2_ref, lk2_ref, sg_ref, o_ref, acc_ref, l_ref):
    lam = (jnp.exp(jnp.sum(lq1_ref[...] * lk1_ref[...], axis=-1, keepdims=True))
           - jnp.exp(jnp.sum(lq2_ref[...] * lk2_ref[...], axis=-1, keepdims=True)) + LAM_INIT)
    o = acc_ref[0] / l_ref[0] - lam * (acc_ref[1] / l_ref[1])
    ms = jnp.mean(o * o, axis=0, keepdims=True)
    q0 = pl.multiple_of(qi * tq, tq)
    o_ref[pl.ds(q0, tq), :] = (o * lax.rsqrt(ms + RMS_EPS) * sg_ref[...] * (1.0 - LAM_INIT)).T


def _attn_kernel(tq, tk, nk, nq, slopes_ref, kn_ref, qn_ref, qT_ref, k_ref, vT_ref, lq1_ref, lk1_ref, lq2_ref,
                 lk2_ref, sg_ref, o_ref, e_ref, s_ref, s1_ref, acc_ref, l_ref, plan_i_ref, plan_f_ref):
    bi, hi = pl.program_id(0), pl.program_id(1)
    slope = slopes_ref[hi]
    ii = lax.broadcasted_iota(jnp.int32, (tk, tq), 1)
    jj = lax.broadcasted_iota(jnp.int32, (tk, tq), 0)
    rel = (ii - jj).astype(F32)
    e_ref[0] = -slope * (rel + float(tk - 1))
    e_ref[1] = slope * (rel - float(tq - 1))
    for d in range(tk // tq):
        e_ref[2 + d] = -slope * jnp.abs(rel + float(d * tq))
    s_ref[...] = jnp.zeros_like(s_ref)
    acc_ref[...] = jnp.zeros_like(acc_ref)
    l_ref[...] = jnp.ones_like(l_ref)
    _attn_plan(tq, tk, nk, nq, bi, hi, 0, slope, kn_ref, qn_ref, plan_i_ref, plan_f_ref)

    def q_tile(qi, carry):
        _attn_q_tile(tq, tk, nk, nq, bi, hi, qi, slope, kn_ref, qn_ref, qT_ref, k_ref, vT_ref, lq1_ref, lk1_ref,
                     lq2_ref, lk2_ref, sg_ref, o_ref, e_ref, s_ref, s1_ref, acc_ref, l_ref, plan_i_ref, plan_f_ref)
        return carry

    lax.fori_loop(0, nq, q_tile, 0)
    _attn_epilogue(tq, nq - 1, lq1_ref, lk1_ref, lq2_ref, lk2_ref, sg_ref, o_ref, acc_ref, l_ref)


def _diff_attn(qT, k3, vT, kn, qn, lq1, lk1, lq2, lk2, subln_g, tq=ATTN_TQ, tk=ATTN_TK):
    b, s, _ = k3.shape
    nq, nk = s // tq, s // tk
    ratio = tk // tq
    assert tk % tq == 0 and nk % 2 == 0
    hw = 2 * ATTN_HEAD_DIM
    slopes = jnp.exp2(-8.0 * (jnp.arange(ATTN_HEADS, dtype=F32) + 1.0) / ATTN_HEADS) * LOG2E
    kn_tab = jnp.transpose(kn[:, 0, :N_MAPS].reshape(b, nk, N_MAPS), (0, 2, 1)).reshape(-1)
    kparts = tk // ATTN_QPART
    qn_tab = jnp.transpose(qn[:, :, :kparts].reshape(b, nk, N_MAPS, kparts), (0, 2, 1, 3)).reshape(-1)
    vec = lambda a: a.reshape(1, -1)
    return pl.pallas_call(
        functools.partial(_attn_kernel, tq, tk, nk, nq),
        grid=(b, ATTN_HEADS),
        in_specs=[
            pl.BlockSpec(memory_space=pltpu.SMEM),
            pl.BlockSpec(memory_space=pltpu.SMEM),
            pl.BlockSpec(memory_space=pltpu.SMEM),
            pl.BlockSpec((hw, s), lambda bi, hi: (hi, bi)),
            pl.BlockSpec((None, s, hw), lambda bi, hi: (bi, 0, hi)),
            pl.BlockSpec((ATTN_V_DIM, s), lambda bi, hi: (hi, bi)),
            _const_spec((1, ATTN_HEAD_DIM)), _const_spec((1, ATTN_HEAD_DIM)),
            _const_spec((1, ATTN_HEAD_DIM)), _const_spec((1, ATTN_HEAD_DIM)),
            _const_spec((ATTN_V_DIM, 1)),
        ],
        out_specs=pl.BlockSpec((None, s, ATTN_V_DIM), lambda bi, hi: (bi, 0, hi)),
        out_shape=jax.ShapeDtypeStruct((b, s, ATTN_HEADS * ATTN_V_DIM), F32),
        scratch_shapes=[pltpu.VMEM((2 + tk // tq, tk, tq), F32),
                        pltpu.VMEM((2, tk, tq), F32), pltpu.VMEM((2, tk, tq), F32),
                        pltpu.VMEM((2, ATTN_V_DIM, tq), F32), pltpu.VMEM((2, 1, tq), F32),
                        pltpu.SMEM((2,), jnp.int32), pltpu.SMEM((1,), F32)],
        compiler_params=pltpu.CompilerParams(dimension_semantics=("parallel", "parallel"),
                                             vmem_limit_bytes=VMEM_LIMIT),
        name="diff_attn",
    )(slopes, kn_tab, qn_tab, qT, k3, vT, vec(lq1), vec(lk1), vec(lq2), vec(lk2), subln_g.reshape(-1, 1))


def _out_proj_kernel(x_ref, eg_ref, eb_ref, ys_ref, o_ref, gate_ref, w_ref, g_ref, b_ref, out_ref):
    h = _layer_norm(x_ref[...], eg_ref[...], eb_ref[...])
    ya = o_ref[...] * _silu(gate_ref[...])
    mix = _dot(ys_ref[...].astype(BF16), w_ref[:D_SSM, :]) + _dot(ya.astype(BF16), w_ref[D_SSM:, :])
    out_ref[...] = _layer_norm(ALPHA * h + mix, g_ref[...], b_ref[...])


def _out_proj(x2d, ln_emb_g, ln_emb_b, y_ssm, o, gate, w_out, ln_g, ln_b, tm=512):
    t = x2d.shape[0]
    row = lambda n: pl.BlockSpec((tm, n), lambda i: (i, 0))
    vec = lambda a: a.reshape(1, -1)
    return pl.pallas_call(
        _out_proj_kernel,
        grid=(t // tm,),
        in_specs=[row(D_MODEL), _const_spec((1, D_MODEL)), _const_spec((1, D_MODEL)),
                  row(D_SSM), row(D_ATTN), row(D_ATTN),
                  _const_spec((D_SSM + D_ATTN, D_MODEL)), _const_spec((1, D_MODEL)), _const_spec((1, D_MODEL))],
        out_specs=row(D_MODEL),
        out_shape=jax.ShapeDtypeStruct((t, D_MODEL), F32),
        compiler_params=pltpu.CompilerParams(dimension_semantics=("parallel",), vmem_limit_bytes=VMEM_LIMIT),
        name="out_proj",
    )(x2d, vec(ln_emb_g), vec(ln_emb_b), y_ssm, o, gate, w_out.astype(BF16), vec(ln_g), vec(ln_b))


def kernel(x, ln_emb_g, ln_emb_b, w_in, conv_w, conv_b, A_log_fwd, A_log_bwd, dt_bias_fwd, dt_bias_bwd, D_skip,
           ssm_norm_g, lambda_q1, lambda_k1, lambda_q2, lambda_k2, subln_g, w_out, ln_g, ln_b):
    b, s, _ = x.shape
    t = b * s
    x2d = x.reshape(t, D_MODEL)
    z, xbc, dtT, qT, k, vT, gate, kn, qn = _ln_inproj(x2d, ln_emb_g, ln_emb_b, w_in[0])

    xs, bt, cm = _conv_silu(xbc.reshape(b, s, D_XBC), conv_w[0], conv_b[0])
    y_fwd = _ssd_pass(False, xs, bt, cm, dtT, A_log_fwd[0], dt_bias_fwd[0], (D_skip[0],))
    y_ssm = _ssd_pass(True, xs, bt, cm, dtT, A_log_bwd[0], dt_bias_bwd[0],
                      (y_fwd, z.reshape(b, s, D_SSM), ssm_norm_g[0]))

    o = _diff_attn(qT, k.reshape(b, s, D_ATTN), vT, kn, qn, lambda_q1[0], lambda_k1[0], lambda_q2[0],
                   lambda_k2[0], subln_g[0]).reshape(t, D_ATTN)

    out = _out_proj(x2d, ln_emb_g, ln_emb_b, y_ssm.reshape(t, D_SSM), o, gate, w_out[0], ln_g[0], ln_b[0])
    return out.reshape(b, s, D_MODEL)
```

```python
import functools
import math

import jax
import jax.numpy as jnp
from jax import lax
from jax.experimental import pallas as pl
from jax.experimental.pallas import tpu as pltpu

D_MODEL = 1024
D_SSM = 1024
SSM_HEAD_DIM = 64
SSM_HEADS = 16
SSM_GROUPS = 2
HEADS_PER_GROUP = SSM_HEADS // SSM_GROUPS
D_STATE = 128
D_CONV = 5
CHUNK = 128
D_XBC = D_SSM + 2 * SSM_GROUPS * D_STATE
D_ATTN = 1024
ATTN_HEADS = 8
ATTN_HEAD_DIM = 64
ATTN_V_DIM = 128
GROUP_WIDTH = D_SSM // SSM_GROUPS
DEPTH = 1
ALPHA = (2.0 * DEPTH) ** 0.25
LN_EPS = 1e-5
RMS_EPS = 1e-5
LAM_INIT = 0.8 - 0.6 * math.exp(-0.3 * 0)

LANES = 128
VMEM_LIMIT = 48 * 1024 * 1024

F32 = jnp.float32
BF16 = jnp.bfloat16


LOG2E = math.log2(math.e)


def _dot(a, b):
    return jnp.dot(a, b, preferred_element_type=F32)


def _dot_nt(a, b):
    return lax.dot_general(a, b, (((1,), (1,)), ((), ())), preferred_element_type=F32)


def _layer_norm(x, g, b):
    mu = jnp.mean(x, axis=-1, keepdims=True)
    xc = x - mu
    var = jnp.mean(xc * xc, axis=-1, keepdims=True)
    return xc * lax.rsqrt(var + LN_EPS) * g + b


def _silu(x):
    return x / (1.0 + jnp.exp(-x))


def _softplus(x):
    return jnp.maximum(x, 0.0) + jnp.log1p(jnp.exp(-jnp.abs(x)))


def _split3(x):
    hi = x.astype(BF16)
    r = x - hi.astype(F32)
    mid = r.astype(BF16)
    lo = (r - mid.astype(F32)).astype(BF16)
    return hi, mid, lo


def _dot_exact_rhs(m, x):
    hi, mid, lo = _split3(x)
    return _dot(m, hi) + _dot(m, mid) + _dot(m, lo)


def _dot_exact_lhs(x, m):
    hi, mid, lo = _split3(x)
    return _dot(hi, m) + _dot(mid, m) + _dot(lo, m)


def _const_spec(shape):
    nd = len(shape)
    return pl.BlockSpec(shape, lambda *_: (0,) * nd)


N_DT_TILES = 2 * SSM_GROUPS


ATTN_TQ = 256
ATTN_QPART = 256
ATTN_TK = 512
N_MAPS = 2 * ATTN_HEADS
NORM_SLACK = 1.01


def _ln_inproj_kernel(x_ref, g_ref, b_ref, wz_ref, wxbc_ref, wdtT_ref, wqT_ref, wk_ref, wvT_ref, wg_ref,
                      grp_ref, grpT_ref,
                      z_ref, xbc_ref, dtT_ref, qT_ref, k_ref, vT_ref, gate_ref, kn_ref, qn_ref):
    h = _layer_norm(x_ref[...], g_ref[...], b_ref[...]).astype(BF16)
    z_ref[...] = _dot(h, wz_ref[...])
    xbc_ref[...] = _dot(h, wxbc_ref[...])
    dtT_ref[...] = _dot_nt(wdtT_ref[...], h).reshape(dtT_ref.shape)
    qT = _dot_nt(wqT_ref[...], h) * (LOG2E * ATTN_HEAD_DIM ** -0.5)
    qT_ref[...] = qT.astype(BF16)
    k = _dot(h, wk_ref[...])
    k_ref[...] = k.astype(BF16)
    vT_ref[...] = _dot_nt(wvT_ref[...], h).astype(BF16)
    gate_ref[...] = _dot(h, wg_ref[...])
    k2 = _dot((k * k).astype(BF16), grp_ref[...])
    kn_ref[...] = jnp.sqrt(jnp.max(k2, axis=0, keepdims=True)) * NORM_SLACK
    q2 = _dot(grpT_ref[...], (qT * qT).astype(BF16))[:N_MAPS, :]
    lane = lax.broadcasted_iota(jnp.int32, (N_MAPS, LANES), 1)
    qn = jnp.zeros((N_MAPS, LANES), F32)
    for part in range(q2.shape[1] // ATTN_QPART):
        pm = jnp.max(q2[:, part * ATTN_QPART:(part + 1) * ATTN_QPART], axis=1, keepdims=True)
        qn = jnp.where(lane == part, jnp.sqrt(pm) * NORM_SLACK, qn)
    qn_ref[...] = qn


def _ln_inproj(x2d, ln_g, ln_b, w_in, tm=ATTN_TK):
    t = x2d.shape[0]
    grp = (jnp.arange(D_ATTN)[:, None] // ATTN_HEAD_DIM == jnp.arange(LANES)[None, :]).astype(BF16)
    offs = [0, D_SSM, D_SSM + D_XBC, D_SSM + D_XBC + 2 * SSM_HEADS]
    wz = w_in[:, offs[0]:offs[1]].astype(BF16)
    wxbc = w_in[:, offs[1]:offs[2]].astype(BF16)
    wdt = w_in[:, offs[2]:offs[3]].astype(BF16)
    o = offs[3]
    wq, wk, wv, wg = (w_in[:, o + i * D_ATTN:o + (i + 1) * D_ATTN].astype(BF16) for i in range(4))
    row = lambda n: pl.BlockSpec((tm, n), lambda i: (i, 0))
    col = lambda n: pl.BlockSpec((n, tm), lambda i: (0, i))
    weights = [wz, wxbc, wdt.T, wq.T, wk, wv.T, wg, grp, grp.T]
    nt = t // tm
    return pl.pallas_call(
        _ln_inproj_kernel,
        grid=(nt,),
        in_specs=[row(D_MODEL), _const_spec((1, D_MODEL)), _const_spec((1, D_MODEL))]
                 + [pl.BlockSpec(w.shape, lambda i: (0, 0), pipeline_mode=pl.Buffered(1)) for w in weights],
        out_specs=[row(D_SSM), row(D_XBC),
                   pl.BlockSpec((N_DT_TILES, HEADS_PER_GROUP, tm), lambda i: (0, 0, i)),
                   col(D_ATTN), row(D_ATTN), col(D_ATTN), row(D_ATTN),
                   pl.BlockSpec((None, 1, LANES), lambda i: (i, 0, 0)),
                   pl.BlockSpec((None, N_MAPS, LANES), lambda i: (i, 0, 0))],
        out_shape=[jax.ShapeDtypeStruct((t, D_SSM), F32), jax.ShapeDtypeStruct((t, D_XBC), F32),
                   jax.ShapeDtypeStruct((N_DT_TILES, HEADS_PER_GROUP, t), F32),
                   jax.ShapeDtypeStruct((D_ATTN, t), BF16), jax.ShapeDtypeStruct((t, D_ATTN), BF16),
                   jax.ShapeDtypeStruct((D_ATTN, t), BF16), jax.ShapeDtypeStruct((t, D_ATTN), F32),
                   jax.ShapeDtypeStruct((nt, 1, LANES), F32), jax.ShapeDtypeStruct((nt, N_MAPS, LANES), F32)],
        compiler_params=pltpu.CompilerParams(dimension_semantics=("parallel",), vmem_limit_bytes=VMEM_LIMIT),
        name="ln_inproj",
    )(x2d, ln_g.reshape(1, -1), ln_b.reshape(1, -1), *weights)


HALO = 8


CONV_ROWS = 256


def _conv_silu_kernel(prev_ref, cur_ref, next_ref, w_ref, b_ref, xs_ref, bt_ref, c_ref, ext_ref):
    i = pl.program_id(1)
    n = pl.num_programs(1)
    tc = cur_ref.shape[0]
    ext_ref[0:HALO, :] = jnp.where(i == 0, 0.0, prev_ref[...])
    ext_ref[HALO:HALO + tc, :] = cur_ref[...]
    ext_ref[HALO + tc:, :] = jnp.where(i == n - 1, 0.0, next_ref[...])
    for cb in range(D_XBC // LANES):
        cols = slice(cb * LANES, (cb + 1) * LANES)
        for r0 in range(0, tc, CONV_ROWS):
            acc = b_ref[:, cols]
            for kk in range(D_CONV):
                acc = acc + ext_ref[pl.ds(r0 + HALO - D_CONV // 2 + kk, CONV_ROWS), cols] * w_ref[kk:kk + 1, cols]
            act = _silu(acc)
            rows = slice(r0, r0 + CONV_ROWS)
            if cb < D_SSM // LANES:
                xs_ref[rows, cols] = act
            elif cb < D_SSM // LANES + SSM_GROUPS:
                bt_ref[cb - D_SSM // LANES, :, rows] = act.T.astype(BF16)
            else:
                c_ref[rows, (cb - D_SSM // LANES - SSM_GROUPS) * LANES:(cb - D_SSM // LANES - SSM_GROUPS + 1) * LANES] = (
                    act.astype(BF16))


def _conv_silu(xbc, conv_w, conv_b, tc=512):
    b, s, _ = xbc.shape
    nb = s // tc
    hb = tc // HALO
    return pl.pallas_call(
        _conv_silu_kernel,
        grid=(b, nb),
        in_specs=[
            pl.BlockSpec((None, HALO, D_XBC), lambda bi, i: (bi, jnp.maximum(i * hb - 1, 0), 0)),
            pl.BlockSpec((None, tc, D_XBC), lambda bi, i: (bi, i, 0)),
            pl.BlockSpec((None, HALO, D_XBC), lambda bi, i: (bi, jnp.minimum((i + 1) * hb, s // HALO - 1), 0)),
            _const_spec((D_CONV, D_XBC)),
            _const_spec((1, D_XBC)),
        ],
        out_specs=[
            pl.BlockSpec((None, tc, D_SSM), lambda bi, i: (bi, i, 0)),
            pl.BlockSpec((None, SSM_GROUPS, D_STATE, tc), lambda bi, i: (bi, 0, 0, i)),
            pl.BlockSpec((None, tc, SSM_GROUPS * D_STATE), lambda bi, i: (bi, i, 0)),
        ],
        out_shape=[
            jax.ShapeDtypeStruct((b, s, D_SSM), F32),
            jax.ShapeDtypeStruct((b, SSM_GROUPS, D_STATE, s), BF16),
            jax.ShapeDtypeStruct((b, s, SSM_GROUPS * D_STATE), BF16),
        ],
        scratch_shapes=[pltpu.VMEM((tc + 2 * HALO, D_XBC), F32)],
        compiler_params=pltpu.CompilerParams(dimension_semantics=("parallel", "parallel"),
                                             vmem_limit_bytes=VMEM_LIMIT),
        name="conv_silu",
    )(xbc, xbc, xbc, conv_w, conv_b.reshape(1, -1))


SSD_CHUNKS_IN_FLIGHT = 2


def _ssd_bodies(rev, xs, bts, cms, dtTs, alogs, biases, groups, states):
    L = CHUNK
    r = lax.broadcasted_iota(jnp.int32, (L, L), 0)
    c = lax.broadcasted_iota(jnp.int32, (L, L), 1)
    keep = (c >= r) if rev else (c <= r)
    cum_r = ((r >= c) if rev else (r <= c)).astype(BF16)
    hg = HEADS_PER_GROUP
    eh = lax.broadcasted_iota(jnp.int32, (6 * hg, 2 * GROUP_WIDTH), 0)
    el = lax.broadcasted_iota(jnp.int32, (6 * hg, 2 * GROUP_WIDTH), 1)
    spread = ((el // GROUP_WIDTH == eh // (3 * hg)) & ((el % GROUP_WIDTH) // SSM_HEAD_DIM == eh % hg)).astype(BF16)
    lane = lax.broadcasted_iota(jnp.int32, (L, LANES), 1)
    edge = 0 if rev else L - 1
    gs = range(len(xs))

    dt_t = [_softplus(dtTs[g] + biases[g]) for g in gs]
    acs_t = [_dot_exact_lhs(dt_t[g] * (-jnp.exp(alogs[g])), cum_r) for g in gs]
    pieces_t = [jnp.concatenate([p.astype(F32) for p in _split3(dt_t[g]) + _split3(acs_t[g])], axis=0) for g in gs]
    both_x = [_dot(pieces_t[g].T.astype(BF16), spread) for g in gs]
    acs = [acs_t[g].T for g in gs]
    cb = [_dot(cms[g], bts[g]) for g in gs]
    dt_x = [both_x[g][:, :GROUP_WIDTH] for g in gs]
    acs_x = [both_x[g][:, GROUP_WIDTH:] for g in gs]
    tot_x = [acs_x[g][edge:edge + 1, :] for g in gs]
    xdt = [xs[g] * dt_x[g] for g in gs]

    y_diag = []
    for g in gs:
        tiles = []
        for t in range(GROUP_WIDTH // LANES):
            xt = xdt[g][:, t * LANES:(t + 1) * LANES]
            gmats, xparts = [], []
            for half in range(LANES // SSM_HEAD_DIM):
                j = t * (LANES // SSM_HEAD_DIM) + half
                seg = acs[g][:, j:j + 1] - acs_t[g][j:j + 1, :]
                gmats.append((cb[g] * jnp.exp(jnp.where(keep, seg, -jnp.inf))).astype(BF16))
                in_head = (lane >= half * SSM_HEAD_DIM) & (lane < (half + 1) * SSM_HEAD_DIM)
                xparts.append(jnp.where(in_head, xt, 0.0).astype(BF16))
            tiles.append(_dot(jnp.concatenate(gmats, axis=1), jnp.concatenate(xparts, axis=0)))
        y_diag.append(jnp.concatenate(tiles, axis=1))

    states = list(states)
    ys = []
    for g in gs:
        grp = groups[g]
        y_off = _dot(cms[g], states[grp].astype(BF16)) * jnp.exp(acs_x[g])
        ys.append(y_off + y_diag[g])
        w = (xdt[g] * jnp.exp(tot_x[g] - acs_x[g])).astype(BF16)
        states[grp] = jnp.exp(tot_x[g]) * states[grp] + _dot(bts[g], w)
    return ys, states


def _ssd_kernel(rev, nchunk, *refs):
    if rev:
        (xs_ref, bt_ref, c_ref, dtT_ref, alc_ref, bc_ref, yf_ref, z_ref, ng_ref, y_ref, state_ref) = refs
    else:
        (xs_ref, bt_ref, c_ref, dtT_ref, alc_ref, bc_ref, dx_ref, y_ref, state_ref) = refs

    @pl.when(pl.program_id(1) == 0)
    def _():
        state_ref[...] = jnp.zeros_like(state_ref)

    gw = GROUP_WIDTH
    gs = range(SSM_GROUPS)
    cols = [slice(g * gw, (g + 1) * gw) for g in gs]
    order = list(range(nchunk - 1, -1, -1) if rev else range(nchunk))
    states = [state_ref[g] for g in gs]
    for c0 in range(0, nchunk, SSD_CHUNKS_IN_FLIGHT):
        bodies = [(slice(ci * CHUNK, (ci + 1) * CHUNK), g) for ci in order[c0:c0 + SSD_CHUNKS_IN_FLIGHT] for g in gs]
        x = [xs_ref[rows, cols[g]] for rows, g in bodies]
        ys, states = _ssd_bodies(rev, x, [bt_ref[g, :, rows] for rows, g in bodies],
                                 [c_ref[rows, g * D_STATE:(g + 1) * D_STATE] for rows, g in bodies],
                                 [dtT_ref[g, :, rows] for rows, g in bodies],
                                 [alc_ref[g] for _, g in bodies], [bc_ref[g] for _, g in bodies],
                                 [g for _, g in bodies], states)
        for (rows, g), xb, y in zip(bodies, x, ys):
            if rev:
                yy = (yf_ref[rows, cols[g]] + y) * _silu(z_ref[rows, cols[g]])
                ms = jnp.mean(yy * yy, axis=-1, keepdims=True)
                y_ref[rows, cols[g]] = yy * lax.rsqrt(ms + RMS_EPS) * ng_ref[g]
            else:
                y_ref[rows, cols[g]] = y + dx_ref[g] * xb
    for g in gs:
        state_ref[g] = states[g]


def _ssd_pass(rev, xs, bt, cm, dtT, a_log, dt_bias, extra, nchunk=4):
    b, s, _ = xs.shape
    tcs = nchunk * CHUNK
    nblk = s // tcs
    blk = (lambda i: nblk - 1 - i) if rev else (lambda i: i)
    dirn = 1 if rev else 0
    gw, hg, ng = GROUP_WIDTH, HEADS_PER_GROUP, SSM_GROUPS
    seq_spec = lambda w: pl.BlockSpec((None, tcs, w), lambda bi, i: (bi, blk(i), 0))
    in_specs = [
        seq_spec(D_SSM),
        pl.BlockSpec((None, ng, D_STATE, tcs), lambda bi, i: (bi, 0, 0, blk(i))),
        seq_spec(ng * D_STATE),
        pl.BlockSpec((ng, hg, tcs), lambda bi, i: (dirn, 0, bi * nblk + blk(i))),
        _const_spec((ng, hg, 1)), _const_spec((ng, hg, 1)),
    ]
    args = [xs, bt, cm, dtT, a_log.reshape(ng, hg, 1), dt_bias.reshape(ng, hg, 1)]
    if rev:
        y_fwd, z, norm_g = extra
        in_specs += [seq_spec(D_SSM), seq_spec(D_SSM), _const_spec((ng, 1, gw))]
        args += [y_fwd, z, norm_g.reshape(ng, 1, gw)]
    else:
        (d_skip,) = extra
        in_specs += [_const_spec((ng, 1, gw))]
        args += [jnp.repeat(d_skip, SSM_HEAD_DIM).reshape(ng, 1, gw)]
    return pl.pallas_call(
        functools.partial(_ssd_kernel, rev, nchunk),
        grid=(b, nblk),
        in_specs=in_specs,
        out_specs=seq_spec(D_SSM),
        out_shape=jax.ShapeDtypeStruct((b, s, D_SSM), F32),
        scratch_shapes=[pltpu.VMEM((ng, D_STATE, gw), F32)],
        compiler_params=pltpu.CompilerParams(dimension_semantics=("parallel", "arbitrary"),
                                             vmem_limit_bytes=VMEM_LIMIT),
        name="ssd_bwd" if rev else "ssd_fwd",
    )(*args)


NEG_BIG = -1e30


SKIP_LOG2 = 80.0
FIXED_SHIFT_MAX_LOG2 = 60.0
PV_DEPTH = 256


def _attn_plan(tq, tk, nk, nq, bi, hi, qi, slope, kn_ref, qn_ref, plan_i_ref, plan_f_ref):
    i0 = qi * tq
    jd = qi // (tk // tq)
    parts = tq // ATTN_QPART
    qns, bases = [], []
    for r in range(2):
        base = (bi * ATTN_HEADS + hi) * 2 + r
        qn = qn_ref[base * (nq * parts) + qi * parts]
        for part in range(1, parts):
            qn = jnp.maximum(qn, qn_ref[base * (nq * parts) + qi * parts + part])
        qns.append(qn)
        bases.append(base * nk)
    jlo, jhi = jd, jd
    kmax = [kn_ref[bases[0]], kn_ref[bases[1]]]
    for jt in range(nk):
        dist = jnp.maximum(jnp.maximum(i0 - (jt * tk + tk - 1), jt * tk - (i0 + tq - 1)), 0).astype(F32)
        need = None
        for r in range(2):
            kn = kn_ref[bases[r] + jt]
            kmax[r] = jnp.maximum(kmax[r], kn)
            reach = SKIP_LOG2 + qns[r] * (kn + kn_ref[bases[r] + jd])
            need_r = slope * dist <= reach
            need = need_r if need is None else (need | need_r)
        jlo = jnp.where(need, jnp.minimum(jlo, jt), jlo)
        jhi = jnp.where(need, jnp.maximum(jhi, jt), jhi)
    plan_i_ref[0] = jlo
    plan_i_ref[1] = jhi
    odd = (jhi - jlo + 1) % 2
    room_above = (jhi < nk - 1).astype(jnp.int32)
    plan_i_ref[2] = jlo - odd * (1 - room_above)
    plan_i_ref[3] = jhi + odd * room_above
    plan_f_ref[0] = jnp.maximum(qns[0] * kmax[0], qns[1] * kmax[1])


def _attn_q_tile(tq, tk, nk, nq, bi, hi, qi, slope, kn_ref, qn_ref, qT_ref, k_ref, vT_ref, lq1_ref, lk1_ref, lq2_ref,
                 lk2_ref, sg_ref, o_ref, e_ref, s_ref, s1_ref, acc_ref, l_ref, plan_i_ref, plan_f_ref, pre_ref):
    ratio = tk // tq
    i0 = qi * tq
    q0 = pl.multiple_of(i0, tq)
    jd = qi // ratio
    dsel = qi % ratio
    jlo, jhi, jl, jh, score_bound = plan_i_ref[0], plan_i_ref[1], plan_i_ref[2], plan_i_ref[3], plan_f_ref[0]
    first_scores_ready = pre_ref[0]
    qi_next = jnp.minimum(qi + 1, nq - 1)

    def plan_next():
        _attn_plan(tq, tk, nk, nq, bi, hi, qi_next, slope, kn_ref, qn_ref, plan_i_ref, plan_f_ref)

    row = lax.broadcasted_iota(jnp.int32, (2 * ATTN_HEAD_DIM, tq), 0)

    def masked_q(q_start):
        qf = qT_ref[:, pl.ds(q_start, tq)].astype(F32)
        return (jnp.where(row < ATTN_HEAD_DIM, qf, 0.0).astype(BF16),
                jnp.where(row >= ATTN_HEAD_DIM, qf, 0.0).astype(BF16))

    rhs = masked_q(q0)

    def finish_previous():
        _attn_epilogue(tq, jnp.maximum(qi - 1, 0), lq1_ref, lk1_ref, lq2_ref, lk2_ref, sg_ref, o_ref, acc_ref, l_ref)
        acc_ref[...] = jnp.zeros_like(acc_ref)

    def tile_bias(j):
        off = (i0 - j * tk).astype(F32)
        before, after = j < jd, j > jd
        eidx = jnp.where(before, 0, jnp.where(after, 1, 2 + dsel))
        cj = jnp.where(before, -slope * (off - float(tk - 1)), jnp.where(after, slope * (off + float(tq - 1)), 0.0))
        return eidx, cj

    @pl.when(score_bound <= FIXED_SHIFT_MAX_LOG2)
    def _():
        def stage_a(j, buf, w):
            kt = k_ref[pl.ds(pl.multiple_of(j * tk, tk), tk), :]
            for r in range(2):
                buf[r] = _dot(kt, w[r])

        @pl.when(first_scores_ready == 0)
        def _():
            stage_a(jl, s_ref, rhs)

        plan_next()
        jl_next = plan_i_ref[2]
        rhs_next = masked_q(pl.multiple_of(qi_next * tq, tq))

        def stage_b(j, buf, l):
            eidx, cj = tile_bias(j)
            half = jnp.exp2(jnp.full((1, tq), 0.5 * cj, F32))
            vt = vT_ref[:, pl.ds(pl.multiple_of(j * tk, tk), tk)]
            l_new = []
            for r in range(2):
                psum, pv = None, None
                for k0 in range(0, tk, PV_DEPTH):
                    p = jnp.exp2(buf[r, k0:k0 + PV_DEPTH, :] + e_ref[eidx, k0:k0 + PV_DEPTH, :])
                    ps = jnp.sum(p, axis=0, keepdims=True)
                    d = _dot(vt[:, k0:k0 + PV_DEPTH], p.astype(BF16))
                    psum, pv = (ps, d) if pv is None else (psum + ps, pv + d)
                l_new.append(l[r] + psum * half * half)
                acc_ref[r] += pv * half * half
            return tuple(l_new)

        def pair(j, l):
            stage_a(j + 1, s1_ref, rhs)
            l = stage_b(j, s_ref, l)
            done = j + 2 > jh
            w_next = tuple(jnp.where(done, rhs_next[r], rhs[r]) for r in range(2))
            stage_a(jnp.where(done, jl_next, j + 2), s_ref, w_next)
            return stage_b(j + 1, s1_ref, l)

        def body(t, l):
            j = jl + 2 + 4 * t
            return pair(j + 2, pair(j, l))

        finish_previous()
        zero = jnp.zeros((1, tq), F32)
        l = pair(jl, (zero, zero))
        rest = (jh - jl + 1) // 2 - 1
        l = lax.fori_loop(0, rest // 2, body, l)
        l_ref[0] = l[0]
        l_ref[1] = l[1]

        @pl.when(rest % 2 == 1)
        def _():
            l = pair(jh - 1, (l_ref[0], l_ref[1]))
            l_ref[0] = l[0]
            l_ref[1] = l[1]

        pre_ref[0] = 1

    def stage_a(j, m_old):
        eidx, cj = tile_bias(j)
        kt = k_ref[pl.ds(pl.multiple_of(j * tk, tk), tk), :]
        m_new, shift, alpha = [], [], []
        for r in range(2):
            s = _dot(kt, rhs[r]) + e_ref[eidx]
            s_ref[r] = s
            mr = jnp.maximum(m_old[r], jnp.max(s, axis=0, keepdims=True) + cj)
            m_new.append(mr)
            shift.append(mr - cj)
            alpha.append(jnp.exp2(m_old[r] - mr))
        return tuple(m_new), tuple(shift), tuple(alpha)

    def stage_b(j, shift, alpha, l_old):
        vt = vT_ref[:, pl.ds(pl.multiple_of(j * tk, tk), tk)]
        l_new = []
        for r in range(2):
            p = jnp.exp2(s_ref[r] - shift[r])
            l_new.append(alpha[r] * l_old[r] + jnp.sum(p, axis=0, keepdims=True))
            acc_ref[r] = alpha[r] * acc_ref[r] + _dot(vt, p.astype(BF16))
        return tuple(l_new)

    def body(t, carry):
        m, shift, alpha, l = carry
        j = jlo + t
        l = stage_b(jnp.maximum(j - 1, jlo), shift, alpha, l)
        m, shift, alpha = stage_a(jnp.minimum(j, jhi), m)
        return m, shift, alpha, l

    @pl.when(score_bound > FIXED_SHIFT_MAX_LOG2)
    def _():
        neg = jnp.full((1, tq), NEG_BIG, F32)
        big = jnp.full((1, tq), -NEG_BIG, F32)
        one = jnp.ones((1, tq), F32)
        zero = jnp.zeros((1, tq), F32)
        finish_previous()
        plan_next()
        pre_ref[0] = 0
        init = ((neg, neg), (big, big), (one, one), (zero, zero))
        _, _, _, l = lax.fori_loop(0, jhi - jlo + 2, body, init)
        l_ref[0] = l[0]
        l_ref[1] = l[1]


def _attn_epilogue(tq, qi, lq1_ref, lk1_ref, lq2_ref, lk2_ref, sg_ref, o_ref, acc_ref, l_ref):
    lam = (jnp.exp(jnp.sum(lq1_ref[...] * lk1_ref[...], axis=-1, keepdims=True))
           - jnp.exp(jnp.sum(lq2_ref[...] * lk2_ref[...], axis=-1, keepdims=True)) + LAM_INIT)
    o = acc_ref[0] / l_ref[0] - lam * (acc_ref[1] / l_ref[1])
    ms = jnp.mean(o * o, axis=0, keepdims=True)
    q0 = pl.multiple_of(qi * tq, tq)
    o_ref[pl.ds(q0, tq), :] = (o * lax.rsqrt(ms + RMS_EPS) * sg_ref[...] * (1.0 - LAM_INIT)).T


def _attn_kernel(tq, tk, nk, nq, slopes_ref, kn_ref, qn_ref, qT_ref, k_ref, vT_ref, lq1_ref, lk1_ref, lq2_ref,
                 lk2_ref, sg_ref, o_ref, e_ref, s_ref, s1_ref, acc_ref, l_ref, plan_i_ref, plan_f_ref, pre_ref):
    bi, hi = pl.program_id(0), pl.program_id(1)
    slope = slopes_ref[hi]
    ii = lax.broadcasted_iota(jnp.int32, (tk, tq), 1)
    jj = lax.broadcasted_iota(jnp.int32, (tk, tq), 0)
    rel = (ii - jj).astype(F32)
    e_ref[0] = -slope * (rel + float(tk - 1))
    e_ref[1] = slope * (rel - float(tq - 1))
    for d in range(tk // tq):
        e_ref[2 + d] = -slope * jnp.abs(rel + float(d * tq))
    s_ref[...] = jnp.zeros_like(s_ref)
    acc_ref[...] = jnp.zeros_like(acc_ref)
    l_ref[...] = jnp.ones_like(l_ref)
    _attn_plan(tq, tk, nk, nq, bi, hi, 0, slope, kn_ref, qn_ref, plan_i_ref, plan_f_ref)
    pre_ref[0] = 0

    def q_tile(qi, carry):
        _attn_q_tile(tq, tk, nk, nq, bi, hi, qi, slope, kn_ref, qn_ref, qT_ref, k_ref, vT_ref, lq1_ref, lk1_ref,
                     lq2_ref, lk2_ref, sg_ref, o_ref, e_ref, s_ref, s1_ref, acc_ref, l_ref, plan_i_ref, plan_f_ref,
                     pre_ref)
        return carry

    lax.fori_loop(0, nq, q_tile, 0)
    _attn_epilogue(tq, nq - 1, lq1_ref, lk1_ref, lq2_ref, lk2_ref, sg_ref, o_ref, acc_ref, l_ref)


def _diff_attn(qT, k3, vT, kn, qn, lq1, lk1, lq2, lk2, subln_g, tq=ATTN_TQ, tk=ATTN_TK):
    b, s, _ = k3.shape
    nq, nk = s // tq, s // tk
    ratio = tk // tq
    assert tk % tq == 0 and nk % 2 == 0
    hw = 2 * ATTN_HEAD_DIM
    slopes = jnp.exp2(-8.0 * (jnp.arange(ATTN_HEADS, dtype=F32) + 1.0) / ATTN_HEADS) * LOG2E
    kn_tab = jnp.transpose(kn[:, 0, :N_MAPS].reshape(b, nk, N_MAPS), (0, 2, 1)).reshape(-1)
    kparts = tk // ATTN_QPART
    qn_tab = jnp.transpose(qn[:, :, :kparts].reshape(b, nk, N_MAPS, kparts), (0, 2, 1, 3)).reshape(-1)
    vec = lambda a: a.reshape(1, -1)
    return pl.pallas_call(
        functools.partial(_attn_kernel, tq, tk, nk, nq),
        grid=(b, ATTN_HEADS),
        in_specs=[
            pl.BlockSpec(memory_space=pltpu.SMEM),
            pl.BlockSpec(memory_space=pltpu.SMEM),
            pl.BlockSpec(memory_space=pltpu.SMEM),
            pl.BlockSpec((hw, s), lambda bi, hi: (hi, bi)),
            pl.BlockSpec((None, s, hw), lambda bi, hi: (bi, 0, hi)),
            pl.BlockSpec((ATTN_V_DIM, s), lambda bi, hi: (hi, bi)),
            _const_spec((1, ATTN_HEAD_DIM)), _const_spec((1, ATTN_HEAD_DIM)),
            _const_spec((1, ATTN_HEAD_DIM)), _const_spec((1, ATTN_HEAD_DIM)),
            _const_spec((ATTN_V_DIM, 1)),
        ],
        out_specs=pl.BlockSpec((None, s, ATTN_V_DIM), lambda bi, hi: (bi, 0, hi)),
        out_shape=jax.ShapeDtypeStruct((b, s, ATTN_HEADS * ATTN_V_DIM), F32),
        scratch_shapes=[pltpu.VMEM((2 + tk // tq, tk, tq), F32),
                        pltpu.VMEM((2, tk, tq), F32), pltpu.VMEM((2, tk, tq), F32),
                        pltpu.VMEM((2, ATTN_V_DIM, tq), F32), pltpu.VMEM((2, 1, tq), F32),
                        pltpu.SMEM((4,), jnp.int32), pltpu.SMEM((1,), F32), pltpu.SMEM((1,), jnp.int32)],
        compiler_params=pltpu.CompilerParams(dimension_semantics=("parallel", "parallel"),
                                             vmem_limit_bytes=VMEM_LIMIT),
        name="diff_attn",
    )(slopes, kn_tab, qn_tab, qT, k3, vT, vec(lq1), vec(lk1), vec(lq2), vec(lk2), subln_g.reshape(-1, 1))


def _out_proj_kernel(x_ref, eg_ref, eb_ref, ys_ref, o_ref, gate_ref, w_ref, g_ref, b_ref, out_ref):
    h = _layer_norm(x_ref[...], eg_ref[...], eb_ref[...])
    ya = o_ref[...] * _silu(gate_ref[...])
    mix = _dot(ys_ref[...].astype(BF16), w_ref[:D_SSM, :]) + _dot(ya.astype(BF16), w_ref[D_SSM:, :])
    out_ref[...] = _layer_norm(ALPHA * h + mix, g_ref[...], b_ref[...])


def _out_proj(x2d, ln_emb_g, ln_emb_b, y_ssm, o, gate, w_out, ln_g, ln_b, tm=512):
    t = x2d.shape[0]
    row = lambda n: pl.BlockSpec((tm, n), lambda i: (i, 0))
    vec = lambda a: a.reshape(1, -1)
    return pl.pallas_call(
        _out_proj_kernel,
        grid=(t // tm,),
        in_specs=[row(D_MODEL), _const_spec((1, D_MODEL)), _const_spec((1, D_MODEL)),
                  row(D_SSM), row(D_ATTN), row(D_ATTN),
                  _const_spec((D_SSM + D_ATTN, D_MODEL)), _const_spec((1, D_MODEL)), _const_spec((1, D_MODEL))],
        out_specs=row(D_MODEL),
        out_shape=jax.ShapeDtypeStruct((t, D_MODEL), F32),
        compiler_params=pltpu.CompilerParams(dimension_semantics=("parallel",), vmem_limit_bytes=VMEM_LIMIT),
        name="out_proj",
    )(x2d, vec(ln_emb_g), vec(ln_emb_b), y_ssm, o, gate, w_out.astype(BF16), vec(ln_g), vec(ln_b))


def kernel(x, ln_emb_g, ln_emb_b, w_in, conv_w, conv_b, A_log_fwd, A_log_bwd, dt_bias_fwd, dt_bias_bwd, D_skip,
           ssm_norm_g, lambda_q1, lambda_k1, lambda_q2, lambda_k2, subln_g, w_out, ln_g, ln_b):
    b, s, _ = x.shape
    t = b * s
    x2d = x.reshape(t, D_MODEL)
    z, xbc, dtT, qT, k, vT, gate, kn, qn = _ln_inproj(x2d, ln_emb_g, ln_emb_b, w_in[0])

    xs, bt, cm = _conv_silu(xbc.reshape(b, s, D_XBC), conv_w[0], conv_b[0])
    y_fwd = _ssd_pass(False, xs, bt, cm, dtT, A_log_fwd[0], dt_bias_fwd[0], (D_skip[0],))
    y_ssm = _ssd_pass(True, xs, bt, cm, dtT, A_log_bwd[0], dt_bias_bwd[0],
                      (y_fwd, z.reshape(b, s, D_SSM), ssm_norm_g[0]))

    o = _diff_attn(qT, k.reshape(b, s, D_ATTN), vT, kn, qn, lambda_q1[0], lambda_k1[0], lambda_q2[0],
                   lambda_k2[0], subln_g[0]).reshape(t, D_ATTN)

    out = _out_proj(x2d, ln_emb_g, ln_emb_b, y_ssm.reshape(t, D_SSM), o, gate, w_out[0], ln_g[0], ln_b[0])
    return out.reshape(b, s, D_MODEL)
```

```python
import functools
import math

import jax
import jax.numpy as jnp
from jax import lax
from jax.experimental import pallas as pl
from jax.experimental.pallas import tpu as pltpu

D_MODEL = 1024
D_SSM = 1024
SSM_HEAD_DIM = 64
SSM_HEADS = 16
SSM_GROUPS = 2
HEADS_PER_GROUP = SSM_HEADS // SSM_GROUPS
D_STATE = 128
D_CONV = 5
CHUNK = 128
D_XBC = D_SSM + 2 * SSM_GROUPS * D_STATE
D_ATTN = 1024
ATTN_HEADS = 8
ATTN_HEAD_DIM = 64
ATTN_V_DIM = 128
GROUP_WIDTH = D_SSM // SSM_GROUPS
DEPTH = 1
ALPHA = (2.0 * DEPTH) ** 0.25
LN_EPS = 1e-5
RMS_EPS = 1e-5
LAM_INIT = 0.8 - 0.6 * math.exp(-0.3 * 0)

LANES = 128
VMEM_LIMIT = 48 * 1024 * 1024

F32 = jnp.float32
BF16 = jnp.bfloat16


LOG2E = math.log2(math.e)


def _dot(a, b):
    return jnp.dot(a, b, preferred_element_type=F32)


def _dot_nt(a, b):
    return lax.dot_general(a, b, (((1,), (1,)), ((), ())), preferred_element_type=F32)


def _layer_norm(x, g, b):
    mu = jnp.mean(x, axis=-1, keepdims=True)
    xc = x - mu
    var = jnp.mean(xc * xc, axis=-1, keepdims=True)
    return xc * lax.rsqrt(var + LN_EPS) * g + b


def _silu(x):
    return x / (1.0 + jnp.exp(-x))


def _softplus(x):
    return jnp.maximum(x, 0.0) + jnp.log1p(jnp.exp(-jnp.abs(x)))


def _split3(x):
    hi = x.astype(BF16)
    r = x - hi.astype(F32)
    mid = r.astype(BF16)
    lo = (r - mid.astype(F32)).astype(BF16)
    return hi, mid, lo


def _dot_exact_rhs(m, x):
    hi, mid, lo = _split3(x)
    return _dot(m, hi) + _dot(m, mid) + _dot(m, lo)


def _dot_exact_lhs(x, m):
    hi, mid, lo = _split3(x)
    return _dot(hi, m) + _dot(mid, m) + _dot(lo, m)


def _const_spec(shape):
    nd = len(shape)
    return pl.BlockSpec(shape, lambda *_: (0,) * nd)


N_DT_TILES = 2 * SSM_GROUPS


ATTN_TQ = 256
ATTN_QPART = 256
ATTN_TK = 512
N_MAPS = 2 * ATTN_HEADS
NORM_SLACK = 1.01


def _ln_inproj_kernel(x_ref, g_ref, b_ref, wz_ref, wxbc_ref, wdtT_ref, wqT_ref, wk_ref, wvT_ref, wg_ref,
                      grpT_ref,
                      z_ref, xbc_ref, dtT_ref, qT_ref, k_ref, vT_ref, gate_ref, kn_ref, qn_ref):
    h = _layer_norm(x_ref[...], g_ref[...], b_ref[...]).astype(BF16)
    z_ref[...] = _dot(h, wz_ref[...])
    xbc_ref[...] = _dot(h, wxbc_ref[...])
    dtT_ref[...] = _dot_nt(wdtT_ref[...], h).reshape(dtT_ref.shape)
    qT = _dot_nt(wqT_ref[...], h) * (LOG2E * ATTN_HEAD_DIM ** -0.5)
    qT_ref[...] = qT.astype(BF16)
    k = _dot(h, wk_ref[...])
    k_ref[...] = k.astype(BF16)
    vT_ref[...] = _dot_nt(wvT_ref[...], h).astype(BF16)
    gate_ref[...] = _dot(h, wg_ref[...])
    lane = lax.broadcasted_iota(jnp.int32, (N_MAPS, LANES), 1)
    k2 = _dot_nt(grpT_ref[...], (k * k).astype(BF16))
    kn = jnp.sqrt(jnp.max(k2, axis=1, keepdims=True)) * NORM_SLACK
    kn_ref[...] = jnp.where(lane == 0, kn, 0.0)
    q2 = _dot(grpT_ref[...], (qT * qT).astype(BF16))
    qn = jnp.zeros((N_MAPS, LANES), F32)
    for part in range(q2.shape[1] // ATTN_QPART):
        pm = jnp.max(q2[:, part * ATTN_QPART:(part + 1) * ATTN_QPART], axis=1, keepdims=True)
        qn = jnp.where(lane == part, jnp.sqrt(pm) * NORM_SLACK, qn)
    qn_ref[...] = qn


def _ln_inproj(x2d, ln_g, ln_b, w_in, tm=ATTN_TK):
    t = x2d.shape[0]
    grp_t = (jnp.arange(N_MAPS)[:, None] == jnp.arange(D_ATTN)[None, :] // ATTN_HEAD_DIM).astype(BF16)
    offs = [0, D_SSM, D_SSM + D_XBC, D_SSM + D_XBC + 2 * SSM_HEADS]
    wz = w_in[:, offs[0]:offs[1]].astype(BF16)
    wxbc = w_in[:, offs[1]:offs[2]].astype(BF16)
    wdt = w_in[:, offs[2]:offs[3]].astype(BF16)
    o = offs[3]
    wq, wk, wv, wg = (w_in[:, o + i * D_ATTN:o + (i + 1) * D_ATTN].astype(BF16) for i in range(4))
    row = lambda n: pl.BlockSpec((tm, n), lambda i: (i, 0))
    col = lambda n: pl.BlockSpec((n, tm), lambda i: (0, i))
    weights = [wz, wxbc, wdt.T, wq.T, wk, wv.T, wg, grp_t]
    nt = t // tm
    return pl.pallas_call(
        _ln_inproj_kernel,
        grid=(nt,),
        in_specs=[row(D_MODEL), _const_spec((1, D_MODEL)), _const_spec((1, D_MODEL))]
                 + [pl.BlockSpec(w.shape, lambda i: (0, 0), pipeline_mode=pl.Buffered(1)) for w in weights],
        out_specs=[row(D_SSM), row(D_XBC),
                   pl.BlockSpec((N_DT_TILES, HEADS_PER_GROUP, tm), lambda i: (0, 0, i)),
                   col(D_ATTN), row(D_ATTN), col(D_ATTN), row(D_ATTN),
                   pl.BlockSpec((None, N_MAPS, LANES), lambda i: (i, 0, 0)),
                   pl.BlockSpec((None, N_MAPS, LANES), lambda i: (i, 0, 0))],
        out_shape=[jax.ShapeDtypeStruct((t, D_SSM), F32), jax.ShapeDtypeStruct((t, D_XBC), F32),
                   jax.ShapeDtypeStruct((N_DT_TILES, HEADS_PER_GROUP, t), F32),
                   jax.ShapeDtypeStruct((D_ATTN, t), BF16), jax.ShapeDtypeStruct((t, D_ATTN), BF16),
                   jax.ShapeDtypeStruct((D_ATTN, t), BF16), jax.ShapeDtypeStruct((t, D_ATTN), F32),
                   jax.ShapeDtypeStruct((nt, N_MAPS, LANES), F32), jax.ShapeDtypeStruct((nt, N_MAPS, LANES), F32)],
        compiler_params=pltpu.CompilerParams(dimension_semantics=("parallel",), vmem_limit_bytes=VMEM_LIMIT),
        name="ln_inproj",
    )(x2d, ln_g.reshape(1, -1), ln_b.reshape(1, -1), *weights)


HALO = 8


CONV_ROWS = 128


def _conv_silu_kernel(prev_ref, cur_ref, next_ref, w_ref, b_ref, xs_ref, bt_ref, c_ref, ext_ref):
    i = pl.program_id(1)
    n = pl.num_programs(1)
    tc = cur_ref.shape[0]
    ext_ref[0:HALO, :] = jnp.where(i == 0, 0.0, prev_ref[...])
    ext_ref[HALO:HALO + tc, :] = cur_ref[...]
    ext_ref[HALO + tc:, :] = jnp.where(i == n - 1, 0.0, next_ref[...])
    for cb in range(D_XBC // LANES):
        cols = slice(cb * LANES, (cb + 1) * LANES)
        for r0 in range(0, tc, CONV_ROWS):
            acc = b_ref[:, cols]
            for kk in range(D_CONV):
                acc = acc + ext_ref[pl.ds(r0 + HALO - D_CONV // 2 + kk, CONV_ROWS), cols] * w_ref[kk:kk + 1, cols]
            act = _silu(acc)
            rows = slice(r0, r0 + CONV_ROWS)
            if cb < D_SSM // LANES:
                xs_ref[rows, cols] = act
            elif cb < D_SSM // LANES + SSM_GROUPS:
                bt_ref[cb - D_SSM // LANES, :, rows] = act.T.astype(BF16)
            else:
                c_ref[rows, (cb - D_SSM // LANES - SSM_GROUPS) * LANES:(cb - D_SSM // LANES - SSM_GROUPS + 1) * LANES] = (
                    act.astype(BF16))


def _conv_silu(xbc, conv_w, conv_b, tc=512):
    b, s, _ = xbc.shape
    nb = s // tc
    hb = tc // HALO
    return pl.pallas_call(
        _conv_silu_kernel,
        grid=(b, nb),
        in_specs=[
            pl.BlockSpec((None, HALO, D_XBC), lambda bi, i: (bi, jnp.maximum(i * hb - 1, 0), 0)),
            pl.BlockSpec((None, tc, D_XBC), lambda bi, i: (bi, i, 0)),
            pl.BlockSpec((None, HALO, D_XBC), lambda bi, i: (bi, jnp.minimum((i + 1) * hb, s // HALO - 1), 0)),
            _const_spec((D_CONV, D_XBC)),
            _const_spec((1, D_XBC)),
        ],
        out_specs=[
            pl.BlockSpec((None, tc, D_SSM), lambda bi, i: (bi, i, 0)),
            pl.BlockSpec((None, SSM_GROUPS, D_STATE, tc), lambda bi, i: (bi, 0, 0, i)),
            pl.BlockSpec((None, tc, SSM_GROUPS * D_STATE), lambda bi, i: (bi, i, 0)),
        ],
        out_shape=[
            jax.ShapeDtypeStruct((b, s, D_SSM), F32),
            jax.ShapeDtypeStruct((b, SSM_GROUPS, D_STATE, s), BF16),
            jax.ShapeDtypeStruct((b, s, SSM_GROUPS * D_STATE), BF16),
        ],
        scratch_shapes=[pltpu.VMEM((tc + 2 * HALO, D_XBC), F32)],
        compiler_params=pltpu.CompilerParams(dimension_semantics=("parallel", "parallel"),
                                             vmem_limit_bytes=VMEM_LIMIT),
        name="conv_silu",
    )(xbc, xbc, xbc, conv_w, conv_b.reshape(1, -1))


SSD_CHUNKS_IN_FLIGHT = 4


def _ssd_bodies(rev, xs, bts, cms, dtTs, alogs, biases, groups, states):
    L = CHUNK
    r = lax.broadcasted_iota(jnp.int32, (L, L), 0)
    c = lax.broadcasted_iota(jnp.int32, (L, L), 1)
    keep = (c >= r) if rev else (c <= r)
    cum_r = ((r >= c) if rev else (r <= c)).astype(BF16)
    hg = HEADS_PER_GROUP
    eh = lax.broadcasted_iota(jnp.int32, (6 * hg, 2 * GROUP_WIDTH), 0)
    el = lax.broadcasted_iota(jnp.int32, (6 * hg, 2 * GROUP_WIDTH), 1)
    spread = ((el // GROUP_WIDTH == eh // (3 * hg)) & ((el % GROUP_WIDTH) // SSM_HEAD_DIM == eh % hg)).astype(BF16)
    lane = lax.broadcasted_iota(jnp.int32, (L, LANES), 1)
    edge = 0 if rev else L - 1
    gs = range(len(xs))

    dt_t = [_softplus(dtTs[g] + biases[g]) for g in gs]
    acs_t = [_dot_exact_lhs(dt_t[g] * (-jnp.exp(alogs[g])), cum_r) for g in gs]
    pieces_t = [jnp.concatenate([p.astype(F32) for p in _split3(dt_t[g]) + _split3(acs_t[g])], axis=0) for g in gs]
    both_x = [_dot(pieces_t[g].T.astype(BF16), spread) for g in gs]
    acs = [acs_t[g].T for g in gs]
    cb = [_dot(cms[g], bts[g]) for g in gs]
    dt_x = [both_x[g][:, :GROUP_WIDTH] for g in gs]
    acs_x = [both_x[g][:, GROUP_WIDTH:] for g in gs]
    tot_x = [acs_x[g][edge:edge + 1, :] for g in gs]
    xdt = [xs[g] * dt_x[g] for g in gs]

    y_diag = []
    for g in gs:
        tiles = []
        for t in range(GROUP_WIDTH // LANES):
            xt = xdt[g][:, t * LANES:(t + 1) * LANES]
            gmats, xparts = [], []
            for half in range(LANES // SSM_HEAD_DIM):
                j = t * (LANES // SSM_HEAD_DIM) + half
                seg = acs[g][:, j:j + 1] - acs_t[g][j:j + 1, :]
                gmats.append((cb[g] * jnp.exp(jnp.where(keep, seg, -jnp.inf))).astype(BF16))
                in_head = (lane >= half * SSM_HEAD_DIM) & (lane < (half + 1) * SSM_HEAD_DIM)
                xparts.append(jnp.where(in_head, xt, 0.0).astype(BF16))
            tiles.append(_dot(jnp.concatenate(gmats, axis=1), jnp.concatenate(xparts, axis=0)))
        y_diag.append(jnp.concatenate(tiles, axis=1))

    states = list(states)
    ys = []
    for g in gs:
        grp = groups[g]
        y_off = _dot(cms[g], states[grp].astype(BF16)) * jnp.exp(acs_x[g])
        ys.append(y_off + y_diag[g])
        w = (xdt[g] * jnp.exp(tot_x[g] - acs_x[g])).astype(BF16)
        states[grp] = jnp.exp(tot_x[g]) * states[grp] + _dot(bts[g], w)
    return ys, states


def _ssd_kernel(rev, nchunk, *refs):
    if rev:
        (xs_ref, bt_ref, c_ref, dtT_ref, alc_ref, bc_ref, yf_ref, z_ref, ng_ref, y_ref, state_ref) = refs
    else:
        (xs_ref, bt_ref, c_ref, dtT_ref, alc_ref, bc_ref, dx_ref, y_ref, state_ref) = refs

    @pl.when(pl.program_id(1) == 0)
    def _():
        state_ref[...] = jnp.zeros_like(state_ref)

    gw = GROUP_WIDTH
    gs = range(SSM_GROUPS)
    cols = [slice(g * gw, (g + 1) * gw) for g in gs]
    order = list(range(nchunk - 1, -1, -1) if rev else range(nchunk))
    states = [state_ref[g] for g in gs]
    for c0 in range(0, nchunk, SSD_CHUNKS_IN_FLIGHT):
        bodies = [(slice(ci * CHUNK, (ci + 1) * CHUNK), g) for ci in order[c0:c0 + SSD_CHUNKS_IN_FLIGHT] for g in gs]
        x = [xs_ref[rows, cols[g]] for rows, g in bodies]
        ys, states = _ssd_bodies(rev, x, [bt_ref[g, :, rows] for rows, g in bodies],
                                 [c_ref[rows, g * D_STATE:(g + 1) * D_STATE] for rows, g in bodies],
                                 [dtT_ref[g, :, rows] for rows, g in bodies],
                                 [alc_ref[g] for _, g in bodies], [bc_ref[g] for _, g in bodies],
                                 [g for _, g in bodies], states)
        for (rows, g), xb, y in zip(bodies, x, ys):
            if rev:
                yy = (yf_ref[rows, cols[g]] + y) * _silu(z_ref[rows, cols[g]])
                ms = jnp.mean(yy * yy, axis=-1, keepdims=True)
                y_ref[rows, cols[g]] = yy * lax.rsqrt(ms + RMS_EPS) * ng_ref[g]
            else:
                y_ref[rows, cols[g]] = y + dx_ref[g] * xb
    for g in gs:
        state_ref[g] = states[g]


def _ssd_pass(rev, xs, bt, cm, dtT, a_log, dt_bias, extra, nchunk=4):
    b, s, _ = xs.shape
    tcs = nchunk * CHUNK
    nblk = s // tcs
    blk = (lambda i: nblk - 1 - i) if rev else (lambda i: i)
    dirn = 1 if rev else 0
    gw, hg, ng = GROUP_WIDTH, HEADS_PER_GROUP, SSM_GROUPS
    seq_spec = lambda w: pl.BlockSpec((None, tcs, w), lambda bi, i: (bi, blk(i), 0))
    in_specs = [
        seq_spec(D_SSM),
        pl.BlockSpec((None, ng, D_STATE, tcs), lambda bi, i: (bi, 0, 0, blk(i))),
        seq_spec(ng * D_STATE),
        pl.BlockSpec((ng, hg, tcs), lambda bi, i: (dirn, 0, bi * nblk + blk(i))),
        _const_spec((ng, hg, 1)), _const_spec((ng, hg, 1)),
    ]
    args = [xs, bt, cm, dtT, a_log.reshape(ng, hg, 1), dt_bias.reshape(ng, hg, 1)]
    if rev:
        y_fwd, z, norm_g = extra
        in_specs += [seq_spec(D_SSM), seq_spec(D_SSM), _const_spec((ng, 1, gw))]
        args += [y_fwd, z, norm_g.reshape(ng, 1, gw)]
    else:
        (d_skip,) = extra
        in_specs += [_const_spec((ng, 1, gw))]
        args += [jnp.repeat(d_skip, SSM_HEAD_DIM).reshape(ng, 1, gw)]
    return pl.pallas_call(
        functools.partial(_ssd_kernel, rev, nchunk),
        grid=(b, nblk),
        in_specs=in_specs,
        out_specs=seq_spec(D_SSM),
        out_shape=jax.ShapeDtypeStruct((b, s, D_SSM), F32),
        scratch_shapes=[pltpu.VMEM((ng, D_STATE, gw), F32)],
        compiler_params=pltpu.CompilerParams(dimension_semantics=("parallel", "arbitrary"),
                                             vmem_limit_bytes=VMEM_LIMIT),
        name="ssd_bwd" if rev else "ssd_fwd",
    )(*args)


NEG_BIG = -1e30


SKIP_LOG2 = 80.0
FIXED_SHIFT_MAX_LOG2 = 60.0
PV_DEPTH = 256


def _attn_plan(tq, tk, nk, nq, bi, hi, qi, slope, kn_ref, qn_ref, plan_i_ref, plan_f_ref):
    i0 = qi * tq
    jd = qi // (tk // tq)
    parts = tq // ATTN_QPART
    qns, bases = [], []
    for r in range(2):
        base = (bi * ATTN_HEADS + hi) * 2 + r
        qn = qn_ref[base * (nq * parts) + qi * parts]
        for part in range(1, parts):
            qn = jnp.maximum(qn, qn_ref[base * (nq * parts) + qi * parts + part])
        qns.append(qn)
        bases.append(base * nk)
    jlo, jhi = jd, jd
    kmax = [kn_ref[bases[0]], kn_ref[bases[1]]]
    for jt in range(nk):
        dist = jnp.maximum(jnp.maximum(i0 - (jt * tk + tk - 1), jt * tk - (i0 + tq - 1)), 0).astype(F32)
        need = None
        for r in range(2):
            kn = kn_ref[bases[r] + jt]
            kmax[r] = jnp.maximum(kmax[r], kn)
            reach = SKIP_LOG2 + qns[r] * (kn + kn_ref[bases[r] + jd])
            need_r = slope * dist <= reach
            need = need_r if need is None else (need | need_r)
        jlo = jnp.where(need, jnp.minimum(jlo, jt), jlo)
        jhi = jnp.where(need, jnp.maximum(jhi, jt), jhi)
    plan_i_ref[0] = jlo
    plan_i_ref[1] = jhi
    odd = (jhi - jlo + 1) % 2
    room_above = (jhi < nk - 1).astype(jnp.int32)
    plan_i_ref[2] = jlo - odd * (1 - room_above)
    plan_i_ref[3] = jhi + odd * room_above
    plan_f_ref[0] = jnp.maximum(qns[0] * kmax[0], qns[1] * kmax[1])


def _attn_q_tile(tq, tk, nk, nq, bi, hi, qi, slope, kn_ref, qn_ref, qT_ref, k_ref, vT_ref, lq1_ref, lk1_ref, lq2_ref,
                 lk2_ref, sg_ref, o_ref, e_ref, s_ref, s1_ref, acc_ref, l_ref, plan_i_ref, plan_f_ref, pre_ref):
    ratio = tk // tq
    i0 = qi * tq
    q0 = pl.multiple_of(i0, tq)
    jd = qi // ratio
    dsel = qi % ratio
    jlo, jhi, jl, jh, score_bound = plan_i_ref[0], plan_i_ref[1], plan_i_ref[2], plan_i_ref[3], plan_f_ref[0]
    first_scores_ready = pre_ref[0]
    qi_next = jnp.minimum(qi + 1, nq - 1)

    def plan_next():
        _attn_plan(tq, tk, nk, nq, bi, hi, qi_next, slope, kn_ref, qn_ref, plan_i_ref, plan_f_ref)

    row = lax.broadcasted_iota(jnp.int32, (2 * ATTN_HEAD_DIM, tq), 0)

    def masked_q(q_start):
        qf = qT_ref[:, pl.ds(q_start, tq)].astype(F32)
        return (jnp.where(row < ATTN_HEAD_DIM, qf, 0.0).astype(BF16),
                jnp.where(row >= ATTN_HEAD_DIM, qf, 0.0).astype(BF16))

    rhs = masked_q(q0)

    def finish_previous():
        _attn_epilogue(tq, jnp.maximum(qi - 1, 0), lq1_ref, lk1_ref, lq2_ref, lk2_ref, sg_ref, o_ref, acc_ref, l_ref)
        acc_ref[...] = jnp.zeros_like(acc_ref)

    def tile_bias(j):
        off = (i0 - j * tk).astype(F32)
        before, after = j < jd, j > jd
        eidx = jnp.where(before, 0, jnp.where(after, 1, 2 + dsel))
        cj = jnp.where(before, -slope * (off - float(tk - 1)), jnp.where(after, slope * (off + float(tq - 1)), 0.0))
        return eidx, cj

    @pl.when(score_bound <= FIXED_SHIFT_MAX_LOG2)
    def _():
        def stage_a(j, buf, w):
            kt = k_ref[pl.ds(pl.multiple_of(j * tk, tk), tk), :]
            for r in range(2):
                buf[r] = _dot(kt, w[r])

        @pl.when(first_scores_ready == 0)
        def _():
            stage_a(jl, s_ref, rhs)

        plan_next()
        jl_next = plan_i_ref[2]
        rhs_next = masked_q(pl.multiple_of(qi_next * tq, tq))

        def stage_b(j, buf, l):
            eidx, cj = tile_bias(j)
            half = jnp.exp2(jnp.full((1, tq), 0.5 * cj, F32))
            vt = vT_ref[:, pl.ds(pl.multiple_of(j * tk, tk), tk)]
            l_new = []
            for r in range(2):
                psum, pv = None, None
                for k0 in range(0, tk, PV_DEPTH):
                    p = jnp.exp2(buf[r, k0:k0 + PV_DEPTH, :] + e_ref[eidx, k0:k0 + PV_DEPTH, :])
                    ps = jnp.sum(p, axis=0, keepdims=True)
                    d = _dot(vt[:, k0:k0 + PV_DEPTH], p.astype(BF16))
                    psum, pv = (ps, d) if pv is None else (psum + ps, pv + d)
                l_new.append(l[r] + psum * half * half)
                acc_ref[r] += pv * half * half
            return tuple(l_new)

        def pair(j, l):
            stage_a(j + 1, s1_ref, rhs)
            l = stage_b(j, s_ref, l)
            done = j + 2 > jh
            w_next = tuple(jnp.where(done, rhs_next[r], rhs[r]) for r in range(2))
            stage_a(jnp.where(done, jl_next, j + 2), s_ref, w_next)
            return stage_b(j + 1, s1_ref, l)

        def body(t, l):
            j = jl + 2 + 4 * t
            return pair(j + 2, pair(j, l))

        finish_previous()
        zero = jnp.zeros((1, tq), F32)
        l = pair(jl, (zero, zero))
        rest = (jh - jl + 1) // 2 - 1
        l = lax.fori_loop(0, rest // 2, body, l)
        l_ref[0] = l[0]
        l_ref[1] = l[1]

        @pl.when(rest % 2 == 1)
        def _():
            l = pair(jh - 1, (l_ref[0], l_ref[1]))
            l_ref[0] = l[0]
            l_ref[1] = l[1]

        pre_ref[0] = 1

    def stage_a(j, m_old):
        eidx, cj = tile_bias(j)
        kt = k_ref[pl.ds(pl.multiple_of(j * tk, tk), tk), :]
        m_new, shift, alpha = [], [], []
        for r in range(2):
            s = _dot(kt, rhs[r]) + e_ref[eidx]
            s_ref[r] = s
            mr = jnp.maximum(m_old[r], jnp.max(s, axis=0, keepdims=True) + cj)
            m_new.append(mr)
            shift.append(mr - cj)
            alpha.append(jnp.exp2(m_old[r] - mr))
        return tuple(m_new), tuple(shift), tuple(alpha)

    def stage_b(j, shift, alpha, l_old):
        vt = vT_ref[:, pl.ds(pl.multiple_of(j * tk, tk), tk)]
        l_new = []
        for r in range(2):
            p = jnp.exp2(s_ref[r] - shift[r])
            l_new.append(alpha[r] * l_old[r] + jnp.sum(p, axis=0, keepdims=True))
            acc_ref[r] = alpha[r] * acc_ref[r] + _dot(vt, p.astype(BF16))
        return tuple(l_new)

    def body(t, carry):
        m, shift, alpha, l = carry
        j = jlo + t
        l = stage_b(jnp.maximum(j - 1, jlo), shift, alpha, l)
        m, shift, alpha = stage_a(jnp.minimum(j, jhi), m)
        return m, shift, alpha, l

    @pl.when(score_bound > FIXED_SHIFT_MAX_LOG2)
    def _():
        neg = jnp.full((1, tq), NEG_BIG, F32)
        big = jnp.full((1, tq), -NEG_BIG, F32)
        one = jnp.ones((1, tq), F32)
        zero = jnp.zeros((1, tq), F32)
        finish_previous()
        plan_next()
        pre_ref[0] = 0
        init = ((neg, neg), (big, big), (one, one), (zero, zero))
        _, _, _, l = lax.fori_loop(0, jhi - jlo + 2, body, init)
        l_ref[0] = l[0]
        l_ref[1] = l[1]


def _attn_epilogue(tq, qi, lq1_ref, lk1_ref, lq2_ref, lk2_ref, sg_ref, o_ref, acc_ref, l_ref):
    lam = (jnp.exp(jnp.sum(lq1_ref[...] * lk1_ref[...], axis=-1, keepdims=True))
           - jnp.exp(jnp.sum(lq2_ref[...] * lk2_ref[...], axis=-1, keepdims=True)) + LAM_INIT)
    o = acc_ref[0] / l_ref[0] - lam * (acc_ref[1] / l_ref[1])
    ms = jnp.mean(o * o, axis=0, keepdims=True)
    q0 = pl.multiple_of(qi * tq, tq)
    o_ref[pl.ds(q0, tq), :] = (o * lax.rsqrt(ms + RMS_EPS) * sg_ref[...] * (1.0 - LAM_INIT)).T


def _attn_kernel(tq, tk, nk, nq, slopes_ref, kn_ref, qn_ref, qT_ref, k_ref, vT_ref, lq1_ref, lk1_ref, lq2_ref,
                 lk2_ref, sg_ref, o_ref, e_ref, s_ref, s1_ref, acc_ref, l_ref, plan_i_ref, plan_f_ref, pre_ref):
    bi, hi = pl.program_id(0), pl.program_id(1)
    slope = slopes_ref[hi]
    ii = lax.broadcasted_iota(jnp.int32, (tk, tq), 1)
    jj = lax.broadcasted_iota(jnp.int32, (tk, tq), 0)
    rel = (ii - jj).astype(F32)
    e_ref[0] = -slope * (rel + float(tk - 1))
    e_ref[1] = slope * (rel - float(tq - 1))
    for d in range(tk // tq):
        e_ref[2 + d] = -slope * jnp.abs(rel + float(d * tq))
    s_ref[...] = jnp.zeros_like(s_ref)
    acc_ref[...] = jnp.zeros_like(acc_ref)
    l_ref[...] = jnp.ones_like(l_ref)
    _attn_plan(tq, tk, nk, nq, bi, hi, 0, slope, kn_ref, qn_ref, plan_i_ref, plan_f_ref)
    pre_ref[0] = 0

    def q_tile(qi, carry):
        _attn_q_tile(tq, tk, nk, nq, bi, hi, qi, slope, kn_ref, qn_ref, qT_ref, k_ref, vT_ref, lq1_ref, lk1_ref,
                     lq2_ref, lk2_ref, sg_ref, o_ref, e_ref, s_ref, s1_ref, acc_ref, l_ref, plan_i_ref, plan_f_ref,
                     pre_ref)
        return carry

    lax.fori_loop(0, nq, q_tile, 0)
    _attn_epilogue(tq, nq - 1, lq1_ref, lk1_ref, lq2_ref, lk2_ref, sg_ref, o_ref, acc_ref, l_ref)


def _diff_attn(qT, k3, vT, kn, qn, lq1, lk1, lq2, lk2, subln_g, tq=ATTN_TQ, tk=ATTN_TK):
    b, s, _ = k3.shape
    nq, nk = s // tq, s // tk
    ratio = tk // tq
    assert tk % tq == 0 and nk % 2 == 0
    hw = 2 * ATTN_HEAD_DIM
    slopes = jnp.exp2(-8.0 * (jnp.arange(ATTN_HEADS, dtype=F32) + 1.0) / ATTN_HEADS) * LOG2E
    kn_tab = jnp.transpose(kn[:, :, 0].reshape(b, nk, N_MAPS), (0, 2, 1)).reshape(-1)
    kparts = tk // ATTN_QPART
    qn_tab = jnp.transpose(qn[:, :, :kparts].reshape(b, nk, N_MAPS, kparts), (0, 2, 1, 3)).reshape(-1)
    vec = lambda a: a.reshape(1, -1)
    return pl.pallas_call(
        functools.partial(_attn_kernel, tq, tk, nk, nq),
        grid=(b, ATTN_HEADS),
        in_specs=[
            pl.BlockSpec(memory_space=pltpu.SMEM),
            pl.BlockSpec(memory_space=pltpu.SMEM),
            pl.BlockSpec(memory_space=pltpu.SMEM),
            pl.BlockSpec((hw, s), lambda bi, hi: (hi, bi)),
            pl.BlockSpec((None, s, hw), lambda bi, hi: (bi, 0, hi)),
            pl.BlockSpec((ATTN_V_DIM, s), lambda bi, hi: (hi, bi)),
            _const_spec((1, ATTN_HEAD_DIM)), _const_spec((1, ATTN_HEAD_DIM)),
            _const_spec((1, ATTN_HEAD_DIM)), _const_spec((1, ATTN_HEAD_DIM)),
            _const_spec((ATTN_V_DIM, 1)),
        ],
        out_specs=pl.BlockSpec((None, s, ATTN_V_DIM), lambda bi, hi: (bi, 0, hi)),
        out_shape=jax.ShapeDtypeStruct((b, s, ATTN_HEADS * ATTN_V_DIM), F32),
        scratch_shapes=[pltpu.VMEM((2 + tk // tq, tk, tq), F32),
                        pltpu.VMEM((2, tk, tq), F32), pltpu.VMEM((2, tk, tq), F32),
                        pltpu.VMEM((2, ATTN_V_DIM, tq), F32), pltpu.VMEM((2, 1, tq), F32),
                        pltpu.SMEM((4,), jnp.int32), pltpu.SMEM((1,), F32), pltpu.SMEM((1,), jnp.int32)],
        compiler_params=pltpu.CompilerParams(dimension_semantics=("parallel", "parallel"),
                                             vmem_limit_bytes=VMEM_LIMIT),
        name="diff_attn",
    )(slopes, kn_tab, qn_tab, qT, k3, vT, vec(lq1), vec(lk1), vec(lq2), vec(lk2), subln_g.reshape(-1, 1))


def _out_proj_kernel(x_ref, eg_ref, eb_ref, ys_ref, o_ref, gate_ref, w_ref, g_ref, b_ref, out_ref):
    h = _layer_norm(x_ref[...], eg_ref[...], eb_ref[...])
    ya = o_ref[...] * _silu(gate_ref[...])
    mix = _dot(ys_ref[...].astype(BF16), w_ref[:D_SSM, :]) + _dot(ya.astype(BF16), w_ref[D_SSM:, :])
    out_ref[...] = _layer_norm(ALPHA * h + mix, g_ref[...], b_ref[...])


def _out_proj(x2d, ln_emb_g, ln_emb_b, y_ssm, o, gate, w_out, ln_g, ln_b, tm=512):
    t = x2d.shape[0]
    row = lambda n: pl.BlockSpec((tm, n), lambda i: (i, 0))
    vec = lambda a: a.reshape(1, -1)
    return pl.pallas_call(
        _out_proj_kernel,
        grid=(t // tm,),
        in_specs=[row(D_MODEL), _const_spec((1, D_MODEL)), _const_spec((1, D_MODEL)),
                  row(D_SSM), row(D_ATTN), row(D_ATTN),
                  _const_spec((D_SSM + D_ATTN, D_MODEL)), _const_spec((1, D_MODEL)), _const_spec((1, D_MODEL))],
        out_specs=row(D_MODEL),
        out_shape=jax.ShapeDtypeStruct((t, D_MODEL), F32),
        compiler_params=pltpu.CompilerParams(dimension_semantics=("parallel",), vmem_limit_bytes=VMEM_LIMIT),
        name="out_proj",
    )(x2d, vec(ln_emb_g), vec(ln_emb_b), y_ssm, o, gate, w_out.astype(BF16), vec(ln_g), vec(ln_b))


def kernel(x, ln_emb_g, ln_emb_b, w_in, conv_w, conv_b, A_log_fwd, A_log_bwd, dt_bias_fwd, dt_bias_bwd, D_skip,
           ssm_norm_g, lambda_q1, lambda_k1, lambda_q2, lambda_k2, subln_g, w_out, ln_g, ln_b):
    b, s, _ = x.shape
    t = b * s
    x2d = x.reshape(t, D_MODEL)
    z, xbc, dtT, qT, k, vT, gate, kn, qn = _ln_inproj(x2d, ln_emb_g, ln_emb_b, w_in[0])

    xs, bt, cm = _conv_silu(xbc.reshape(b, s, D_XBC), conv_w[0], conv_b[0])
    y_fwd = _ssd_pass(False, xs, bt, cm, dtT, A_log_fwd[0], dt_bias_fwd[0], (D_skip[0],))
    y_ssm = _ssd_pass(True, xs, bt, cm, dtT, A_log_bwd[0], dt_bias_bwd[0],
                      (y_fwd, z.reshape(b, s, D_SSM), ssm_norm_g[0]))

    o = _diff_attn(qT, k.reshape(b, s, D_ATTN), vT, kn, qn, lambda_q1[0], lambda_k1[0], lambda_q2[0],
                   lambda_k2[0], subln_g[0]).reshape(t, D_ATTN)

    out = _out_proj(x2d, ln_emb_g, ln_emb_b, y_ssm.reshape(t, D_SSM), o, gate, w_out[0], ln_g[0], ln_b[0])
    return out.reshape(b, s, D_MODEL)
```

```python
import functools
import math

import jax
import jax.numpy as jnp
from jax import lax
from jax.experimental import pallas as pl
from jax.experimental.pallas import tpu as pltpu

D_MODEL = 1024
D_SSM = 1024
SSM_HEAD_DIM = 64
SSM_HEADS = 16
SSM_GROUPS = 2
HEADS_PER_GROUP = SSM_HEADS // SSM_GROUPS
D_STATE = 128
D_CONV = 5
CHUNK = 128
D_XBC = D_SSM + 2 * SSM_GROUPS * D_STATE
D_ATTN = 1024
ATTN_HEADS = 8
ATTN_HEAD_DIM = 64
ATTN_V_DIM = 128
GROUP_WIDTH = D_SSM // SSM_GROUPS
DEPTH = 1
ALPHA = (2.0 * DEPTH) ** 0.25
LN_EPS = 1e-5
RMS_EPS = 1e-5
LAM_INIT = 0.8 - 0.6 * math.exp(-0.3 * 0)

LANES = 128
VMEM_LIMIT = 48 * 1024 * 1024

F32 = jnp.float32
BF16 = jnp.bfloat16


LOG2E = math.log2(math.e)


def _dot(a, b):
    return jnp.dot(a, b, preferred_element_type=F32)


def _dot_nt(a, b):
    return lax.dot_general(a, b, (((1,), (1,)), ((), ())), preferred_element_type=F32)


def _layer_norm(x, g, b):
    mu = jnp.mean(x, axis=-1, keepdims=True)
    xc = x - mu
    var = jnp.mean(xc * xc, axis=-1, keepdims=True)
    return xc * lax.rsqrt(var + LN_EPS) * g + b


def _silu(x):
    return x / (1.0 + jnp.exp(-x))


def _softplus(x):
    return jnp.maximum(x, 0.0) + jnp.log1p(jnp.exp(-jnp.abs(x)))


def _split3(x):
    hi = x.astype(BF16)
    r = x - hi.astype(F32)
    mid = r.astype(BF16)
    lo = (r - mid.astype(F32)).astype(BF16)
    return hi, mid, lo


def _dot_exact_lhs(x, m):
    hi, mid, lo = _split3(x)
    return _dot(hi, m) + _dot(mid, m) + _dot(lo, m)


def _const_spec(shape):
    nd = len(shape)
    return pl.BlockSpec(shape, lambda *_: (0,) * nd)


N_DT_TILES = 2 * SSM_GROUPS


ATTN_TQ = 256
ATTN_QPART = 256
ATTN_TK = 512
N_MAPS = 2 * ATTN_HEADS
NORM_SLACK = 1.01


def _ln_inproj_kernel(x_ref, g_ref, b_ref, wz_ref, wxbc_ref, wdtT_ref, wqT_ref, wk_ref, wvT_ref, wg_ref,
                      grpT_ref,
                      z_ref, xbc_ref, dtT_ref, qT_ref, k_ref, vT_ref, gate_ref, kn_ref, qn_ref):
    h = _layer_norm(x_ref[...], g_ref[...], b_ref[...]).astype(BF16)
    z_ref[...] = _dot(h, wz_ref[...])
    xbc_ref[...] = _dot(h, wxbc_ref[...])
    dtT_ref[...] = _dot_nt(wdtT_ref[...], h).reshape(dtT_ref.shape)
    qT = _dot_nt(wqT_ref[...], h) * (LOG2E * ATTN_HEAD_DIM ** -0.5)
    qT_ref[...] = qT.astype(BF16)
    k = _dot(h, wk_ref[...])
    k_ref[...] = k.astype(BF16)
    vT_ref[...] = _dot_nt(wvT_ref[...], h).astype(BF16)
    gate_ref[...] = _dot(h, wg_ref[...])
    lane = lax.broadcasted_iota(jnp.int32, (N_MAPS, LANES), 1)
    k2 = _dot_nt(grpT_ref[...], (k * k).astype(BF16))
    kn = jnp.sqrt(jnp.max(k2, axis=1, keepdims=True)) * NORM_SLACK
    kn_ref[...] = jnp.where(lane == 0, kn, 0.0)
    q2 = _dot(grpT_ref[...], (qT * qT).astype(BF16))
    qn = jnp.zeros((N_MAPS, LANES), F32)
    for part in range(q2.shape[1] // ATTN_QPART):
        pm = jnp.max(q2[:, part * ATTN_QPART:(part + 1) * ATTN_QPART], axis=1, keepdims=True)
        qn = jnp.where(lane == part, jnp.sqrt(pm) * NORM_SLACK, qn)
    qn_ref[...] = qn


def _ln_inproj(x2d, ln_g, ln_b, w_in, tm=ATTN_TK):
    t = x2d.shape[0]
    grp_t = (jnp.arange(N_MAPS)[:, None] == jnp.arange(D_ATTN)[None, :] // ATTN_HEAD_DIM).astype(BF16)
    offs = [0, D_SSM, D_SSM + D_XBC, D_SSM + D_XBC + 2 * SSM_HEADS]
    wz = w_in[:, offs[0]:offs[1]].astype(BF16)
    wxbc = w_in[:, offs[1]:offs[2]].astype(BF16)
    wdt = w_in[:, offs[2]:offs[3]].astype(BF16)
    o = offs[3]
    wq, wk, wv, wg = (w_in[:, o + i * D_ATTN:o + (i + 1) * D_ATTN].astype(BF16) for i in range(4))
    row = lambda n: pl.BlockSpec((tm, n), lambda i: (i, 0))
    col = lambda n: pl.BlockSpec((n, tm), lambda i: (0, i))
    weights = [wz, wxbc, wdt.T, wq.T, wk, wv.T, wg, grp_t]
    nt = t // tm
    return pl.pallas_call(
        _ln_inproj_kernel,
        grid=(nt,),
        in_specs=[row(D_MODEL), _const_spec((1, D_MODEL)), _const_spec((1, D_MODEL))]
                 + [pl.BlockSpec(w.shape, lambda i: (0, 0), pipeline_mode=pl.Buffered(1)) for w in weights],
        out_specs=[row(D_SSM), row(D_XBC),
                   pl.BlockSpec((N_DT_TILES, HEADS_PER_GROUP, tm), lambda i: (0, 0, i)),
                   col(D_ATTN), row(D_ATTN), col(D_ATTN), row(D_ATTN),
                   pl.BlockSpec((None, N_MAPS, LANES), lambda i: (i, 0, 0)),
                   pl.BlockSpec((None, N_MAPS, LANES), lambda i: (i, 0, 0))],
        out_shape=[jax.ShapeDtypeStruct((t, D_SSM), F32), jax.ShapeDtypeStruct((t, D_XBC), F32),
                   jax.ShapeDtypeStruct((N_DT_TILES, HEADS_PER_GROUP, t), F32),
                   jax.ShapeDtypeStruct((D_ATTN, t), BF16), jax.ShapeDtypeStruct((t, D_ATTN), BF16),
                   jax.ShapeDtypeStruct((D_ATTN, t), BF16), jax.ShapeDtypeStruct((t, D_ATTN), F32),
                   jax.ShapeDtypeStruct((nt, N_MAPS, LANES), F32), jax.ShapeDtypeStruct((nt, N_MAPS, LANES), F32)],
        compiler_params=pltpu.CompilerParams(dimension_semantics=("parallel",), vmem_limit_bytes=VMEM_LIMIT),
        name="ln_inproj",
    )(x2d, ln_g.reshape(1, -1), ln_b.reshape(1, -1), *weights)


HALO = 8


CONV_ROWS = 128


def _conv_silu_kernel(prev_ref, cur_ref, next_ref, w_ref, b_ref, xs_ref, bt_ref, c_ref, ext_ref):
    i = pl.program_id(1)
    n = pl.num_programs(1)
    tc = cur_ref.shape[0]
    ext_ref[0:HALO, :] = jnp.where(i == 0, 0.0, prev_ref[...])
    ext_ref[HALO:HALO + tc, :] = cur_ref[...]
    ext_ref[HALO + tc:, :] = jnp.where(i == n - 1, 0.0, next_ref[...])
    for cb in range(D_XBC // LANES):
        cols = slice(cb * LANES, (cb + 1) * LANES)
        for r0 in range(0, tc, CONV_ROWS):
            acc = b_ref[:, cols]
            for kk in range(D_CONV):
                acc = acc + ext_ref[pl.ds(r0 + HALO - D_CONV // 2 + kk, CONV_ROWS), cols] * w_ref[kk:kk + 1, cols]
            act = _silu(acc)
            rows = slice(r0, r0 + CONV_ROWS)
            if cb < D_SSM // LANES:
                xs_ref[rows, cols] = act
            elif cb < D_SSM // LANES + SSM_GROUPS:
                bt_ref[cb - D_SSM // LANES, :, rows] = act.T.astype(BF16)
            else:
                c_ref[rows, (cb - D_SSM // LANES - SSM_GROUPS) * LANES:(cb - D_SSM // LANES - SSM_GROUPS + 1) * LANES] = (
                    act.astype(BF16))


def _conv_silu(xbc, conv_w, conv_b, tc=512):
    b, s, _ = xbc.shape
    nb = s // tc
    hb = tc // HALO
    return pl.pallas_call(
        _conv_silu_kernel,
        grid=(b, nb),
        in_specs=[
            pl.BlockSpec((None, HALO, D_XBC), lambda bi, i: (bi, jnp.maximum(i * hb - 1, 0), 0)),
            pl.BlockSpec((None, tc, D_XBC), lambda bi, i: (bi, i, 0)),
            pl.BlockSpec((None, HALO, D_XBC), lambda bi, i: (bi, jnp.minimum((i + 1) * hb, s // HALO - 1), 0)),
            _const_spec((D_CONV, D_XBC)),
            _const_spec((1, D_XBC)),
        ],
        out_specs=[
            pl.BlockSpec((None, tc, D_SSM), lambda bi, i: (bi, i, 0)),
            pl.BlockSpec((None, SSM_GROUPS, D_STATE, tc), lambda bi, i: (bi, 0, 0, i)),
            pl.BlockSpec((None, tc, SSM_GROUPS * D_STATE), lambda bi, i: (bi, i, 0)),
        ],
        out_shape=[
            jax.ShapeDtypeStruct((b, s, D_SSM), F32),
            jax.ShapeDtypeStruct((b, SSM_GROUPS, D_STATE, s), BF16),
            jax.ShapeDtypeStruct((b, s, SSM_GROUPS * D_STATE), BF16),
        ],
        scratch_shapes=[pltpu.VMEM((tc + 2 * HALO, D_XBC), F32)],
        compiler_params=pltpu.CompilerParams(dimension_semantics=("parallel", "parallel"),
                                             vmem_limit_bytes=VMEM_LIMIT),
        name="conv_silu",
    )(xbc, xbc, xbc, conv_w, conv_b.reshape(1, -1))


SSD_CHUNKS_IN_FLIGHT = 4


def _ssd_bodies(rev, xs, bts, cms, dtTs, alogs, biases, groups, states):
    L = CHUNK
    r = lax.broadcasted_iota(jnp.int32, (L, L), 0)
    c = lax.broadcasted_iota(jnp.int32, (L, L), 1)
    keep = (c >= r) if rev else (c <= r)
    cum_r = ((r >= c) if rev else (r <= c)).astype(BF16)
    hg = HEADS_PER_GROUP
    eh = lax.broadcasted_iota(jnp.int32, (6 * hg, 2 * GROUP_WIDTH), 0)
    el = lax.broadcasted_iota(jnp.int32, (6 * hg, 2 * GROUP_WIDTH), 1)
    spread = ((el // GROUP_WIDTH == eh // (3 * hg)) & ((el % GROUP_WIDTH) // SSM_HEAD_DIM == eh % hg)).astype(BF16)
    lane = lax.broadcasted_iota(jnp.int32, (L, LANES), 1)
    edge = 0 if rev else L - 1
    gs = range(len(xs))

    dt_t = [_softplus(dtTs[g] + biases[g]) for g in gs]
    acs_t = [_dot_exact_lhs(dt_t[g] * (-jnp.exp(alogs[g])), cum_r) for g in gs]
    pieces_t = [jnp.concatenate([p.astype(F32) for p in _split3(dt_t[g]) + _split3(acs_t[g])], axis=0) for g in gs]
    both_x = [_dot(pieces_t[g].T.astype(BF16), spread) for g in gs]
    acs = [acs_t[g].T for g in gs]
    cb = [_dot(cms[g], bts[g]) for g in gs]
    dt_x = [both_x[g][:, :GROUP_WIDTH] for g in gs]
    acs_x = [both_x[g][:, GROUP_WIDTH:] for g in gs]
    tot_x = [acs_x[g][edge:edge + 1, :] for g in gs]
    xdt = [xs[g] * dt_x[g] for g in gs]

    y_diag = []
    for g in gs:
        tiles = []
        for t in range(GROUP_WIDTH // LANES):
            xt = xdt[g][:, t * LANES:(t + 1) * LANES]
            gmats, xparts = [], []
            for half in range(LANES // SSM_HEAD_DIM):
                j = t * (LANES // SSM_HEAD_DIM) + half
                seg = acs[g][:, j:j + 1] - acs_t[g][j:j + 1, :]
                gmats.append((cb[g] * jnp.exp(jnp.where(keep, seg, -jnp.inf))).astype(BF16))
                in_head = (lane >= half * SSM_HEAD_DIM) & (lane < (half + 1) * SSM_HEAD_DIM)
                xparts.append(jnp.where(in_head, xt, 0.0).astype(BF16))
            tiles.append(_dot(jnp.concatenate(gmats, axis=1), jnp.concatenate(xparts, axis=0)))
        y_diag.append(jnp.concatenate(tiles, axis=1))

    states = list(states)
    ys = []
    for g in gs:
        grp = groups[g]
        y_off = _dot(cms[g], states[grp].astype(BF16)) * jnp.exp(acs_x[g])
        ys.append(y_off + y_diag[g])
        w = (xdt[g] * jnp.exp(tot_x[g] - acs_x[g])).astype(BF16)
        states[grp] = jnp.exp(tot_x[g]) * states[grp] + _dot(bts[g], w)
    return ys, states


def _ssd_kernel(rev, nchunk, *refs):
    if rev:
        (xs_ref, bt_ref, c_ref, dtT_ref, alc_ref, bc_ref, yf_ref, z_ref, ng_ref, y_ref, state_ref) = refs
    else:
        (xs_ref, bt_ref, c_ref, dtT_ref, alc_ref, bc_ref, dx_ref, y_ref, state_ref) = refs

    @pl.when(pl.program_id(1) == 0)
    def _():
        state_ref[...] = jnp.zeros_like(state_ref)

    gw = GROUP_WIDTH
    gs = range(SSM_GROUPS)
    cols = [slice(g * gw, (g + 1) * gw) for g in gs]
    order = list(range(nchunk - 1, -1, -1) if rev else range(nchunk))
    states = [state_ref[g] for g in gs]
    for c0 in range(0, nchunk, SSD_CHUNKS_IN_FLIGHT):
        bodies = [(slice(ci * CHUNK, (ci + 1) * CHUNK), g) for ci in order[c0:c0 + SSD_CHUNKS_IN_FLIGHT] for g in gs]
        x = [xs_ref[rows, cols[g]] for rows, g in bodies]
        ys, states = _ssd_bodies(rev, x, [bt_ref[g, :, rows] for rows, g in bodies],
                                 [c_ref[rows, g * D_STATE:(g + 1) * D_STATE] for rows, g in bodies],
                                 [dtT_ref[g, :, rows] for rows, g in bodies],
                                 [alc_ref[g] for _, g in bodies], [bc_ref[g] for _, g in bodies],
                                 [g for _, g in bodies], states)
        for (rows, g), xb, y in zip(bodies, x, ys):
            if rev:
                yy = (yf_ref[rows, cols[g]] + y) * _silu(z_ref[rows, cols[g]])
                ms = jnp.mean(yy * yy, axis=-1, keepdims=True)
                y_ref[rows, cols[g]] = (yy * lax.rsqrt(ms + RMS_EPS) * ng_ref[g]).astype(BF16)
            else:
                y_ref[rows, cols[g]] = y + dx_ref[g] * xb
    for g in gs:
        state_ref[g] = states[g]


def _ssd_pass(rev, xs, bt, cm, dtT, a_log, dt_bias, extra, nchunk=4):
    b, s, _ = xs.shape
    tcs = nchunk * CHUNK
    nblk = s // tcs
    blk = (lambda i: nblk - 1 - i) if rev else (lambda i: i)
    dirn = 1 if rev else 0
    gw, hg, ng = GROUP_WIDTH, HEADS_PER_GROUP, SSM_GROUPS
    seq_spec = lambda w: pl.BlockSpec((None, tcs, w), lambda bi, i: (bi, blk(i), 0))
    in_specs = [
        seq_spec(D_SSM),
        pl.BlockSpec((None, ng, D_STATE, tcs), lambda bi, i: (bi, 0, 0, blk(i))),
        seq_spec(ng * D_STATE),
        pl.BlockSpec((ng, hg, tcs), lambda bi, i: (dirn, 0, bi * nblk + blk(i))),
        _const_spec((ng, hg, 1)), _const_spec((ng, hg, 1)),
    ]
    args = [xs, bt, cm, dtT, a_log.reshape(ng, hg, 1), dt_bias.reshape(ng, hg, 1)]
    if rev:
        y_fwd, z, norm_g = extra
        in_specs += [seq_spec(D_SSM), seq_spec(D_SSM), _const_spec((ng, 1, gw))]
        args += [y_fwd, z, norm_g.reshape(ng, 1, gw)]
    else:
        (d_skip,) = extra
        in_specs += [_const_spec((ng, 1, gw))]
        args += [jnp.repeat(d_skip, SSM_HEAD_DIM).reshape(ng, 1, gw)]
    return pl.pallas_call(
        functools.partial(_ssd_kernel, rev, nchunk),
        grid=(b, nblk),
        in_specs=in_specs,
        out_specs=seq_spec(D_SSM),
        out_shape=jax.ShapeDtypeStruct((b, s, D_SSM), BF16 if rev else F32),
        scratch_shapes=[pltpu.VMEM((ng, D_STATE, gw), F32)],
        compiler_params=pltpu.CompilerParams(dimension_semantics=("parallel", "arbitrary"),
                                             vmem_limit_bytes=VMEM_LIMIT),
        name="ssd_bwd" if rev else "ssd_fwd",
    )(*args)


NEG_BIG = -1e30


SKIP_LOG2 = 80.0
FIXED_SHIFT_MAX_LOG2 = 60.0
PV_DEPTH = 256


def _attn_plan(tq, tk, nk, nq, bi, hi, qi, slope, kn_ref, qn_ref, plan_i_ref, plan_f_ref):
    i0 = qi * tq
    jd = qi // (tk // tq)
    parts = tq // ATTN_QPART
    qns, bases = [], []
    for r in range(2):
        base = (bi * ATTN_HEADS + hi) * 2 + r
        qn = qn_ref[base * (nq * parts) + qi * parts]
        for part in range(1, parts):
            qn = jnp.maximum(qn, qn_ref[base * (nq * parts) + qi * parts + part])
        qns.append(qn)
        bases.append(base * nk)
    jlo, jhi = jd, jd
    kmax = [kn_ref[bases[0]], kn_ref[bases[1]]]
    for jt in range(nk):
        dist = jnp.maximum(jnp.maximum(i0 - (jt * tk + tk - 1), jt * tk - (i0 + tq - 1)), 0).astype(F32)
        need = None
        for r in range(2):
            kn = kn_ref[bases[r] + jt]
            kmax[r] = jnp.maximum(kmax[r], kn)
            reach = SKIP_LOG2 + qns[r] * (kn + kn_ref[bases[r] + jd])
            need_r = slope * dist <= reach
            need = need_r if need is None else (need | need_r)
        jlo = jnp.where(need, jnp.minimum(jlo, jt), jlo)
        jhi = jnp.where(need, jnp.maximum(jhi, jt), jhi)
    plan_i_ref[0] = jlo
    plan_i_ref[1] = jhi
    odd = (jhi - jlo + 1) % 2
    room_above = (jhi < nk - 1).astype(jnp.int32)
    plan_i_ref[2] = jlo - odd * (1 - room_above)
    plan_i_ref[3] = jhi + odd * room_above
    plan_f_ref[0] = jnp.maximum(qns[0] * kmax[0], qns[1] * kmax[1])


def _attn_q_tile(tq, tk, nk, nq, bi, hi, qi, slope, kn_ref, qn_ref, qT_ref, k_ref, vT_ref, lq1_ref, lk1_ref, lq2_ref,
                 lk2_ref, sg_ref, gate_ref, o_ref, e_ref, s_ref, s1_ref, acc_ref, l_ref, plan_i_ref, plan_f_ref, pre_ref):
    ratio = tk // tq
    i0 = qi * tq
    q0 = pl.multiple_of(i0, tq)
    jd = qi // ratio
    dsel = qi % ratio
    jlo, jhi, jl, jh, score_bound = plan_i_ref[0], plan_i_ref[1], plan_i_ref[2], plan_i_ref[3], plan_f_ref[0]
    first_scores_ready = pre_ref[0]
    qi_next = jnp.minimum(qi + 1, nq - 1)

    def plan_next():
        _attn_plan(tq, tk, nk, nq, bi, hi, qi_next, slope, kn_ref, qn_ref, plan_i_ref, plan_f_ref)

    row = lax.broadcasted_iota(jnp.int32, (2 * ATTN_HEAD_DIM, tq), 0)

    def masked_q(q_start):
        qf = qT_ref[:, pl.ds(q_start, tq)].astype(F32)
        return (jnp.where(row < ATTN_HEAD_DIM, qf, 0.0).astype(BF16),
                jnp.where(row >= ATTN_HEAD_DIM, qf, 0.0).astype(BF16))

    rhs = masked_q(q0)

    def finish_previous():
        _attn_epilogue(tq, jnp.maximum(qi - 1, 0), lq1_ref, lk1_ref, lq2_ref, lk2_ref, sg_ref, gate_ref, o_ref, acc_ref, l_ref)
        acc_ref[...] = jnp.zeros_like(acc_ref)

    def tile_bias(j):
        off = (i0 - j * tk).astype(F32)
        before, after = j < jd, j > jd
        eidx = jnp.where(before, 0, jnp.where(after, 1, 2 + dsel))
        cj = jnp.where(before, -slope * (off - float(tk - 1)), jnp.where(after, slope * (off + float(tq - 1)), 0.0))
        return eidx, cj

    @pl.when(score_bound <= FIXED_SHIFT_MAX_LOG2)
    def _():
        def stage_a(j, buf, w):
            kt = k_ref[pl.ds(pl.multiple_of(j * tk, tk), tk), :]
            for r in range(2):
                buf[r] = _dot(kt, w[r])

        @pl.when(first_scores_ready == 0)
        def _():
            stage_a(jl, s_ref, rhs)

        plan_next()
        jl_next = plan_i_ref[2]
        rhs_next = masked_q(pl.multiple_of(qi_next * tq, tq))

        def stage_b(j, buf, l):
            eidx, cj = tile_bias(j)
            half = jnp.exp2(jnp.full((1, tq), 0.5 * cj, F32))
            vt = vT_ref[:, pl.ds(pl.multiple_of(j * tk, tk), tk)]
            l_new = []
            for r in range(2):
                psum, pv = None, None
                for k0 in range(0, tk, PV_DEPTH):
                    p = jnp.exp2(buf[r, k0:k0 + PV_DEPTH, :] + e_ref[eidx, k0:k0 + PV_DEPTH, :])
                    ps = jnp.sum(p, axis=0, keepdims=True)
                    d = _dot(vt[:, k0:k0 + PV_DEPTH], p.astype(BF16))
                    psum, pv = (ps, d) if pv is None else (psum + ps, pv + d)
                l_new.append(l[r] + psum * half * half)
                acc_ref[r] += pv * half * half
            return tuple(l_new)

        def pair(j, l):
            stage_a(j + 1, s1_ref, rhs)
            l = stage_b(j, s_ref, l)
            done = j + 2 > jh
            w_next = tuple(jnp.where(done, rhs_next[r], rhs[r]) for r in range(2))
            stage_a(jnp.where(done, jl_next, j + 2), s_ref, w_next)
            return stage_b(j + 1, s1_ref, l)

        def body(t, l):
            j = jl + 2 + 4 * t
            return pair(j + 2, pair(j, l))

        finish_previous()
        zero = jnp.zeros((1, tq), F32)
        l = pair(jl, (zero, zero))
        rest = (jh - jl + 1) // 2 - 1
        l = lax.fori_loop(0, rest // 2, body, l)
        l_ref[0] = l[0]
        l_ref[1] = l[1]

        @pl.when(rest % 2 == 1)
        def _():
            l = pair(jh - 1, (l_ref[0], l_ref[1]))
            l_ref[0] = l[0]
            l_ref[1] = l[1]

        pre_ref[0] = 1

    def stage_a(j, m_old):
        eidx, cj = tile_bias(j)
        kt = k_ref[pl.ds(pl.multiple_of(j * tk, tk), tk), :]
        m_new, shift, alpha = [], [], []
        for r in range(2):
            s = _dot(kt, rhs[r]) + e_ref[eidx]
            s_ref[r] = s
            mr = jnp.maximum(m_old[r], jnp.max(s, axis=0, keepdims=True) + cj)
            m_new.append(mr)
            shift.append(mr - cj)
            alpha.append(jnp.exp2(m_old[r] - mr))
        return tuple(m_new), tuple(shift), tuple(alpha)

    def stage_b(j, shift, alpha, l_old):
        vt = vT_ref[:, pl.ds(pl.multiple_of(j * tk, tk), tk)]
        l_new = []
        for r in range(2):
            p = jnp.exp2(s_ref[r] - shift[r])
            l_new.append(alpha[r] * l_old[r] + jnp.sum(p, axis=0, keepdims=True))
            acc_ref[r] = alpha[r] * acc_ref[r] + _dot(vt, p.astype(BF16))
        return tuple(l_new)

    def body(t, carry):
        m, shift, alpha, l = carry
        j = jlo + t
        l = stage_b(jnp.maximum(j - 1, jlo), shift, alpha, l)
        m, shift, alpha = stage_a(jnp.minimum(j, jhi), m)
        return m, shift, alpha, l

    @pl.when(score_bound > FIXED_SHIFT_MAX_LOG2)
    def _():
        neg = jnp.full((1, tq), NEG_BIG, F32)
        big = jnp.full((1, tq), -NEG_BIG, F32)
        one = jnp.ones((1, tq), F32)
        zero = jnp.zeros((1, tq), F32)
        finish_previous()
        plan_next()
        pre_ref[0] = 0
        init = ((neg, neg), (big, big), (one, one), (zero, zero))
        _, _, _, l = lax.fori_loop(0, jhi - jlo + 2, body, init)
        l_ref[0] = l[0]
        l_ref[1] = l[1]


def _attn_epilogue(tq, qi, lq1_ref, lk1_ref, lq2_ref, lk2_ref, sg_ref, gate_ref, o_ref, acc_ref, l_ref):
    lam = (jnp.exp(jnp.sum(lq1_ref[...] * lk1_ref[...], axis=-1, keepdims=True))
           - jnp.exp(jnp.sum(lq2_ref[...] * lk2_ref[...], axis=-1, keepdims=True)) + LAM_INIT)
    o = acc_ref[0] / l_ref[0] - lam * (acc_ref[1] / l_ref[1])
    ms = jnp.mean(o * o, axis=0, keepdims=True)
    q0 = pl.multiple_of(qi * tq, tq)
    on = (o * lax.rsqrt(ms + RMS_EPS) * sg_ref[...] * (1.0 - LAM_INIT)).T
    o_ref[pl.ds(q0, tq), :] = (on * _silu(gate_ref[pl.ds(q0, tq), :])).astype(BF16)


def _attn_kernel(tq, tk, nk, nq, slopes_ref, kn_ref, qn_ref, qT_ref, k_ref, vT_ref, lq1_ref, lk1_ref, lq2_ref,
                 lk2_ref, sg_ref, gate_ref, o_ref, e_ref, s_ref, s1_ref, acc_ref, l_ref, plan_i_ref, plan_f_ref, pre_ref):
    bi, hi = pl.program_id(0), pl.program_id(1)
    slope = slopes_ref[hi]
    ii = lax.broadcasted_iota(jnp.int32, (tk, tq), 1)
    jj = lax.broadcasted_iota(jnp.int32, (tk, tq), 0)
    rel = (ii - jj).astype(F32)
    e_ref[0] = -slope * (rel + float(tk - 1))
    e_ref[1] = slope * (rel - float(tq - 1))
    for d in range(tk // tq):
        e_ref[2 + d] = -slope * jnp.abs(rel + float(d * tq))
    s_ref[...] = jnp.zeros_like(s_ref)
    acc_ref[...] = jnp.zeros_like(acc_ref)
    l_ref[...] = jnp.ones_like(l_ref)
    _attn_plan(tq, tk, nk, nq, bi, hi, 0, slope, kn_ref, qn_ref, plan_i_ref, plan_f_ref)
    pre_ref[0] = 0

    def q_tile(qi, carry):
        _attn_q_tile(tq, tk, nk, nq, bi, hi, qi, slope, kn_ref, qn_ref, qT_ref, k_ref, vT_ref, lq1_ref, lk1_ref,
                     lq2_ref, lk2_ref, sg_ref, gate_ref, o_ref, e_ref, s_ref, s1_ref, acc_ref, l_ref, plan_i_ref, plan_f_ref,
                     pre_ref)
        return carry

    lax.fori_loop(0, nq, q_tile, 0)
    _attn_epilogue(tq, nq - 1, lq1_ref, lk1_ref, lq2_ref, lk2_ref, sg_ref, gate_ref, o_ref, acc_ref, l_ref)


def _diff_attn(qT, k3, vT, kn, qn, lq1, lk1, lq2, lk2, subln_g, gate3, tq=ATTN_TQ, tk=ATTN_TK):
    b, s, _ = k3.shape
    nq, nk = s // tq, s // tk
    ratio = tk // tq
    assert tk % tq == 0 and nk % 2 == 0
    hw = 2 * ATTN_HEAD_DIM
    slopes = jnp.exp2(-8.0 * (jnp.arange(ATTN_HEADS, dtype=F32) + 1.0) / ATTN_HEADS) * LOG2E
    kn_tab = jnp.transpose(kn[:, :, 0].reshape(b, nk, N_MAPS), (0, 2, 1)).reshape(-1)
    kparts = tk // ATTN_QPART
    qn_tab = jnp.transpose(qn[:, :, :kparts].reshape(b, nk, N_MAPS, kparts), (0, 2, 1, 3)).reshape(-1)
    vec = lambda a: a.reshape(1, -1)
    return pl.pallas_call(
        functools.partial(_attn_kernel, tq, tk, nk, nq),
        grid=(b, ATTN_HEADS),
        in_specs=[
            pl.BlockSpec(memory_space=pltpu.SMEM),
            pl.BlockSpec(memory_space=pltpu.SMEM),
            pl.BlockSpec(memory_space=pltpu.SMEM),
            pl.BlockSpec((hw, s), lambda bi, hi: (hi, bi)),
            pl.BlockSpec((None, s, hw), lambda bi, hi: (bi, 0, hi)),
            pl.BlockSpec((ATTN_V_DIM, s), lambda bi, hi: (hi, bi)),
            _const_spec((1, ATTN_HEAD_DIM)), _const_spec((1, ATTN_HEAD_DIM)),
            _const_spec((1, ATTN_HEAD_DIM)), _const_spec((1, ATTN_HEAD_DIM)),
            _const_spec((ATTN_V_DIM, 1)),
            pl.BlockSpec((None, s, ATTN_V_DIM), lambda bi, hi: (bi, 0, hi)),
        ],
        out_specs=pl.BlockSpec((None, s, ATTN_V_DIM), lambda bi, hi: (bi, 0, hi)),
        out_shape=jax.ShapeDtypeStruct((b, s, ATTN_HEADS * ATTN_V_DIM), BF16),
        scratch_shapes=[pltpu.VMEM((2 + tk // tq, tk, tq), F32),
                        pltpu.VMEM((2, tk, tq), F32), pltpu.VMEM((2, tk, tq), F32),
                        pltpu.VMEM((2, ATTN_V_DIM, tq), F32), pltpu.VMEM((2, 1, tq), F32),
                        pltpu.SMEM((4,), jnp.int32), pltpu.SMEM((1,), F32), pltpu.SMEM((1,), jnp.int32)],
        compiler_params=pltpu.CompilerParams(dimension_semantics=("parallel", "parallel"),
                                             vmem_limit_bytes=VMEM_LIMIT),
        name="diff_attn",
    )(slopes, kn_tab, qn_tab, qT, k3, vT, vec(lq1), vec(lk1), vec(lq2), vec(lk2), subln_g.reshape(-1, 1), gate3)


def _out_proj_kernel(x_ref, eg_ref, eb_ref, ys_ref, ya_ref, w_ref, g_ref, b_ref, out_ref):
    h = _layer_norm(x_ref[...], eg_ref[...], eb_ref[...])
    mix = _dot(ys_ref[...], w_ref[:D_SSM, :]) + _dot(ya_ref[...], w_ref[D_SSM:, :])
    out_ref[...] = _layer_norm(ALPHA * h + mix, g_ref[...], b_ref[...])


def _out_proj(x2d, ln_emb_g, ln_emb_b, y_ssm, y_attn, w_out, ln_g, ln_b, tm=512):
    t = x2d.shape[0]
    row = lambda n: pl.BlockSpec((tm, n), lambda i: (i, 0))
    vec = lambda a: a.reshape(1, -1)
    return pl.pallas_call(
        _out_proj_kernel,
        grid=(t // tm,),
        in_specs=[row(D_MODEL), _const_spec((1, D_MODEL)), _const_spec((1, D_MODEL)),
                  row(D_SSM), row(D_ATTN),
                  _const_spec((D_SSM + D_ATTN, D_MODEL)), _const_spec((1, D_MODEL)), _const_spec((1, D_MODEL))],
        out_specs=row(D_MODEL),
        out_shape=jax.ShapeDtypeStruct((t, D_MODEL), F32),
        compiler_params=pltpu.CompilerParams(dimension_semantics=("parallel",), vmem_limit_bytes=VMEM_LIMIT),
        name="out_proj",
    )(x2d, vec(ln_emb_g), vec(ln_emb_b), y_ssm, y_attn, w_out.astype(BF16), vec(ln_g), vec(ln_b))


def kernel(x, ln_emb_g, ln_emb_b, w_in, conv_w, conv_b, A_log_fwd, A_log_bwd, dt_bias_fwd, dt_bias_bwd, D_skip,
           ssm_norm_g, lambda_q1, lambda_k1, lambda_q2, lambda_k2, subln_g, w_out, ln_g, ln_b):
    b, s, _ = x.shape
    t = b * s
    x2d = x.reshape(t, D_MODEL)
    z, xbc, dtT, qT, k, vT, gate, kn, qn = _ln_inproj(x2d, ln_emb_g, ln_emb_b, w_in[0])

    xs, bt, cm = _conv_silu(xbc.reshape(b, s, D_XBC), conv_w[0], conv_b[0])
    y_fwd = _ssd_pass(False, xs, bt, cm, dtT, A_log_fwd[0], dt_bias_fwd[0], (D_skip[0],))
    y_ssm = _ssd_pass(True, xs, bt, cm, dtT, A_log_bwd[0], dt_bias_bwd[0],
                      (y_fwd, z.reshape(b, s, D_SSM), ssm_norm_g[0]))

    y_attn = _diff_attn(qT, k.reshape(b, s, D_ATTN), vT, kn, qn, lambda_q1[0], lambda_k1[0], lambda_q2[0],
                        lambda_k2[0], subln_g[0], gate.reshape(b, s, D_ATTN)).reshape(t, D_ATTN)

    out = _out_proj(x2d, ln_emb_g, ln_emb_b, y_ssm.reshape(t, D_SSM), y_attn, w_out[0], ln_g[0], ln_b[0])
    return out.reshape(b, s, D_MODEL)
```

```python
import functools
import math

import jax
import jax.numpy as jnp
from jax import lax
from jax.experimental import pallas as pl
from jax.experimental.pallas import tpu as pltpu

D_MODEL = 1024
D_SSM = 1024
SSM_HEAD_DIM = 64
SSM_HEADS = 16
SSM_GROUPS = 2
HEADS_PER_GROUP = SSM_HEADS // SSM_GROUPS
D_STATE = 128
D_CONV = 5
CHUNK = 128
D_XBC = D_SSM + 2 * SSM_GROUPS * D_STATE
D_ATTN = 1024
ATTN_HEADS = 8
ATTN_HEAD_DIM = 64
ATTN_V_DIM = 128
GROUP_WIDTH = D_SSM // SSM_GROUPS
DEPTH = 1
ALPHA = (2.0 * DEPTH) ** 0.25
LN_EPS = 1e-5
RMS_EPS = 1e-5
LAM_INIT = 0.8 - 0.6 * math.exp(-0.3 * 0)

LANES = 128
VMEM_LIMIT = 48 * 1024 * 1024

F32 = jnp.float32
BF16 = jnp.bfloat16


LOG2E = math.log2(math.e)


def _dot(a, b):
    return jnp.dot(a, b, preferred_element_type=F32)


def _dot_nt(a, b):
    return lax.dot_general(a, b, (((1,), (1,)), ((), ())), preferred_element_type=F32)


def _layer_norm(x, g, b):
    mu = jnp.mean(x, axis=-1, keepdims=True)
    xc = x - mu
    var = jnp.mean(xc * xc, axis=-1, keepdims=True)
    return xc * lax.rsqrt(var + LN_EPS) * g + b


def _silu(x):
    return x / (1.0 + jnp.exp(-x))


def _softplus(x):
    return jnp.maximum(x, 0.0) + jnp.log1p(jnp.exp(-jnp.abs(x)))


def _split3(x):
    hi = x.astype(BF16)
    r = x - hi.astype(F32)
    mid = r.astype(BF16)
    lo = (r - mid.astype(F32)).astype(BF16)
    return hi, mid, lo


def _dot_exact_lhs(x, m):
    hi, mid, lo = _split3(x)
    return _dot(hi, m) + _dot(mid, m) + _dot(lo, m)


def _const_spec(shape):
    nd = len(shape)
    return pl.BlockSpec(shape, lambda *_: (0,) * nd)


N_DT_TILES = 2 * SSM_GROUPS


ATTN_TQ = 256
ATTN_QPART = 256
ATTN_TK = 512
N_MAPS = 2 * ATTN_HEADS
NORM_SLACK = 1.01


def _ln_inproj_kernel(x_ref, g_ref, b_ref, wz_ref, wxbc_ref, wdtT_ref, wqT_ref, wk_ref, wvT_ref, wg_ref,
                      grpT_ref,
                      z_ref, xbc_ref, dtT_ref, qT_ref, k_ref, vT_ref, gate_ref, kn_ref, qn_ref):
    h = _layer_norm(x_ref[...], g_ref[...], b_ref[...]).astype(BF16)
    z_ref[...] = _dot(h, wz_ref[...])
    xbc_ref[...] = _dot(h, wxbc_ref[...])
    dtT_ref[...] = _dot_nt(wdtT_ref[...], h).reshape(dtT_ref.shape)
    qT = _dot_nt(wqT_ref[...], h) * (LOG2E * ATTN_HEAD_DIM ** -0.5)
    qT_ref[...] = qT.astype(BF16)
    k = _dot(h, wk_ref[...])
    k_ref[...] = k.astype(BF16)
    vT_ref[...] = _dot_nt(wvT_ref[...], h).astype(BF16)
    gate_ref[...] = _dot(h, wg_ref[...])
    lane = lax.broadcasted_iota(jnp.int32, (N_MAPS, LANES), 1)
    k2 = _dot_nt(grpT_ref[...], (k * k).astype(BF16))
    kn = jnp.sqrt(jnp.max(k2, axis=1, keepdims=True)) * NORM_SLACK
    kn_ref[...] = jnp.where(lane == 0, kn, 0.0)
    q2 = _dot(grpT_ref[...], (qT * qT).astype(BF16))
    qn = jnp.zeros((N_MAPS, LANES), F32)
    for part in range(q2.shape[1] // ATTN_QPART):
        pm = jnp.max(q2[:, part * ATTN_QPART:(part + 1) * ATTN_QPART], axis=1, keepdims=True)
        qn = jnp.where(lane == part, jnp.sqrt(pm) * NORM_SLACK, qn)
    qn_ref[...] = qn


def _ln_inproj(x2d, ln_g, ln_b, w_in, tm=ATTN_TK):
    t = x2d.shape[0]
    grp_t = (jnp.arange(N_MAPS)[:, None] == jnp.arange(D_ATTN)[None, :] // ATTN_HEAD_DIM).astype(BF16)
    offs = [0, D_SSM, D_SSM + D_XBC, D_SSM + D_XBC + 2 * SSM_HEADS]
    wz = w_in[:, offs[0]:offs[1]].astype(BF16)
    wxbc = w_in[:, offs[1]:offs[2]].astype(BF16)
    wdt = w_in[:, offs[2]:offs[3]].astype(BF16)
    o = offs[3]
    wq, wk, wv, wg = (w_in[:, o + i * D_ATTN:o + (i + 1) * D_ATTN].astype(BF16) for i in range(4))
    row = lambda n: pl.BlockSpec((tm, n), lambda i: (i, 0))
    col = lambda n: pl.BlockSpec((n, tm), lambda i: (0, i))
    weights = [wz, wxbc, wdt.T, wq.T, wk, wv.T, wg, grp_t]
    nt = t // tm
    return pl.pallas_call(
        _ln_inproj_kernel,
        grid=(nt,),
        in_specs=[row(D_MODEL), _const_spec((1, D_MODEL)), _const_spec((1, D_MODEL))]
                 + [pl.BlockSpec(w.shape, lambda i: (0, 0), pipeline_mode=pl.Buffered(1)) for w in weights],
        out_specs=[row(D_SSM), row(D_XBC),
                   pl.BlockSpec((N_DT_TILES, HEADS_PER_GROUP, tm), lambda i: (0, 0, i)),
                   col(D_ATTN), row(D_ATTN), col(D_ATTN), row(D_ATTN),
                   pl.BlockSpec((None, N_MAPS, LANES), lambda i: (i, 0, 0)),
                   pl.BlockSpec((None, N_MAPS, LANES), lambda i: (i, 0, 0))],
        out_shape=[jax.ShapeDtypeStruct((t, D_SSM), F32), jax.ShapeDtypeStruct((t, D_XBC), F32),
                   jax.ShapeDtypeStruct((N_DT_TILES, HEADS_PER_GROUP, t), F32),
                   jax.ShapeDtypeStruct((D_ATTN, t), BF16), jax.ShapeDtypeStruct((t, D_ATTN), BF16),
                   jax.ShapeDtypeStruct((D_ATTN, t), BF16), jax.ShapeDtypeStruct((t, D_ATTN), F32),
                   jax.ShapeDtypeStruct((nt, N_MAPS, LANES), F32), jax.ShapeDtypeStruct((nt, N_MAPS, LANES), F32)],
        compiler_params=pltpu.CompilerParams(dimension_semantics=("parallel",), vmem_limit_bytes=VMEM_LIMIT),
        name="ln_inproj",
    )(x2d, ln_g.reshape(1, -1), ln_b.reshape(1, -1), *weights)


HALO = 8


CONV_ROWS = 128


def _conv_silu_kernel(prev_ref, cur_ref, next_ref, w_ref, b_ref, xs_ref, bt_ref, c_ref, ext_ref):
    i = pl.program_id(1)
    n = pl.num_programs(1)
    tc = cur_ref.shape[0]
    ext_ref[0:HALO, :] = jnp.where(i == 0, 0.0, prev_ref[...])
    ext_ref[HALO:HALO + tc, :] = cur_ref[...]
    ext_ref[HALO + tc:, :] = jnp.where(i == n - 1, 0.0, next_ref[...])
    for cb in range(D_XBC // LANES):
        cols = slice(cb * LANES, (cb + 1) * LANES)
        for r0 in range(0, tc, CONV_ROWS):
            acc = b_ref[:, cols]
            for kk in range(D_CONV):
                acc = acc + ext_ref[pl.ds(r0 + HALO - D_CONV // 2 + kk, CONV_ROWS), cols] * w_ref[kk:kk + 1, cols]
            act = _silu(acc)
            rows = slice(r0, r0 + CONV_ROWS)
            if cb < D_SSM // LANES:
                xs_ref[rows, cols] = act
            elif cb < D_SSM // LANES + SSM_GROUPS:
                bt_ref[cb - D_SSM // LANES, :, rows] = act.T.astype(BF16)
            else:
                c_ref[rows, (cb - D_SSM // LANES - SSM_GROUPS) * LANES:(cb - D_SSM // LANES - SSM_GROUPS + 1) * LANES] = (
                    act.astype(BF16))


def _conv_silu(xbc, conv_w, conv_b, tc=1024):
    b, s, _ = xbc.shape
    nb = s // tc
    hb = tc // HALO
    return pl.pallas_call(
        _conv_silu_kernel,
        grid=(b, nb),
        in_specs=[
            pl.BlockSpec((None, HALO, D_XBC), lambda bi, i: (bi, jnp.maximum(i * hb - 1, 0), 0)),
            pl.BlockSpec((None, tc, D_XBC), lambda bi, i: (bi, i, 0)),
            pl.BlockSpec((None, HALO, D_XBC), lambda bi, i: (bi, jnp.minimum((i + 1) * hb, s // HALO - 1), 0)),
            _const_spec((D_CONV, D_XBC)),
            _const_spec((1, D_XBC)),
        ],
        out_specs=[
            pl.BlockSpec((None, tc, D_SSM), lambda bi, i: (bi, i, 0)),
            pl.BlockSpec((None, SSM_GROUPS, D_STATE, tc), lambda bi, i: (bi, 0, 0, i)),
            pl.BlockSpec((None, tc, SSM_GROUPS * D_STATE), lambda bi, i: (bi, i, 0)),
        ],
        out_shape=[
            jax.ShapeDtypeStruct((b, s, D_SSM), F32),
            jax.ShapeDtypeStruct((b, SSM_GROUPS, D_STATE, s), BF16),
            jax.ShapeDtypeStruct((b, s, SSM_GROUPS * D_STATE), BF16),
        ],
        scratch_shapes=[pltpu.VMEM((tc + 2 * HALO, D_XBC), F32)],
        compiler_params=pltpu.CompilerParams(dimension_semantics=("parallel", "parallel"),
                                             vmem_limit_bytes=VMEM_LIMIT),
        name="conv_silu",
    )(xbc, xbc, xbc, conv_w, conv_b.reshape(1, -1))


SSD_CHUNKS_IN_FLIGHT = 4


def _ssd_bodies(rev, xs, bts, cms, dtTs, alogs, biases, groups, states):
    L = CHUNK
    r = lax.broadcasted_iota(jnp.int32, (L, L), 0)
    c = lax.broadcasted_iota(jnp.int32, (L, L), 1)
    keep = (c >= r) if rev else (c <= r)
    cum_r = ((r >= c) if rev else (r <= c)).astype(BF16)
    hg = HEADS_PER_GROUP
    eh = lax.broadcasted_iota(jnp.int32, (6 * hg, 2 * GROUP_WIDTH), 0)
    el = lax.broadcasted_iota(jnp.int32, (6 * hg, 2 * GROUP_WIDTH), 1)
    spread = ((el // GROUP_WIDTH == eh // (3 * hg)) & ((el % GROUP_WIDTH) // SSM_HEAD_DIM == eh % hg)).astype(BF16)
    lane = lax.broadcasted_iota(jnp.int32, (L, LANES), 1)
    edge = 0 if rev else L - 1
    gs = range(len(xs))

    dt_t = [_softplus(dtTs[g] + biases[g]) for g in gs]
    acs_t = [_dot_exact_lhs(dt_t[g] * (-jnp.exp(alogs[g])), cum_r) for g in gs]
    pieces_t = [jnp.concatenate([p.astype(F32) for p in _split3(dt_t[g]) + _split3(acs_t[g])], axis=0) for g in gs]
    both_x = [_dot(pieces_t[g].T.astype(BF16), spread) for g in gs]
    acs = [acs_t[g].T for g in gs]
    cb = [_dot(cms[g], bts[g]) for g in gs]
    dt_x = [both_x[g][:, :GROUP_WIDTH] for g in gs]
    acs_x = [both_x[g][:, GROUP_WIDTH:] for g in gs]
    tot_x = [acs_x[g][edge:edge + 1, :] for g in gs]
    xdt = [xs[g] * dt_x[g] for g in gs]

    y_diag = []
    for g in gs:
        tiles = []
        for t in range(GROUP_WIDTH // LANES):
            xt = xdt[g][:, t * LANES:(t + 1) * LANES]
            gmats, xparts = [], []
            for half in range(LANES // SSM_HEAD_DIM):
                j = t * (LANES // SSM_HEAD_DIM) + half
                seg = acs[g][:, j:j + 1] - acs_t[g][j:j + 1, :]
                gmats.append((cb[g] * jnp.exp(jnp.where(keep, seg, -jnp.inf))).astype(BF16))
                in_head = (lane >= half * SSM_HEAD_DIM) & (lane < (half + 1) * SSM_HEAD_DIM)
                xparts.append(jnp.where(in_head, xt, 0.0).astype(BF16))
            tiles.append(_dot(jnp.concatenate(gmats, axis=1), jnp.concatenate(xparts, axis=0)))
        y_diag.append(jnp.concatenate(tiles, axis=1))

    states = list(states)
    ys = []
    for g in gs:
        grp = groups[g]
        y_off = _dot(cms[g], states[grp].astype(BF16)) * jnp.exp(acs_x[g])
        ys.append(y_off + y_diag[g])
        w = (xdt[g] * jnp.exp(tot_x[g] - acs_x[g])).astype(BF16)
        states[grp] = jnp.exp(tot_x[g]) * states[grp] + _dot(bts[g], w)
    return ys, states


def _ssd_kernel(rev, nchunk, *refs):
    if rev:
        (xs_ref, bt_ref, c_ref, dtT_ref, alc_ref, bc_ref, yf_ref, z_ref, ng_ref, y_ref, state_ref) = refs
    else:
        (xs_ref, bt_ref, c_ref, dtT_ref, alc_ref, bc_ref, dx_ref, y_ref, state_ref) = refs

    @pl.when(pl.program_id(1) == 0)
    def _():
        state_ref[...] = jnp.zeros_like(state_ref)

    gw = GROUP_WIDTH
    gs = range(SSM_GROUPS)
    cols = [slice(g * gw, (g + 1) * gw) for g in gs]
    order = list(range(nchunk - 1, -1, -1) if rev else range(nchunk))
    states = [state_ref[g] for g in gs]
    for c0 in range(0, nchunk, SSD_CHUNKS_IN_FLIGHT):
        bodies = [(slice(ci * CHUNK, (ci + 1) * CHUNK), g) for ci in order[c0:c0 + SSD_CHUNKS_IN_FLIGHT] for g in gs]
        x = [xs_ref[rows, cols[g]] for rows, g in bodies]
        ys, states = _ssd_bodies(rev, x, [bt_ref[g, :, rows] for rows, g in bodies],
                                 [c_ref[rows, g * D_STATE:(g + 1) * D_STATE] for rows, g in bodies],
                                 [dtT_ref[g, :, rows] for rows, g in bodies],
                                 [alc_ref[g] for _, g in bodies], [bc_ref[g] for _, g in bodies],
                                 [g for _, g in bodies], states)
        for (rows, g), xb, y in zip(bodies, x, ys):
            if rev:
                yy = (yf_ref[rows, cols[g]] + y) * _silu(z_ref[rows, cols[g]])
                ms = jnp.mean(yy * yy, axis=-1, keepdims=True)
                y_ref[rows, cols[g]] = (yy * lax.rsqrt(ms + RMS_EPS) * ng_ref[g]).astype(BF16)
            else:
                y_ref[rows, cols[g]] = y + dx_ref[g] * xb
    for g in gs:
        state_ref[g] = states[g]


def _ssd_pass(rev, xs, bt, cm, dtT, a_log, dt_bias, extra, nchunk=8):
    b, s, _ = xs.shape
    tcs = nchunk * CHUNK
    nblk = s // tcs
    blk = (lambda i: nblk - 1 - i) if rev else (lambda i: i)
    dirn = 1 if rev else 0
    gw, hg, ng = GROUP_WIDTH, HEADS_PER_GROUP, SSM_GROUPS
    seq_spec = lambda w: pl.BlockSpec((None, tcs, w), lambda bi, i: (bi, blk(i), 0))
    in_specs = [
        seq_spec(D_SSM),
        pl.BlockSpec((None, ng, D_STATE, tcs), lambda bi, i: (bi, 0, 0, blk(i))),
        seq_spec(ng * D_STATE),
        pl.BlockSpec((ng, hg, tcs), lambda bi, i: (dirn, 0, bi * nblk + blk(i))),
        _const_spec((ng, hg, 1)), _const_spec((ng, hg, 1)),
    ]
    args = [xs, bt, cm, dtT, a_log.reshape(ng, hg, 1), dt_bias.reshape(ng, hg, 1)]
    if rev:
        y_fwd, z, norm_g = extra
        in_specs += [seq_spec(D_SSM), seq_spec(D_SSM), _const_spec((ng, 1, gw))]
        args += [y_fwd, z, norm_g.reshape(ng, 1, gw)]
    else:
        (d_skip,) = extra
        in_specs += [_const_spec((ng, 1, gw))]
        args += [jnp.repeat(d_skip, SSM_HEAD_DIM).reshape(ng, 1, gw)]
    return pl.pallas_call(
        functools.partial(_ssd_kernel, rev, nchunk),
        grid=(b, nblk),
        in_specs=in_specs,
        out_specs=seq_spec(D_SSM),
        out_shape=jax.ShapeDtypeStruct((b, s, D_SSM), BF16 if rev else F32),
        scratch_shapes=[pltpu.VMEM((ng, D_STATE, gw), F32)],
        compiler_params=pltpu.CompilerParams(dimension_semantics=("parallel", "arbitrary"),
                                             vmem_limit_bytes=VMEM_LIMIT),
        name="ssd_bwd" if rev else "ssd_fwd",
    )(*args)


NEG_BIG = -1e30


SKIP_LOG2 = 80.0
FIXED_SHIFT_MAX_LOG2 = 60.0
PV_DEPTH = 256


def _attn_plan(tq, tk, nk, nq, bi, hi, qi, slope, kn_ref, qn_ref, plan_i_ref, plan_f_ref):
    i0 = qi * tq
    jd = qi // (tk // tq)
    parts = tq // ATTN_QPART
    qns, bases = [], []
    for r in range(2):
        base = (bi * ATTN_HEADS + hi) * 2 + r
        qn = qn_ref[base * (nq * parts) + qi * parts]
        for part in range(1, parts):
            qn = jnp.maximum(qn, qn_ref[base * (nq * parts) + qi * parts + part])
        qns.append(qn)
        bases.append(base * nk)
    jlo, jhi = jd, jd
    kmax = [kn_ref[bases[0]], kn_ref[bases[1]]]
    for jt in range(nk):
        dist = jnp.maximum(jnp.maximum(i0 - (jt * tk + tk - 1), jt * tk - (i0 + tq - 1)), 0).astype(F32)
        need = None
        for r in range(2):
            kn = kn_ref[bases[r] + jt]
            kmax[r] = jnp.maximum(kmax[r], kn)
            reach = SKIP_LOG2 + qns[r] * (kn + kn_ref[bases[r] + jd])
            need_r = slope * dist <= reach
            need = need_r if need is None else (need | need_r)
        jlo = jnp.where(need, jnp.minimum(jlo, jt), jlo)
        jhi = jnp.where(need, jnp.maximum(jhi, jt), jhi)
    plan_i_ref[0] = jlo
    plan_i_ref[1] = jhi
    odd = (jhi - jlo + 1) % 2
    room_above = (jhi < nk - 1).astype(jnp.int32)
    plan_i_ref[2] = jlo - odd * (1 - room_above)
    plan_i_ref[3] = jhi + odd * room_above
    plan_f_ref[0] = jnp.maximum(qns[0] * kmax[0], qns[1] * kmax[1])


def _attn_q_tile(tq, tk, nk, nq, bi, hi, qi, slope, kn_ref, qn_ref, qT_ref, k_ref, vT_ref, lq1_ref, lk1_ref, lq2_ref,
                 lk2_ref, sg_ref, gate_ref, o_ref, e_ref, s_ref, s1_ref, acc_ref, l_ref, plan_i_ref, plan_f_ref, pre_ref):
    ratio = tk // tq
    i0 = qi * tq
    q0 = pl.multiple_of(i0, tq)
    jd = qi // ratio
    dsel = qi % ratio
    jlo, jhi, jl, jh, score_bound = plan_i_ref[0], plan_i_ref[1], plan_i_ref[2], plan_i_ref[3], plan_f_ref[0]
    first_scores_ready = pre_ref[0]
    qi_next = jnp.minimum(qi + 1, nq - 1)

    def plan_next():
        _attn_plan(tq, tk, nk, nq, bi, hi, qi_next, slope, kn_ref, qn_ref, plan_i_ref, plan_f_ref)

    row = lax.broadcasted_iota(jnp.int32, (2 * ATTN_HEAD_DIM, tq), 0)

    def masked_q(q_start):
        qf = qT_ref[:, pl.ds(q_start, tq)].astype(F32)
        return (jnp.where(row < ATTN_HEAD_DIM, qf, 0.0).astype(BF16),
                jnp.where(row >= ATTN_HEAD_DIM, qf, 0.0).astype(BF16))

    rhs = masked_q(q0)

    def finish_previous():
        _attn_epilogue(tq, jnp.maximum(qi - 1, 0), lq1_ref, lk1_ref, lq2_ref, lk2_ref, sg_ref, gate_ref, o_ref, acc_ref, l_ref)
        acc_ref[...] = jnp.zeros_like(acc_ref)

    def tile_bias(j):
        off = (i0 - j * tk).astype(F32)
        before, after = j < jd, j > jd
        eidx = jnp.where(before, 0, jnp.where(after, 1, 2 + dsel))
        cj = jnp.where(before, -slope * (off - float(tk - 1)), jnp.where(after, slope * (off + float(tq - 1)), 0.0))
        return eidx, cj

    @pl.when(score_bound <= FIXED_SHIFT_MAX_LOG2)
    def _():
        def stage_a(j, buf, w):
            kt = k_ref[pl.ds(pl.multiple_of(j * tk, tk), tk), :]
            for r in range(2):
                buf[r] = _dot(kt, w[r])

        @pl.when(first_scores_ready == 0)
        def _():
            stage_a(jl, s_ref, rhs)

        plan_next()
        jl_next = plan_i_ref[2]
        rhs_next = masked_q(pl.multiple_of(qi_next * tq, tq))

        def stage_b(j, buf, l):
            eidx, cj = tile_bias(j)
            half = jnp.exp2(jnp.full((1, tq), 0.5 * cj, F32))
            vt = vT_ref[:, pl.ds(pl.multiple_of(j * tk, tk), tk)]
            l_new = []
            for r in range(2):
                psum, pv = None, None
                for k0 in range(0, tk, PV_DEPTH):
                    p = jnp.exp2(buf[r, k0:k0 + PV_DEPTH, :] + e_ref[eidx, k0:k0 + PV_DEPTH, :])
                    ps = jnp.sum(p, axis=0, keepdims=True)
                    d = _dot(vt[:, k0:k0 + PV_DEPTH], p.astype(BF16))
                    psum, pv = (ps, d) if pv is None else (psum + ps, pv + d)
                l_new.append(l[r] + psum * half * half)
                acc_ref[r] += pv * half * half
            return tuple(l_new)

        def pair(j, l):
            stage_a(j + 1, s1_ref, rhs)
            l = stage_b(j, s_ref, l)
            done = j + 2 > jh
            w_next = tuple(jnp.where(done, rhs_next[r], rhs[r]) for r in range(2))
            stage_a(jnp.where(done, jl_next, j + 2), s_ref, w_next)
            return stage_b(j + 1, s1_ref, l)

        def body(t, l):
            j = jl + 2 + 4 * t
            return pair(j + 2, pair(j, l))

        finish_previous()
        zero = jnp.zeros((1, tq), F32)
        l = pair(jl, (zero, zero))
        rest = (jh - jl + 1) // 2 - 1
        l = lax.fori_loop(0, rest // 2, body, l)
        l_ref[0] = l[0]
        l_ref[1] = l[1]

        @pl.when(rest % 2 == 1)
        def _():
            l = pair(jh - 1, (l_ref[0], l_ref[1]))
            l_ref[0] = l[0]
            l_ref[1] = l[1]

        pre_ref[0] = 1

    def stage_a(j, m_old):
        eidx, cj = tile_bias(j)
        kt = k_ref[pl.ds(pl.multiple_of(j * tk, tk), tk), :]
        m_new, shift, alpha = [], [], []
        for r in range(2):
            s = _dot(kt, rhs[r]) + e_ref[eidx]
            s_ref[r] = s
            mr = jnp.maximum(m_old[r], jnp.max(s, axis=0, keepdims=True) + cj)
            m_new.append(mr)
            shift.append(mr - cj)
            alpha.append(jnp.exp2(m_old[r] - mr))
        return tuple(m_new), tuple(shift), tuple(alpha)

    def stage_b(j, shift, alpha, l_old):
        vt = vT_ref[:, pl.ds(pl.multiple_of(j * tk, tk), tk)]
        l_new = []
        for r in range(2):
            p = jnp.exp2(s_ref[r] - shift[r])
            l_new.append(alpha[r] * l_old[r] + jnp.sum(p, axis=0, keepdims=True))
            acc_ref[r] = alpha[r] * acc_ref[r] + _dot(vt, p.astype(BF16))
        return tuple(l_new)

    def body(t, carry):
        m, shift, alpha, l = carry
        j = jlo + t
        l = stage_b(jnp.maximum(j - 1, jlo), shift, alpha, l)
        m, shift, alpha = stage_a(jnp.minimum(j, jhi), m)
        return m, shift, alpha, l

    @pl.when(score_bound > FIXED_SHIFT_MAX_LOG2)
    def _():
        neg = jnp.full((1, tq), NEG_BIG, F32)
        big = jnp.full((1, tq), -NEG_BIG, F32)
        one = jnp.ones((1, tq), F32)
        zero = jnp.zeros((1, tq), F32)
        finish_previous()
        plan_next()
        pre_ref[0] = 0
        init = ((neg, neg), (big, big), (one, one), (zero, zero))
        _, _, _, l = lax.fori_loop(0, jhi - jlo + 2, body, init)
        l_ref[0] = l[0]
        l_ref[1] = l[1]


def _attn_epilogue(tq, qi, lq1_ref, lk1_ref, lq2_ref, lk2_ref, sg_ref, gate_ref, o_ref, acc_ref, l_ref):
    lam = (jnp.exp(jnp.sum(lq1_ref[...] * lk1_ref[...], axis=-1, keepdims=True))
           - jnp.exp(jnp.sum(lq2_ref[...] * lk2_ref[...], axis=-1, keepdims=True)) + LAM_INIT)
    o = acc_ref[0] / l_ref[0] - lam * (acc_ref[1] / l_ref[1])
    ms = jnp.mean(o * o, axis=0, keepdims=True)
    q0 = pl.multiple_of(qi * tq, tq)
    on = (o * lax.rsqrt(ms + RMS_EPS) * sg_ref[...] * (1.0 - LAM_INIT)).T
    o_ref[pl.ds(q0, tq), :] = (on * _silu(gate_ref[pl.ds(q0, tq), :])).astype(BF16)


def _attn_kernel(tq, tk, nk, nq, slopes_ref, kn_ref, qn_ref, qT_ref, k_ref, vT_ref, lq1_ref, lk1_ref, lq2_ref,
                 lk2_ref, sg_ref, gate_ref, o_ref, e_ref, s_ref, s1_ref, acc_ref, l_ref, plan_i_ref, plan_f_ref, pre_ref):
    bi, hi = pl.program_id(0), pl.program_id(1)
    slope = slopes_ref[hi]
    ii = lax.broadcasted_iota(jnp.int32, (tk, tq), 1)
    jj = lax.broadcasted_iota(jnp.int32, (tk, tq), 0)
    rel = (ii - jj).astype(F32)
    e_ref[0] = -slope * (rel + float(tk - 1))
    e_ref[1] = slope * (rel - float(tq - 1))
    for d in range(tk // tq):
        e_ref[2 + d] = -slope * jnp.abs(rel + float(d * tq))
    s_ref[...] = jnp.zeros_like(s_ref)
    acc_ref[...] = jnp.zeros_like(acc_ref)
    l_ref[...] = jnp.ones_like(l_ref)
    _attn_plan(tq, tk, nk, nq, bi, hi, 0, slope, kn_ref, qn_ref, plan_i_ref, plan_f_ref)
    pre_ref[0] = 0

    def q_tile(qi, carry):
        _attn_q_tile(tq, tk, nk, nq, bi, hi, qi, slope, kn_ref, qn_ref, qT_ref, k_ref, vT_ref, lq1_ref, lk1_ref,
                     lq2_ref, lk2_ref, sg_ref, gate_ref, o_ref, e_ref, s_ref, s1_ref, acc_ref, l_ref, plan_i_ref, plan_f_ref,
                     pre_ref)
        return carry

    lax.fori_loop(0, nq, q_tile, 0)
    _attn_epilogue(tq, nq - 1, lq1_ref, lk1_ref, lq2_ref, lk2_ref, sg_ref, gate_ref, o_ref, acc_ref, l_ref)


def _diff_attn(qT, k3, vT, kn, qn, lq1, lk1, lq2, lk2, subln_g, gate3, tq=ATTN_TQ, tk=ATTN_TK):
    b, s, _ = k3.shape
    nq, nk = s // tq, s // tk
    ratio = tk // tq
    assert tk % tq == 0 and nk % 2 == 0
    hw = 2 * ATTN_HEAD_DIM
    slopes = jnp.exp2(-8.0 * (jnp.arange(ATTN_HEADS, dtype=F32) + 1.0) / ATTN_HEADS) * LOG2E
    kn_tab = jnp.transpose(kn[:, :, 0].reshape(b, nk, N_MAPS), (0, 2, 1)).reshape(-1)
    kparts = tk // ATTN_QPART
    qn_tab = jnp.transpose(qn[:, :, :kparts].reshape(b, nk, N_MAPS, kparts), (0, 2, 1, 3)).reshape(-1)
    vec = lambda a: a.reshape(1, -1)
    return pl.pallas_call(
        functools.partial(_attn_kernel, tq, tk, nk, nq),
        grid=(b, ATTN_HEADS),
        in_specs=[
            pl.BlockSpec(memory_space=pltpu.SMEM),
            pl.BlockSpec(memory_space=pltpu.SMEM),
            pl.BlockSpec(memory_space=pltpu.SMEM),
            pl.BlockSpec((hw, s), lambda bi, hi: (hi, bi)),
            pl.BlockSpec((None, s, hw), lambda bi, hi: (bi, 0, hi)),
            pl.BlockSpec((ATTN_V_DIM, s), lambda bi, hi: (hi, bi)),
            _const_spec((1, ATTN_HEAD_DIM)), _const_spec((1, ATTN_HEAD_DIM)),
            _const_spec((1, ATTN_HEAD_DIM)), _const_spec((1, ATTN_HEAD_DIM)),
            _const_spec((ATTN_V_DIM, 1)),
            pl.BlockSpec((None, s, ATTN_V_DIM), lambda bi, hi: (bi, 0, hi)),
        ],
        out_specs=pl.BlockSpec((None, s, ATTN_V_DIM), lambda bi, hi: (bi, 0, hi)),
        out_shape=jax.ShapeDtypeStruct((b, s, ATTN_HEADS * ATTN_V_DIM), BF16),
        scratch_shapes=[pltpu.VMEM((2 + tk // tq, tk, tq), F32),
                        pltpu.VMEM((2, tk, tq), F32), pltpu.VMEM((2, tk, tq), F32),
                        pltpu.VMEM((2, ATTN_V_DIM, tq), F32), pltpu.VMEM((2, 1, tq), F32),
                        pltpu.SMEM((4,), jnp.int32), pltpu.SMEM((1,), F32), pltpu.SMEM((1,), jnp.int32)],
        compiler_params=pltpu.CompilerParams(dimension_semantics=("parallel", "parallel"),
                                             vmem_limit_bytes=VMEM_LIMIT),
        name="diff_attn",
    )(slopes, kn_tab, qn_tab, qT, k3, vT, vec(lq1), vec(lk1), vec(lq2), vec(lk2), subln_g.reshape(-1, 1), gate3)


def _out_proj_kernel(x_ref, eg_ref, eb_ref, ys_ref, ya_ref, w_ref, g_ref, b_ref, out_ref):
    h = _layer_norm(x_ref[...], eg_ref[...], eb_ref[...])
    mix = _dot(ys_ref[...], w_ref[:D_SSM, :]) + _dot(ya_ref[...], w_ref[D_SSM:, :])
    out_ref[...] = _layer_norm(ALPHA * h + mix, g_ref[...], b_ref[...])


def _out_proj(x2d, ln_emb_g, ln_emb_b, y_ssm, y_attn, w_out, ln_g, ln_b, tm=512):
    t = x2d.shape[0]
    row = lambda n: pl.BlockSpec((tm, n), lambda i: (i, 0))
    vec = lambda a: a.reshape(1, -1)
    return pl.pallas_call(
        _out_proj_kernel,
        grid=(t // tm,),
        in_specs=[row(D_MODEL), _const_spec((1, D_MODEL)), _const_spec((1, D_MODEL)),
                  row(D_SSM), row(D_ATTN),
                  _const_spec((D_SSM + D_ATTN, D_MODEL)), _const_spec((1, D_MODEL)), _const_spec((1, D_MODEL))],
        out_specs=row(D_MODEL),
        out_shape=jax.ShapeDtypeStruct((t, D_MODEL), F32),
        compiler_params=pltpu.CompilerParams(dimension_semantics=("parallel",), vmem_limit_bytes=VMEM_LIMIT),
        name="out_proj",
    )(x2d, vec(ln_emb_g), vec(ln_emb_b), y_ssm, y_attn, w_out.astype(BF16), vec(ln_g), vec(ln_b))


def kernel(x, ln_emb_g, ln_emb_b, w_in, conv_w, conv_b, A_log_fwd, A_log_bwd, dt_bias_fwd, dt_bias_bwd, D_skip,
           ssm_norm_g, lambda_q1, lambda_k1, lambda_q2, lambda_k2, subln_g, w_out, ln_g, ln_b):
    b, s, _ = x.shape
    t = b * s
    x2d = x.reshape(t, D_MODEL)
    z, xbc, dtT, qT, k, vT, gate, kn, qn = _ln_inproj(x2d, ln_emb_g, ln_emb_b, w_in[0])

    xs, bt, cm = _conv_silu(xbc.reshape(b, s, D_XBC), conv_w[0], conv_b[0])
    y_fwd = _ssd_pass(False, xs, bt, cm, dtT, A_log_fwd[0], dt_bias_fwd[0], (D_skip[0],))
    y_ssm = _ssd_pass(True, xs, bt, cm, dtT, A_log_bwd[0], dt_bias_bwd[0],
                      (y_fwd, z.reshape(b, s, D_SSM), ssm_norm_g[0]))

    y_attn = _diff_attn(qT, k.reshape(b, s, D_ATTN), vT, kn, qn, lambda_q1[0], lambda_k1[0], lambda_q2[0],
                        lambda_k2[0], subln_g[0], gate.reshape(b, s, D_ATTN)).reshape(t, D_ATTN)

    out = _out_proj(x2d, ln_emb_g, ln_emb_b, y_ssm.reshape(t, D_SSM), y_attn, w_out[0], ln_g[0], ln_b[0])
    return out.reshape(b, s, D_MODEL)
```

```python
import functools
import math

import jax
import jax.numpy as jnp
from jax import lax
from jax.experimental import pallas as pl
from jax.experimental.pallas import tpu as pltpu

D_MODEL = 1024
D_SSM = 1024
SSM_HEAD_DIM = 64
SSM_HEADS = 16
SSM_GROUPS = 2
HEADS_PER_GROUP = SSM_HEADS // SSM_GROUPS
D_STATE = 128
D_CONV = 5
CHUNK = 128
D_XBC = D_SSM + 2 * SSM_GROUPS * D_STATE
D_ATTN = 1024
ATTN_HEADS = 8
ATTN_HEAD_DIM = 64
ATTN_V_DIM = 128
GROUP_WIDTH = D_SSM // SSM_GROUPS
DEPTH = 1
ALPHA = (2.0 * DEPTH) ** 0.25
LN_EPS = 1e-5
RMS_EPS = 1e-5
LAM_INIT = 0.8 - 0.6 * math.exp(-0.3 * 0)

LANES = 128
VMEM_LIMIT = 48 * 1024 * 1024

F32 = jnp.float32
BF16 = jnp.bfloat16


LOG2E = math.log2(math.e)


def _dot(a, b):
    return jnp.dot(a, b, preferred_element_type=F32)


def _dot_nt(a, b):
    return lax.dot_general(a, b, (((1,), (1,)), ((), ())), preferred_element_type=F32)


def _layer_norm(x, g, b):
    mu = jnp.mean(x, axis=-1, keepdims=True)
    xc = x - mu
    var = jnp.mean(xc * xc, axis=-1, keepdims=True)
    return xc * lax.rsqrt(var + LN_EPS) * g + b


def _silu(x):
    return x / (1.0 + jnp.exp(-x))


def _softplus(x):
    return jnp.maximum(x, 0.0) + jnp.log1p(jnp.exp(-jnp.abs(x)))


def _split3(x):
    hi = x.astype(BF16)
    r = x - hi.astype(F32)
    mid = r.astype(BF16)
    lo = (r - mid.astype(F32)).astype(BF16)
    return hi, mid, lo


def _dot_exact_lhs(x, m):
    hi, mid, lo = _split3(x)
    return _dot(hi, m) + _dot(mid, m) + _dot(lo, m)


def _const_spec(shape):
    nd = len(shape)
    return pl.BlockSpec(shape, lambda *_: (0,) * nd)


N_DT_TILES = 2 * SSM_GROUPS


ATTN_TQ = 256
ATTN_QPART = 256
ATTN_TK = 512
N_MAPS = 2 * ATTN_HEADS
NORM_SLACK = 1.01


def _ln_inproj_kernel(x_ref, g_ref, b_ref, wz_ref, wxbc_ref, wdtT_ref, wqT_ref, wk_ref, wvT_ref, wg_ref,
                      grpT_ref,
                      z_ref, xbc_ref, dtT_ref, qT_ref, k_ref, vT_ref, gate_ref, kn_ref, qn_ref):
    h = _layer_norm(x_ref[...], g_ref[...], b_ref[...]).astype(BF16)
    z_ref[...] = _dot(h, wz_ref[...])
    xbc_ref[...] = _dot(h, wxbc_ref[...])
    dtT_ref[...] = _dot_nt(wdtT_ref[...], h).reshape(dtT_ref.shape)
    qT = _dot_nt(wqT_ref[...], h) * (LOG2E * ATTN_HEAD_DIM ** -0.5)
    qT_ref[...] = qT.astype(BF16)
    k = _dot(h, wk_ref[...])
    k_ref[...] = k.astype(BF16)
    vT_ref[...] = _dot_nt(wvT_ref[...], h).astype(BF16)
    gate_ref[...] = _dot(h, wg_ref[...])
    lane = lax.broadcasted_iota(jnp.int32, (N_MAPS, LANES), 1)
    k2 = _dot_nt(grpT_ref[...], (k * k).astype(BF16))
    kn = jnp.sqrt(jnp.max(k2, axis=1, keepdims=True)) * NORM_SLACK
    kn_ref[...] = jnp.where(lane == 0, kn, 0.0)
    q2 = _dot(grpT_ref[...], (qT * qT).astype(BF16))
    qn = jnp.zeros((N_MAPS, LANES), F32)
    for part in range(q2.shape[1] // ATTN_QPART):
        pm = jnp.max(q2[:, part * ATTN_QPART:(part + 1) * ATTN_QPART], axis=1, keepdims=True)
        qn = jnp.where(lane == part, jnp.sqrt(pm) * NORM_SLACK, qn)
    qn_ref[...] = qn


def _ln_inproj(x2d, ln_g, ln_b, w_in, tm=ATTN_TK):
    t = x2d.shape[0]
    grp_t = (jnp.arange(N_MAPS)[:, None] == jnp.arange(D_ATTN)[None, :] // ATTN_HEAD_DIM).astype(BF16)
    offs = [0, D_SSM, D_SSM + D_XBC, D_SSM + D_XBC + 2 * SSM_HEADS]
    wz = w_in[:, offs[0]:offs[1]].astype(BF16)
    wxbc = w_in[:, offs[1]:offs[2]].astype(BF16)
    wdt = w_in[:, offs[2]:offs[3]].astype(BF16)
    o = offs[3]
    wq, wk, wv, wg = (w_in[:, o + i * D_ATTN:o + (i + 1) * D_ATTN].astype(BF16) for i in range(4))
    row = lambda n: pl.BlockSpec((tm, n), lambda i: (i, 0))
    col = lambda n: pl.BlockSpec((n, tm), lambda i: (0, i))
    weights = [wz, wxbc, wdt.T, wq.T, wk, wv.T, wg, grp_t]
    nt = t // tm
    return pl.pallas_call(
        _ln_inproj_kernel,
        grid=(nt,),
        in_specs=[row(D_MODEL), _const_spec((1, D_MODEL)), _const_spec((1, D_MODEL))]
                 + [pl.BlockSpec(w.shape, lambda i: (0, 0), pipeline_mode=pl.Buffered(1)) for w in weights],
        out_specs=[row(D_SSM), row(D_XBC),
                   pl.BlockSpec((N_DT_TILES, HEADS_PER_GROUP, tm), lambda i: (0, 0, i)),
                   col(D_ATTN), row(D_ATTN), col(D_ATTN), row(D_ATTN),
                   pl.BlockSpec((None, N_MAPS, LANES), lambda i: (i, 0, 0)),
                   pl.BlockSpec((None, N_MAPS, LANES), lambda i: (i, 0, 0))],
        out_shape=[jax.ShapeDtypeStruct((t, D_SSM), F32), jax.ShapeDtypeStruct((t, D_XBC), F32),
                   jax.ShapeDtypeStruct((N_DT_TILES, HEADS_PER_GROUP, t), F32),
                   jax.ShapeDtypeStruct((D_ATTN, t), BF16), jax.ShapeDtypeStruct((t, D_ATTN), BF16),
                   jax.ShapeDtypeStruct((D_ATTN, t), BF16), jax.ShapeDtypeStruct((t, D_ATTN), F32),
                   jax.ShapeDtypeStruct((nt, N_MAPS, LANES), F32), jax.ShapeDtypeStruct((nt, N_MAPS, LANES), F32)],
        compiler_params=pltpu.CompilerParams(dimension_semantics=("parallel",), vmem_limit_bytes=VMEM_LIMIT),
        name="ln_inproj",
    )(x2d, ln_g.reshape(1, -1), ln_b.reshape(1, -1), *weights)


HALO = 8


CONV_ROWS = 128


def _conv_silu_kernel(prev_ref, cur_ref, next_ref, w_ref, b_ref, xs_ref, bt_ref, c_ref, ext_ref):
    i = pl.program_id(1)
    n = pl.num_programs(1)
    tc = cur_ref.shape[0]
    ext_ref[0:HALO, :] = jnp.where(i == 0, 0.0, prev_ref[...])
    ext_ref[HALO:HALO + tc, :] = cur_ref[...]
    ext_ref[HALO + tc:, :] = jnp.where(i == n - 1, 0.0, next_ref[...])
    for cb in range(D_XBC // LANES):
        cols = slice(cb * LANES, (cb + 1) * LANES)
        for r0 in range(0, tc, CONV_ROWS):
            acc = b_ref[:, cols]
            for kk in range(D_CONV):
                acc = acc + ext_ref[pl.ds(r0 + HALO - D_CONV // 2 + kk, CONV_ROWS), cols] * w_ref[kk:kk + 1, cols]
            act = _silu(acc)
            rows = slice(r0, r0 + CONV_ROWS)
            if cb < D_SSM // LANES:
                xs_ref[rows, cols] = act
            elif cb < D_SSM // LANES + SSM_GROUPS:
                bt_ref[cb - D_SSM // LANES, :, rows] = act.T.astype(BF16)
            else:
                c_ref[rows, (cb - D_SSM // LANES - SSM_GROUPS) * LANES:(cb - D_SSM // LANES - SSM_GROUPS + 1) * LANES] = (
                    act.astype(BF16))


def _conv_silu(xbc, conv_w, conv_b, tc=1024):
    b, s, _ = xbc.shape
    nb = s // tc
    hb = tc // HALO
    return pl.pallas_call(
        _conv_silu_kernel,
        grid=(b, nb),
        in_specs=[
            pl.BlockSpec((None, HALO, D_XBC), lambda bi, i: (bi, jnp.maximum(i * hb - 1, 0), 0)),
            pl.BlockSpec((None, tc, D_XBC), lambda bi, i: (bi, i, 0)),
            pl.BlockSpec((None, HALO, D_XBC), lambda bi, i: (bi, jnp.minimum((i + 1) * hb, s // HALO - 1), 0)),
            _const_spec((D_CONV, D_XBC)),
            _const_spec((1, D_XBC)),
        ],
        out_specs=[
            pl.BlockSpec((None, tc, D_SSM), lambda bi, i: (bi, i, 0)),
            pl.BlockSpec((None, SSM_GROUPS, D_STATE, tc), lambda bi, i: (bi, 0, 0, i)),
            pl.BlockSpec((None, tc, SSM_GROUPS * D_STATE), lambda bi, i: (bi, i, 0)),
        ],
        out_shape=[
            jax.ShapeDtypeStruct((b, s, D_SSM), F32),
            jax.ShapeDtypeStruct((b, SSM_GROUPS, D_STATE, s), BF16),
            jax.ShapeDtypeStruct((b, s, SSM_GROUPS * D_STATE), BF16),
        ],
        scratch_shapes=[pltpu.VMEM((tc + 2 * HALO, D_XBC), F32)],
        compiler_params=pltpu.CompilerParams(dimension_semantics=("parallel", "parallel"),
                                             vmem_limit_bytes=VMEM_LIMIT),
        name="conv_silu",
    )(xbc, xbc, xbc, conv_w, conv_b.reshape(1, -1))


SSD_CHUNKS_IN_FLIGHT = 8


def _ssd_bodies(rev, xs, bts, cms, dtTs, alogs, biases, groups, states):
    L = CHUNK
    r = lax.broadcasted_iota(jnp.int32, (L, L), 0)
    c = lax.broadcasted_iota(jnp.int32, (L, L), 1)
    keep = (c >= r) if rev else (c <= r)
    cum_r = ((r >= c) if rev else (r <= c)).astype(BF16)
    hg = HEADS_PER_GROUP
    eh = lax.broadcasted_iota(jnp.int32, (6 * hg, 2 * GROUP_WIDTH), 0)
    el = lax.broadcasted_iota(jnp.int32, (6 * hg, 2 * GROUP_WIDTH), 1)
    spread = ((el // GROUP_WIDTH == eh // (3 * hg)) & ((el % GROUP_WIDTH) // SSM_HEAD_DIM == eh % hg)).astype(BF16)
    lane = lax.broadcasted_iota(jnp.int32, (L, LANES), 1)
    edge = 0 if rev else L - 1
    gs = range(len(xs))

    dt_t = [_softplus(dtTs[g] + biases[g]) for g in gs]
    acs_t = [_dot_exact_lhs(dt_t[g] * (-jnp.exp(alogs[g])), cum_r) for g in gs]
    pieces_t = [jnp.concatenate([p.astype(F32) for p in _split3(dt_t[g]) + _split3(acs_t[g])], axis=0) for g in gs]
    both_x = [_dot(pieces_t[g].T.astype(BF16), spread) for g in gs]
    acs = [acs_t[g].T for g in gs]
    cb = [_dot(cms[g], bts[g]) for g in gs]
    dt_x = [both_x[g][:, :GROUP_WIDTH] for g in gs]
    acs_x = [both_x[g][:, GROUP_WIDTH:] for g in gs]
    tot_x = [acs_x[g][edge:edge + 1, :] for g in gs]
    xdt = [xs[g] * dt_x[g] for g in gs]

    y_diag = []
    for g in gs:
        tiles = []
        for t in range(GROUP_WIDTH // LANES):
            xt = xdt[g][:, t * LANES:(t + 1) * LANES]
            gmats, xparts = [], []
            for half in range(LANES // SSM_HEAD_DIM):
                j = t * (LANES // SSM_HEAD_DIM) + half
                seg = acs[g][:, j:j + 1] - acs_t[g][j:j + 1, :]
                gmats.append((cb[g] * jnp.exp(jnp.where(keep, seg, -jnp.inf))).astype(BF16))
                in_head = (lane >= half * SSM_HEAD_DIM) & (lane < (half + 1) * SSM_HEAD_DIM)
                xparts.append(jnp.where(in_head, xt, 0.0).astype(BF16))
            tiles.append(_dot(jnp.concatenate(gmats, axis=1), jnp.concatenate(xparts, axis=0)))
        y_diag.append(jnp.concatenate(tiles, axis=1))

    states = list(states)
    ys = []
    for g in gs:
        grp = groups[g]
        y_off = _dot(cms[g], states[grp].astype(BF16)) * jnp.exp(acs_x[g])
        ys.append(y_off + y_diag[g])
        w = (xdt[g] * jnp.exp(tot_x[g] - acs_x[g])).astype(BF16)
        states[grp] = jnp.exp(tot_x[g]) * states[grp] + _dot(bts[g], w)
    return ys, states


def _ssd_kernel(rev, nchunk, *refs):
    if rev:
        (xs_ref, bt_ref, c_ref, dtT_ref, alc_ref, bc_ref, yf_ref, z_ref, ng_ref, y_ref, state_ref) = refs
    else:
        (xs_ref, bt_ref, c_ref, dtT_ref, alc_ref, bc_ref, dx_ref, y_ref, state_ref) = refs

    @pl.when(pl.program_id(1) == 0)
    def _():
        state_ref[...] = jnp.zeros_like(state_ref)

    gw = GROUP_WIDTH
    gs = range(SSM_GROUPS)
    cols = [slice(g * gw, (g + 1) * gw) for g in gs]
    order = list(range(nchunk - 1, -1, -1) if rev else range(nchunk))
    states = [state_ref[g] for g in gs]
    for c0 in range(0, nchunk, SSD_CHUNKS_IN_FLIGHT):
        bodies = [(slice(ci * CHUNK, (ci + 1) * CHUNK), g) for ci in order[c0:c0 + SSD_CHUNKS_IN_FLIGHT] for g in gs]
        x = [xs_ref[rows, cols[g]] for rows, g in bodies]
        ys, states = _ssd_bodies(rev, x, [bt_ref[g, :, rows] for rows, g in bodies],
                                 [c_ref[rows, g * D_STATE:(g + 1) * D_STATE] for rows, g in bodies],
                                 [dtT_ref[g, :, rows] for rows, g in bodies],
                                 [alc_ref[g] for _, g in bodies], [bc_ref[g] for _, g in bodies],
                                 [g for _, g in bodies], states)
        for (rows, g), xb, y in zip(bodies, x, ys):
            if rev:
                yy = (yf_ref[rows, cols[g]] + y) * _silu(z_ref[rows, cols[g]])
                ms = jnp.mean(yy * yy, axis=-1, keepdims=True)
                y_ref[rows, cols[g]] = (yy * lax.rsqrt(ms + RMS_EPS) * ng_ref[g]).astype(BF16)
            else:
                y_ref[rows, cols[g]] = y + dx_ref[g] * xb
    for g in gs:
        state_ref[g] = states[g]


def _ssd_pass(rev, xs, bt, cm, dtT, a_log, dt_bias, extra, nchunk=8):
    b, s, _ = xs.shape
    tcs = nchunk * CHUNK
    nblk = s // tcs
    blk = (lambda i: nblk - 1 - i) if rev else (lambda i: i)
    dirn = 1 if rev else 0
    gw, hg, ng = GROUP_WIDTH, HEADS_PER_GROUP, SSM_GROUPS
    seq_spec = lambda w: pl.BlockSpec((None, tcs, w), lambda bi, i: (bi, blk(i), 0))
    in_specs = [
        seq_spec(D_SSM),
        pl.BlockSpec((None, ng, D_STATE, tcs), lambda bi, i: (bi, 0, 0, blk(i))),
        seq_spec(ng * D_STATE),
        pl.BlockSpec((ng, hg, tcs), lambda bi, i: (dirn, 0, bi * nblk + blk(i))),
        _const_spec((ng, hg, 1)), _const_spec((ng, hg, 1)),
    ]
    args = [xs, bt, cm, dtT, a_log.reshape(ng, hg, 1), dt_bias.reshape(ng, hg, 1)]
    if rev:
        y_fwd, z, norm_g = extra
        in_specs += [seq_spec(D_SSM), seq_spec(D_SSM), _const_spec((ng, 1, gw))]
        args += [y_fwd, z, norm_g.reshape(ng, 1, gw)]
    else:
        (d_skip,) = extra
        in_specs += [_const_spec((ng, 1, gw))]
        args += [jnp.repeat(d_skip, SSM_HEAD_DIM).reshape(ng, 1, gw)]
    return pl.pallas_call(
        functools.partial(_ssd_kernel, rev, nchunk),
        grid=(b, nblk),
        in_specs=in_specs,
        out_specs=seq_spec(D_SSM),
        out_shape=jax.ShapeDtypeStruct((b, s, D_SSM), BF16 if rev else F32),
        scratch_shapes=[pltpu.VMEM((ng, D_STATE, gw), F32)],
        compiler_params=pltpu.CompilerParams(dimension_semantics=("parallel", "arbitrary"),
                                             vmem_limit_bytes=VMEM_LIMIT),
        name="ssd_bwd" if rev else "ssd_fwd",
    )(*args)


NEG_BIG = -1e30


SKIP_LOG2 = 80.0
FIXED_SHIFT_MAX_LOG2 = 60.0
PV_DEPTH = 256


def _attn_plan(tq, tk, nk, nq, bi, hi, qi, slope, kn_ref, qn_ref, plan_i_ref, plan_f_ref):
    i0 = qi * tq
    jd = qi // (tk // tq)
    parts = tq // ATTN_QPART
    qns, bases = [], []
    for r in range(2):
        base = (bi * ATTN_HEADS + hi) * 2 + r
        qn = qn_ref[base * (nq * parts) + qi * parts]
        for part in range(1, parts):
            qn = jnp.maximum(qn, qn_ref[base * (nq * parts) + qi * parts + part])
        qns.append(qn)
        bases.append(base * nk)
    jlo, jhi = jd, jd
    kmax = [kn_ref[bases[0]], kn_ref[bases[1]]]
    for jt in range(nk):
        dist = jnp.maximum(jnp.maximum(i0 - (jt * tk + tk - 1), jt * tk - (i0 + tq - 1)), 0).astype(F32)
        need = None
        for r in range(2):
            kn = kn_ref[bases[r] + jt]
            kmax[r] = jnp.maximum(kmax[r], kn)
            reach = SKIP_LOG2 + qns[r] * (kn + kn_ref[bases[r] + jd])
            need_r = slope * dist <= reach
            need = need_r if need is None else (need | need_r)
        jlo = jnp.where(need, jnp.minimum(jlo, jt), jlo)
        jhi = jnp.where(need, jnp.maximum(jhi, jt), jhi)
    plan_i_ref[0] = jlo
    plan_i_ref[1] = jhi
    odd = (jhi - jlo + 1) % 2
    room_above = (jhi < nk - 1).astype(jnp.int32)
    plan_i_ref[2] = jlo - odd * (1 - room_above)
    plan_i_ref[3] = jhi + odd * room_above
    plan_f_ref[0] = jnp.maximum(qns[0] * kmax[0], qns[1] * kmax[1])


def _attn_q_tile(tq, tk, nk, nq, bi, hi, qi, slope, kn_ref, qn_ref, qT_ref, k_ref, vT_ref, lq1_ref, lk1_ref, lq2_ref,
                 lk2_ref, sg_ref, gate_ref, o_ref, e_ref, s_ref, s1_ref, acc_ref, l_ref, plan_i_ref, plan_f_ref, pre_ref):
    ratio = tk // tq
    i0 = qi * tq
    q0 = pl.multiple_of(i0, tq)
    jd = qi // ratio
    dsel = qi % ratio
    jlo, jhi, jl, jh, score_bound = plan_i_ref[0], plan_i_ref[1], plan_i_ref[2], plan_i_ref[3], plan_f_ref[0]
    first_scores_ready = pre_ref[0]
    qi_next = jnp.minimum(qi + 1, nq - 1)

    def plan_next():
        _attn_plan(tq, tk, nk, nq, bi, hi, qi_next, slope, kn_ref, qn_ref, plan_i_ref, plan_f_ref)

    row = lax.broadcasted_iota(jnp.int32, (2 * ATTN_HEAD_DIM, tq), 0)

    def masked_q(q_start):
        qf = qT_ref[:, pl.ds(q_start, tq)].astype(F32)
        return (jnp.where(row < ATTN_HEAD_DIM, qf, 0.0).astype(BF16),
                jnp.where(row >= ATTN_HEAD_DIM, qf, 0.0).astype(BF16))

    rhs = masked_q(q0)

    def finish_previous():
        _attn_epilogue(tq, jnp.maximum(qi - 1, 0), lq1_ref, lk1_ref, lq2_ref, lk2_ref, sg_ref, gate_ref, o_ref, acc_ref, l_ref)
        acc_ref[...] = jnp.zeros_like(acc_ref)

    def tile_bias(j):
        off = (i0 - j * tk).astype(F32)
        before, after = j < jd, j > jd
        eidx = jnp.where(before, 0, jnp.where(after, 1, 2 + dsel))
        cj = jnp.where(before, -slope * (off - float(tk - 1)), jnp.where(after, slope * (off + float(tq - 1)), 0.0))
        return eidx, cj

    @pl.when(score_bound <= FIXED_SHIFT_MAX_LOG2)
    def _():
        def stage_a(j, buf, w):
            kt = k_ref[pl.ds(pl.multiple_of(j * tk, tk), tk), :]
            for r in range(2):
                buf[r] = _dot(kt, w[r])

        @pl.when(first_scores_ready == 0)
        def _():
            stage_a(jl, s_ref, rhs)

        plan_next()
        jl_next = plan_i_ref[2]
        rhs_next = masked_q(pl.multiple_of(qi_next * tq, tq))

        def stage_b(j, buf, l):
            eidx, cj = tile_bias(j)
            half = jnp.exp2(jnp.full((1, tq), 0.5 * cj, F32))
            vt = vT_ref[:, pl.ds(pl.multiple_of(j * tk, tk), tk)]
            l_new = []
            for r in range(2):
                psum, pv = None, None
                for k0 in range(0, tk, PV_DEPTH):
                    p = jnp.exp2(buf[r, k0:k0 + PV_DEPTH, :] + e_ref[eidx, k0:k0 + PV_DEPTH, :])
                    ps = jnp.sum(p, axis=0, keepdims=True)
                    d = _dot(vt[:, k0:k0 + PV_DEPTH], p.astype(BF16))
                    psum, pv = (ps, d) if pv is None else (psum + ps, pv + d)
                l_new.append(l[r] + psum * half * half)
                acc_ref[r] += pv * half * half
            return tuple(l_new)

        def pair(j, l):
            stage_a(j + 1, s1_ref, rhs)
            l = stage_b(j, s_ref, l)
            done = j + 2 > jh
            w_next = tuple(jnp.where(done, rhs_next[r], rhs[r]) for r in range(2))
            stage_a(jnp.where(done, jl_next, j + 2), s_ref, w_next)
            return stage_b(j + 1, s1_ref, l)

        def body(t, l):
            j = jl + 2 + 4 * t
            return pair(j + 2, pair(j, l))

        finish_previous()
        zero = jnp.zeros((1, tq), F32)
        l = pair(jl, (zero, zero))
        rest = (jh - jl + 1) // 2 - 1
        l = lax.fori_loop(0, rest // 2, body, l)
        l_ref[0] = l[0]
        l_ref[1] = l[1]

        @pl.when(rest % 2 == 1)
        def _():
            l = pair(jh - 1, (l_ref[0], l_ref[1]))
            l_ref[0] = l[0]
            l_ref[1] = l[1]

        pre_ref[0] = 1

    def stage_a(j, m_old):
        eidx, cj = tile_bias(j)
        kt = k_ref[pl.ds(pl.multiple_of(j * tk, tk), tk), :]
        m_new, shift, alpha = [], [], []
        for r in range(2):
            s = _dot(kt, rhs[r]) + e_ref[eidx]
            s_ref[r] = s
            mr = jnp.maximum(m_old[r], jnp.max(s, axis=0, keepdims=True) + cj)
            m_new.append(mr)
            shift.append(mr - cj)
            alpha.append(jnp.exp2(m_old[r] - mr))
        return tuple(m_new), tuple(shift), tuple(alpha)

    def stage_b(j, shift, alpha, l_old):
        vt = vT_ref[:, pl.ds(pl.multiple_of(j * tk, tk), tk)]
        l_new = []
        for r in range(2):
            p = jnp.exp2(s_ref[r] - shift[r])
            l_new.append(alpha[r] * l_old[r] + jnp.sum(p, axis=0, keepdims=True))
            acc_ref[r] = alpha[r] * acc_ref[r] + _dot(vt, p.astype(BF16))
        return tuple(l_new)

    def body(t, carry):
        m, shift, alpha, l = carry
        j = jlo + t
        l = stage_b(jnp.maximum(j - 1, jlo), shift, alpha, l)
        m, shift, alpha = stage_a(jnp.minimum(j, jhi), m)
        return m, shift, alpha, l

    @pl.when(score_bound > FIXED_SHIFT_MAX_LOG2)
    def _():
        neg = jnp.full((1, tq), NEG_BIG, F32)
        big = jnp.full((1, tq), -NEG_BIG, F32)
        one = jnp.ones((1, tq), F32)
        zero = jnp.zeros((1, tq), F32)
        finish_previous()
        plan_next()
        pre_ref[0] = 0
        init = ((neg, neg), (big, big), (one, one), (zero, zero))
        _, _, _, l = lax.fori_loop(0, jhi - jlo + 2, body, init)
        l_ref[0] = l[0]
        l_ref[1] = l[1]


def _attn_epilogue(tq, qi, lq1_ref, lk1_ref, lq2_ref, lk2_ref, sg_ref, gate_ref, o_ref, acc_ref, l_ref):
    lam = (jnp.exp(jnp.sum(lq1_ref[...] * lk1_ref[...], axis=-1, keepdims=True))
           - jnp.exp(jnp.sum(lq2_ref[...] * lk2_ref[...], axis=-1, keepdims=True)) + LAM_INIT)
    o = acc_ref[0] / l_ref[0] - lam * (acc_ref[1] / l_ref[1])
    ms = jnp.mean(o * o, axis=0, keepdims=True)
    q0 = pl.multiple_of(qi * tq, tq)
    on = (o * lax.rsqrt(ms + RMS_EPS) * sg_ref[...] * (1.0 - LAM_INIT)).T
    o_ref[pl.ds(q0, tq), :] = (on * _silu(gate_ref[pl.ds(q0, tq), :])).astype(BF16)


def _attn_kernel(tq, tk, nk, nq, slopes_ref, kn_ref, qn_ref, qT_ref, k_ref, vT_ref, lq1_ref, lk1_ref, lq2_ref,
                 lk2_ref, sg_ref, gate_ref, o_ref, e_ref, s_ref, s1_ref, acc_ref, l_ref, plan_i_ref, plan_f_ref, pre_ref):
    bi, hi = pl.program_id(0), pl.program_id(1)
    slope = slopes_ref[hi]
    ii = lax.broadcasted_iota(jnp.int32, (tk, tq), 1)
    jj = lax.broadcasted_iota(jnp.int32, (tk, tq), 0)
    rel = (ii - jj).astype(F32)
    e_ref[0] = -slope * (rel + float(tk - 1))
    e_ref[1] = slope * (rel - float(tq - 1))
    for d in range(tk // tq):
        e_ref[2 + d] = -slope * jnp.abs(rel + float(d * tq))
    s_ref[...] = jnp.zeros_like(s_ref)
    acc_ref[...] = jnp.zeros_like(acc_ref)
    l_ref[...] = jnp.ones_like(l_ref)
    _attn_plan(tq, tk, nk, nq, bi, hi, 0, slope, kn_ref, qn_ref, plan_i_ref, plan_f_ref)
    pre_ref[0] = 0

    def q_tile(qi, carry):
        _attn_q_tile(tq, tk, nk, nq, bi, hi, qi, slope, kn_ref, qn_ref, qT_ref, k_ref, vT_ref, lq1_ref, lk1_ref,
                     lq2_ref, lk2_ref, sg_ref, gate_ref, o_ref, e_ref, s_ref, s1_ref, acc_ref, l_ref, plan_i_ref, plan_f_ref,
                     pre_ref)
        return carry

    lax.fori_loop(0, nq, q_tile, 0)
    _attn_epilogue(tq, nq - 1, lq1_ref, lk1_ref, lq2_ref, lk2_ref, sg_ref, gate_ref, o_ref, acc_ref, l_ref)


def _diff_attn(qT, k3, vT, kn, qn, lq1, lk1, lq2, lk2, subln_g, gate3, tq=ATTN_TQ, tk=ATTN_TK):
    b, s, _ = k3.shape
    nq, nk = s // tq, s // tk
    ratio = tk // tq
    assert tk % tq == 0 and nk % 2 == 0
    hw = 2 * ATTN_HEAD_DIM
    slopes = jnp.exp2(-8.0 * (jnp.arange(ATTN_HEADS, dtype=F32) + 1.0) / ATTN_HEADS) * LOG2E
    kn_tab = jnp.transpose(kn[:, :, 0].reshape(b, nk, N_MAPS), (0, 2, 1)).reshape(-1)
    kparts = tk // ATTN_QPART
    qn_tab = jnp.transpose(qn[:, :, :kparts].reshape(b, nk, N_MAPS, kparts), (0, 2, 1, 3)).reshape(-1)
    vec = lambda a: a.reshape(1, -1)
    return pl.pallas_call(
        functools.partial(_attn_kernel, tq, tk, nk, nq),
        grid=(b, ATTN_HEADS),
        in_specs=[
            pl.BlockSpec(memory_space=pltpu.SMEM),
            pl.BlockSpec(memory_space=pltpu.SMEM),
            pl.BlockSpec(memory_space=pltpu.SMEM),
            pl.BlockSpec((hw, s), lambda bi, hi: (hi, bi)),
            pl.BlockSpec((None, s, hw), lambda bi, hi: (bi, 0, hi)),
            pl.BlockSpec((ATTN_V_DIM, s), lambda bi, hi: (hi, bi)),
            _const_spec((1, ATTN_HEAD_DIM)), _const_spec((1, ATTN_HEAD_DIM)),
            _const_spec((1, ATTN_HEAD_DIM)), _const_spec((1, ATTN_HEAD_DIM)),
            _const_spec((ATTN_V_DIM, 1)),
            pl.BlockSpec((None, s, ATTN_V_DIM), lambda bi, hi: (bi, 0, hi)),
        ],
        out_specs=pl.BlockSpec((None, s, ATTN_V_DIM), lambda bi, hi: (bi, 0, hi)),
        out_shape=jax.ShapeDtypeStruct((b, s, ATTN_HEADS * ATTN_V_DIM), BF16),
        scratch_shapes=[pltpu.VMEM((2 + tk // tq, tk, tq), F32),
                        pltpu.VMEM((2, tk, tq), F32), pltpu.VMEM((2, tk, tq), F32),
                        pltpu.VMEM((2, ATTN_V_DIM, tq), F32), pltpu.VMEM((2, 1, tq), F32),
                        pltpu.SMEM((4,), jnp.int32), pltpu.SMEM((1,), F32), pltpu.SMEM((1,), jnp.int32)],
        compiler_params=pltpu.CompilerParams(dimension_semantics=("parallel", "parallel"),
                                             vmem_limit_bytes=VMEM_LIMIT),
        name="diff_attn",
    )(slopes, kn_tab, qn_tab, qT, k3, vT, vec(lq1), vec(lk1), vec(lq2), vec(lk2), subln_g.reshape(-1, 1), gate3)


def _out_proj_kernel(x_ref, eg_ref, eb_ref, ys_ref, ya_ref, w_ref, g_ref, b_ref, out_ref):
    h = _layer_norm(x_ref[...], eg_ref[...], eb_ref[...])
    mix = _dot(ys_ref[...], w_ref[:D_SSM, :]) + _dot(ya_ref[...], w_ref[D_SSM:, :])
    out_ref[...] = _layer_norm(ALPHA * h + mix, g_ref[...], b_ref[...])


def _out_proj(x2d, ln_emb_g, ln_emb_b, y_ssm, y_attn, w_out, ln_g, ln_b, tm=512):
    t = x2d.shape[0]
    row = lambda n: pl.BlockSpec((tm, n), lambda i: (i, 0))
    vec = lambda a: a.reshape(1, -1)
    return pl.pallas_call(
        _out_proj_kernel,
        grid=(t // tm,),
        in_specs=[row(D_MODEL), _const_spec((1, D_MODEL)), _const_spec((1, D_MODEL)),
                  row(D_SSM), row(D_ATTN),
                  _const_spec((D_SSM + D_ATTN, D_MODEL)), _const_spec((1, D_MODEL)), _const_spec((1, D_MODEL))],
        out_specs=row(D_MODEL),
        out_shape=jax.ShapeDtypeStruct((t, D_MODEL), F32),
        compiler_params=pltpu.CompilerParams(dimension_semantics=("parallel",), vmem_limit_bytes=VMEM_LIMIT),
        name="out_proj",
    )(x2d, vec(ln_emb_g), vec(ln_emb_b), y_ssm, y_attn, w_out.astype(BF16), vec(ln_g), vec(ln_b))


def kernel(x, ln_emb_g, ln_emb_b, w_in, conv_w, conv_b, A_log_fwd, A_log_bwd, dt_bias_fwd, dt_bias_bwd, D_skip,
           ssm_norm_g, lambda_q1, lambda_k1, lambda_q2, lambda_k2, subln_g, w_out, ln_g, ln_b):
    b, s, _ = x.shape
    t = b * s
    x2d = x.reshape(t, D_MODEL)
    z, xbc, dtT, qT, k, vT, gate, kn, qn = _ln_inproj(x2d, ln_emb_g, ln_emb_b, w_in[0])

    xs, bt, cm = _conv_silu(xbc.reshape(b, s, D_XBC), conv_w[0], conv_b[0])
    y_fwd = _ssd_pass(False, xs, bt, cm, dtT, A_log_fwd[0], dt_bias_fwd[0], (D_skip[0],))
    y_ssm = _ssd_pass(True, xs, bt, cm, dtT, A_log_bwd[0], dt_bias_bwd[0],
                      (y_fwd, z.reshape(b, s, D_SSM), ssm_norm_g[0]))

    y_attn = _diff_attn(qT, k.reshape(b, s, D_ATTN), vT, kn, qn, lambda_q1[0], lambda_k1[0], lambda_q2[0],
                        lambda_k2[0], subln_g[0], gate.reshape(b, s, D_ATTN)).reshape(t, D_ATTN)

    out = _out_proj(x2d, ln_emb_g, ln_emb_b, y_ssm.reshape(t, D_SSM), y_attn, w_out[0], ln_g[0], ln_b[0])
    return out.reshape(b, s, D_MODEL)
```

```python
import functools
import math

import jax
import jax.numpy as jnp
from jax import lax
from jax.experimental import pallas as pl
from jax.experimental.pallas import tpu as pltpu

D_MODEL = 1024
D_SSM = 1024
SSM_HEAD_DIM = 64
SSM_HEADS = 16
SSM_GROUPS = 2
HEADS_PER_GROUP = SSM_HEADS // SSM_GROUPS
D_STATE = 128
D_CONV = 5
CHUNK = 128
D_XBC = D_SSM + 2 * SSM_GROUPS * D_STATE
D_ATTN = 1024
ATTN_HEADS = 8
ATTN_HEAD_DIM = 64
ATTN_V_DIM = 128
GROUP_WIDTH = D_SSM // SSM_GROUPS
DEPTH = 1
ALPHA = (2.0 * DEPTH) ** 0.25
LN_EPS = 1e-5
RMS_EPS = 1e-5
LAM_INIT = 0.8 - 0.6 * math.exp(-0.3 * 0)

LANES = 128
VMEM_LIMIT = 48 * 1024 * 1024

F32 = jnp.float32
BF16 = jnp.bfloat16


LOG2E = math.log2(math.e)


def _dot(a, b):
    return jnp.dot(a, b, preferred_element_type=F32)


def _dot_nt(a, b):
    return lax.dot_general(a, b, (((1,), (1,)), ((), ())), preferred_element_type=F32)


def _layer_norm(x, g, b):
    mu = jnp.mean(x, axis=-1, keepdims=True)
    xc = x - mu
    var = jnp.mean(xc * xc, axis=-1, keepdims=True)
    return xc * lax.rsqrt(var + LN_EPS) * g + b


def _silu(x):
    hx = 0.5 * x
    return hx + hx * jnp.tanh(hx)


def _softplus(x):
    return jnp.maximum(x, 0.0) + jnp.log1p(jnp.exp(-jnp.abs(x)))


def _split3(x):
    hi = x.astype(BF16)
    r = x - hi.astype(F32)
    mid = r.astype(BF16)
    lo = (r - mid.astype(F32)).astype(BF16)
    return hi, mid, lo


def _dot_exact_lhs(x, m):
    hi, mid, lo = _split3(x)
    return _dot(hi, m) + _dot(mid, m) + _dot(lo, m)


def _const_spec(shape):
    nd = len(shape)
    return pl.BlockSpec(shape, lambda *_: (0,) * nd)


N_DT_TILES = 2 * SSM_GROUPS


ATTN_TQ = 256
ATTN_QPART = 256
ATTN_TK = 512
N_MAPS = 2 * ATTN_HEADS
NORM_SLACK = 1.01


def _ln_inproj_kernel(x_ref, g_ref, b_ref, wz_ref, wxbc_ref, wdtT_ref, wqT_ref, wk_ref, wvT_ref, wg_ref,
                      grpT_ref,
                      z_ref, xbc_ref, dtT_ref, qT_ref, k_ref, vT_ref, gate_ref, kn_ref, qn_ref):
    h = _layer_norm(x_ref[...], g_ref[...], b_ref[...]).astype(BF16)
    z_ref[...] = _dot(h, wz_ref[...])
    xbc_ref[...] = _dot(h, wxbc_ref[...])
    dtT_ref[...] = _dot_nt(wdtT_ref[...], h).reshape(dtT_ref.shape)
    qT = _dot_nt(wqT_ref[...], h) * (LOG2E * ATTN_HEAD_DIM ** -0.5)
    qT_ref[...] = qT.astype(BF16)
    k = _dot(h, wk_ref[...])
    k_ref[...] = k.astype(BF16)
    vT_ref[...] = _dot_nt(wvT_ref[...], h).astype(BF16)
    gate_ref[...] = _dot(h, wg_ref[...])
    lane = lax.broadcasted_iota(jnp.int32, (N_MAPS, LANES), 1)
    k2 = _dot_nt(grpT_ref[...], (k * k).astype(BF16))
    kn = jnp.sqrt(jnp.max(k2, axis=1, keepdims=True)) * NORM_SLACK
    kn_ref[...] = jnp.where(lane == 0, kn, 0.0)
    q2 = _dot(grpT_ref[...], (qT * qT).astype(BF16))
    qn = jnp.zeros((N_MAPS, LANES), F32)
    for part in range(q2.shape[1] // ATTN_QPART):
        pm = jnp.max(q2[:, part * ATTN_QPART:(part + 1) * ATTN_QPART], axis=1, keepdims=True)
        qn = jnp.where(lane == part, jnp.sqrt(pm) * NORM_SLACK, qn)
    qn_ref[...] = qn


def _ln_inproj(x2d, ln_g, ln_b, w_in, tm=ATTN_TK):
    t = x2d.shape[0]
    grp_t = (jnp.arange(N_MAPS)[:, None] == jnp.arange(D_ATTN)[None, :] // ATTN_HEAD_DIM).astype(BF16)
    offs = [0, D_SSM, D_SSM + D_XBC, D_SSM + D_XBC + 2 * SSM_HEADS]
    wz = w_in[:, offs[0]:offs[1]].astype(BF16)
    wxbc = w_in[:, offs[1]:offs[2]].astype(BF16)
    wdt = w_in[:, offs[2]:offs[3]].astype(BF16)
    o = offs[3]
    wq, wk, wv, wg = (w_in[:, o + i * D_ATTN:o + (i + 1) * D_ATTN].astype(BF16) for i in range(4))
    row = lambda n: pl.BlockSpec((tm, n), lambda i: (i, 0))
    col = lambda n: pl.BlockSpec((n, tm), lambda i: (0, i))
    weights = [wz, wxbc, wdt.T, wq.T, wk, wv.T, wg, grp_t]
    nt = t // tm
    return pl.pallas_call(
        _ln_inproj_kernel,
        grid=(nt,),
        in_specs=[row(D_MODEL), _const_spec((1, D_MODEL)), _const_spec((1, D_MODEL))]
                 + [pl.BlockSpec(w.shape, lambda i: (0, 0), pipeline_mode=pl.Buffered(1)) for w in weights],
        out_specs=[row(D_SSM), row(D_XBC),
                   pl.BlockSpec((N_DT_TILES, HEADS_PER_GROUP, tm), lambda i: (0, 0, i)),
                   col(D_ATTN), row(D_ATTN), col(D_ATTN), row(D_ATTN),
                   pl.BlockSpec((None, N_MAPS, LANES), lambda i: (i, 0, 0)),
                   pl.BlockSpec((None, N_MAPS, LANES), lambda i: (i, 0, 0))],
        out_shape=[jax.ShapeDtypeStruct((t, D_SSM), F32), jax.ShapeDtypeStruct((t, D_XBC), F32),
                   jax.ShapeDtypeStruct((N_DT_TILES, HEADS_PER_GROUP, t), F32),
                   jax.ShapeDtypeStruct((D_ATTN, t), BF16), jax.ShapeDtypeStruct((t, D_ATTN), BF16),
                   jax.ShapeDtypeStruct((D_ATTN, t), BF16), jax.ShapeDtypeStruct((t, D_ATTN), F32),
                   jax.ShapeDtypeStruct((nt, N_MAPS, LANES), F32), jax.ShapeDtypeStruct((nt, N_MAPS, LANES), F32)],
        compiler_params=pltpu.CompilerParams(dimension_semantics=("parallel",), vmem_limit_bytes=VMEM_LIMIT),
        name="ln_inproj",
    )(x2d, ln_g.reshape(1, -1), ln_b.reshape(1, -1), *weights)


HALO = 8


CONV_ROWS = 128


def _conv_silu_kernel(prev_ref, cur_ref, next_ref, w_ref, b_ref, xs_ref, bt_ref, c_ref, ext_ref):
    i = pl.program_id(1)
    n = pl.num_programs(1)
    tc = cur_ref.shape[0]
    ext_ref[0:HALO, :] = jnp.where(i == 0, 0.0, prev_ref[...])
    ext_ref[HALO:HALO + tc, :] = cur_ref[...]
    ext_ref[HALO + tc:, :] = jnp.where(i == n - 1, 0.0, next_ref[...])
    for cb in range(D_XBC // LANES):
        cols = slice(cb * LANES, (cb + 1) * LANES)
        for r0 in range(0, tc, CONV_ROWS):
            acc = b_ref[:, cols]
            for kk in range(D_CONV):
                acc = acc + ext_ref[pl.ds(r0 + HALO - D_CONV // 2 + kk, CONV_ROWS), cols] * w_ref[kk:kk + 1, cols]
            act = _silu(acc)
            rows = slice(r0, r0 + CONV_ROWS)
            if cb < D_SSM // LANES:
                xs_ref[rows, cols] = act
            elif cb < D_SSM // LANES + SSM_GROUPS:
                bt_ref[cb - D_SSM // LANES, :, rows] = act.T.astype(BF16)
            else:
                c_ref[rows, (cb - D_SSM // LANES - SSM_GROUPS) * LANES:(cb - D_SSM // LANES - SSM_GROUPS + 1) * LANES] = (
                    act.astype(BF16))


def _conv_silu(xbc, conv_w, conv_b, tc=1024):
    b, s, _ = xbc.shape
    nb = s // tc
    hb = tc // HALO
    return pl.pallas_call(
        _conv_silu_kernel,
        grid=(b, nb),
        in_specs=[
            pl.BlockSpec((None, HALO, D_XBC), lambda bi, i: (bi, jnp.maximum(i * hb - 1, 0), 0)),
            pl.BlockSpec((None, tc, D_XBC), lambda bi, i: (bi, i, 0)),
            pl.BlockSpec((None, HALO, D_XBC), lambda bi, i: (bi, jnp.minimum((i + 1) * hb, s // HALO - 1), 0)),
            _const_spec((D_CONV, D_XBC)),
            _const_spec((1, D_XBC)),
        ],
        out_specs=[
            pl.BlockSpec((None, tc, D_SSM), lambda bi, i: (bi, i, 0)),
            pl.BlockSpec((None, SSM_GROUPS, D_STATE, tc), lambda bi, i: (bi, 0, 0, i)),
            pl.BlockSpec((None, tc, SSM_GROUPS * D_STATE), lambda bi, i: (bi, i, 0)),
        ],
        out_shape=[
            jax.ShapeDtypeStruct((b, s, D_SSM), F32),
            jax.ShapeDtypeStruct((b, SSM_GROUPS, D_STATE, s), BF16),
            jax.ShapeDtypeStruct((b, s, SSM_GROUPS * D_STATE), BF16),
        ],
        scratch_shapes=[pltpu.VMEM((tc + 2 * HALO, D_XBC), F32)],
        compiler_params=pltpu.CompilerParams(dimension_semantics=("parallel", "parallel"),
                                             vmem_limit_bytes=VMEM_LIMIT),
        name="conv_silu",
    )(xbc, xbc, xbc, conv_w, conv_b.reshape(1, -1))


SSD_CHUNKS_IN_FLIGHT = 8


def _ssd_bodies(rev, xs, bts, cms, dtTs, alogs, biases, groups, states):
    L = CHUNK
    r = lax.broadcasted_iota(jnp.int32, (L, L), 0)
    c = lax.broadcasted_iota(jnp.int32, (L, L), 1)
    keep = (c >= r) if rev else (c <= r)
    cum_r = ((r >= c) if rev else (r <= c)).astype(BF16)
    hg = HEADS_PER_GROUP
    eh = lax.broadcasted_iota(jnp.int32, (6 * hg, 2 * GROUP_WIDTH), 0)
    el = lax.broadcasted_iota(jnp.int32, (6 * hg, 2 * GROUP_WIDTH), 1)
    spread = ((el // GROUP_WIDTH == eh // (3 * hg)) & ((el % GROUP_WIDTH) // SSM_HEAD_DIM == eh % hg)).astype(BF16)
    lane = lax.broadcasted_iota(jnp.int32, (L, LANES), 1)
    edge = 0 if rev else L - 1
    gs = range(len(xs))

    dt_t = [_softplus(dtTs[g] + biases[g]) for g in gs]
    acs_t = [_dot_exact_lhs(dt_t[g] * (-jnp.exp(alogs[g])), cum_r) for g in gs]
    pieces_t = [jnp.concatenate([p.astype(F32) for p in _split3(dt_t[g]) + _split3(acs_t[g])], axis=0) for g in gs]
    both_x = [_dot(pieces_t[g].T.astype(BF16), spread) for g in gs]
    acs = [acs_t[g].T for g in gs]
    cb = [_dot(cms[g], bts[g]) for g in gs]
    dt_x = [both_x[g][:, :GROUP_WIDTH] for g in gs]
    acs_x = [both_x[g][:, GROUP_WIDTH:] for g in gs]
    tot_x = [acs_x[g][edge:edge + 1, :] for g in gs]
    xdt = [xs[g] * dt_x[g] for g in gs]

    y_diag = []
    for g in gs:
        tiles = []
        for t in range(GROUP_WIDTH // LANES):
            xt = xdt[g][:, t * LANES:(t + 1) * LANES]
            gmats, xparts = [], []
            for half in range(LANES // SSM_HEAD_DIM):
                j = t * (LANES // SSM_HEAD_DIM) + half
                seg = acs[g][:, j:j + 1] - acs_t[g][j:j + 1, :]
                gmats.append((cb[g] * jnp.exp(jnp.where(keep, seg, -jnp.inf))).astype(BF16))
                in_head = (lane >= half * SSM_HEAD_DIM) & (lane < (half + 1) * SSM_HEAD_DIM)
                xparts.append(jnp.where(in_head, xt, 0.0).astype(BF16))
            tiles.append(_dot(jnp.concatenate(gmats, axis=1), jnp.concatenate(xparts, axis=0)))
        y_diag.append(jnp.concatenate(tiles, axis=1))

    states = list(states)
    ys = []
    for g in gs:
        grp = groups[g]
        y_off = _dot(cms[g], states[grp].astype(BF16)) * jnp.exp(acs_x[g])
        ys.append(y_off + y_diag[g])
        w = (xdt[g] * jnp.exp(tot_x[g] - acs_x[g])).astype(BF16)
        states[grp] = jnp.exp(tot_x[g]) * states[grp] + _dot(bts[g], w)
    return ys, states


def _ssd_kernel(rev, nchunk, *refs):
    if rev:
        (xs_ref, bt_ref, c_ref, dtT_ref, alc_ref, bc_ref, yf_ref, z_ref, ng_ref, y_ref, state_ref) = refs
    else:
        (xs_ref, bt_ref, c_ref, dtT_ref, alc_ref, bc_ref, dx_ref, y_ref, state_ref) = refs

    @pl.when(pl.program_id(1) == 0)
    def _():
        state_ref[...] = jnp.zeros_like(state_ref)

    gw = GROUP_WIDTH
    gs = range(SSM_GROUPS)
    cols = [slice(g * gw, (g + 1) * gw) for g in gs]
    order = list(range(nchunk - 1, -1, -1) if rev else range(nchunk))
    states = [state_ref[g] for g in gs]
    for c0 in range(0, nchunk, SSD_CHUNKS_IN_FLIGHT):
        bodies = [(slice(ci * CHUNK, (ci + 1) * CHUNK), g) for ci in order[c0:c0 + SSD_CHUNKS_IN_FLIGHT] for g in gs]
        x = [xs_ref[rows, cols[g]] for rows, g in bodies]
        ys, states = _ssd_bodies(rev, x, [bt_ref[g, :, rows] for rows, g in bodies],
                                 [c_ref[rows, g * D_STATE:(g + 1) * D_STATE] for rows, g in bodies],
                                 [dtT_ref[g, :, rows] for rows, g in bodies],
                                 [alc_ref[g] for _, g in bodies], [bc_ref[g] for _, g in bodies],
                                 [g for _, g in bodies], states)
        for (rows, g), xb, y in zip(bodies, x, ys):
            if rev:
                yy = (yf_ref[rows, cols[g]] + y) * _silu(z_ref[rows, cols[g]])
                ms = jnp.mean(yy * yy, axis=-1, keepdims=True)
                y_ref[rows, cols[g]] = (yy * lax.rsqrt(ms + RMS_EPS) * ng_ref[g]).astype(BF16)
            else:
                y_ref[rows, cols[g]] = y + dx_ref[g] * xb
    for g in gs:
        state_ref[g] = states[g]


def _ssd_pass(rev, xs, bt, cm, dtT, a_log, dt_bias, extra, nchunk=8):
    b, s, _ = xs.shape
    tcs = nchunk * CHUNK
    nblk = s // tcs
    blk = (lambda i: nblk - 1 - i) if rev else (lambda i: i)
    dirn = 1 if rev else 0
    gw, hg, ng = GROUP_WIDTH, HEADS_PER_GROUP, SSM_GROUPS
    seq_spec = lambda w: pl.BlockSpec((None, tcs, w), lambda bi, i: (bi, blk(i), 0))
    in_specs = [
        seq_spec(D_SSM),
        pl.BlockSpec((None, ng, D_STATE, tcs), lambda bi, i: (bi, 0, 0, blk(i))),
        seq_spec(ng * D_STATE),
        pl.BlockSpec((ng, hg, tcs), lambda bi, i: (dirn, 0, bi * nblk + blk(i))),
        _const_spec((ng, hg, 1)), _const_spec((ng, hg, 1)),
    ]
    args = [xs, bt, cm, dtT, a_log.reshape(ng, hg, 1), dt_bias.reshape(ng, hg, 1)]
    if rev:
        y_fwd, z, norm_g = extra
        in_specs += [seq_spec(D_SSM), seq_spec(D_SSM), _const_spec((ng, 1, gw))]
        args += [y_fwd, z, norm_g.reshape(ng, 1, gw)]
    else:
        (d_skip,) = extra
        in_specs += [_const_spec((ng, 1, gw))]
        args += [jnp.repeat(d_skip, SSM_HEAD_DIM).reshape(ng, 1, gw)]
    return pl.pallas_call(
        functools.partial(_ssd_kernel, rev, nchunk),
        grid=(b, nblk),
        in_specs=in_specs,
        out_specs=seq_spec(D_SSM),
        out_shape=jax.ShapeDtypeStruct((b, s, D_SSM), BF16 if rev else F32),
        scratch_shapes=[pltpu.VMEM((ng, D_STATE, gw), F32)],
        compiler_params=pltpu.CompilerParams(dimension_semantics=("parallel", "arbitrary"),
                                             vmem_limit_bytes=VMEM_LIMIT),
        name="ssd_bwd" if rev else "ssd_fwd",
    )(*args)


NEG_BIG = -1e30


SKIP_LOG2 = 80.0
FIXED_SHIFT_MAX_LOG2 = 60.0
PV_DEPTH = 256


def _attn_plan(tq, tk, nk, nq, bi, hi, qi, slope, kn_ref, qn_ref, plan_i_ref, plan_f_ref):
    i0 = qi * tq
    jd = qi // (tk // tq)
    parts = tq // ATTN_QPART
    qns, bases = [], []
    for r in range(2):
        base = (bi * ATTN_HEADS + hi) * 2 + r
        qn = qn_ref[base * (nq * parts) + qi * parts]
        for part in range(1, parts):
            qn = jnp.maximum(qn, qn_ref[base * (nq * parts) + qi * parts + part])
        qns.append(qn)
        bases.append(base * nk)
    jlo, jhi = jd, jd
    kmax = [kn_ref[bases[0]], kn_ref[bases[1]]]
    for jt in range(nk):
        dist = jnp.maximum(jnp.maximum(i0 - (jt * tk + tk - 1), jt * tk - (i0 + tq - 1)), 0).astype(F32)
        need = None
        for r in range(2):
            kn = kn_ref[bases[r] + jt]
            kmax[r] = jnp.maximum(kmax[r], kn)
            reach = SKIP_LOG2 + qns[r] * (kn + kn_ref[bases[r] + jd])
            need_r = slope * dist <= reach
            need = need_r if need is None else (need | need_r)
        jlo = jnp.where(need, jnp.minimum(jlo, jt), jlo)
        jhi = jnp.where(need, jnp.maximum(jhi, jt), jhi)
    plan_i_ref[0] = jlo
    plan_i_ref[1] = jhi
    odd = (jhi - jlo + 1) % 2
    room_above = (jhi < nk - 1).astype(jnp.int32)
    plan_i_ref[2] = jlo - odd * (1 - room_above)
    plan_i_ref[3] = jhi + odd * room_above
    plan_f_ref[0] = jnp.maximum(qns[0] * kmax[0], qns[1] * kmax[1])


def _attn_q_tile(tq, tk, nk, nq, bi, hi, qi, slope, kn_ref, qn_ref, qT_ref, k_ref, vT_ref, lq1_ref, lk1_ref, lq2_ref,
                 lk2_ref, sg_ref, gate_ref, o_ref, e_ref, s_ref, s1_ref, acc_ref, l_ref, plan_i_ref, plan_f_ref, pre_ref):
    ratio = tk // tq
    i0 = qi * tq
    q0 = pl.multiple_of(i0, tq)
    jd = qi // ratio
    dsel = qi % ratio
    jlo, jhi, jl, jh, score_bound = plan_i_ref[0], plan_i_ref[1], plan_i_ref[2], plan_i_ref[3], plan_f_ref[0]
    first_scores_ready = pre_ref[0]
    qi_next = jnp.minimum(qi + 1, nq - 1)

    def plan_next():
        _attn_plan(tq, tk, nk, nq, bi, hi, qi_next, slope, kn_ref, qn_ref, plan_i_ref, plan_f_ref)

    row = lax.broadcasted_iota(jnp.int32, (2 * ATTN_HEAD_DIM, tq), 0)

    def masked_q(q_start):
        qf = qT_ref[:, pl.ds(q_start, tq)].astype(F32)
        return (jnp.where(row < ATTN_HEAD_DIM, qf, 0.0).astype(BF16),
                jnp.where(row >= ATTN_HEAD_DIM, qf, 0.0).astype(BF16))

    rhs = masked_q(q0)

    def finish_previous():
        _attn_epilogue(tq, jnp.maximum(qi - 1, 0), lq1_ref, lk1_ref, lq2_ref, lk2_ref, sg_ref, gate_ref, o_ref, acc_ref, l_ref)
        acc_ref[...] = jnp.zeros_like(acc_ref)

    def tile_bias(j):
        off = (i0 - j * tk).astype(F32)
        before, after = j < jd, j > jd
        eidx = jnp.where(before, 0, jnp.where(after, 1, 2 + dsel))
        cj = jnp.where(before, -slope * (off - float(tk - 1)), jnp.where(after, slope * (off + float(tq - 1)), 0.0))
        return eidx, cj

    @pl.when(score_bound <= FIXED_SHIFT_MAX_LOG2)
    def _():
        def stage_a(j, buf, w):
            kt = k_ref[pl.ds(pl.multiple_of(j * tk, tk), tk), :]
            for r in range(2):
                buf[r] = _dot(kt, w[r])

        @pl.when(first_scores_ready == 0)
        def _():
            stage_a(jl, s_ref, rhs)

        plan_next()
        jl_next = plan_i_ref[2]
        rhs_next = masked_q(pl.multiple_of(qi_next * tq, tq))

        def stage_b(j, buf, l):
            eidx, cj = tile_bias(j)
            half = jnp.exp2(jnp.full((1, tq), 0.5 * cj, F32))
            vt = vT_ref[:, pl.ds(pl.multiple_of(j * tk, tk), tk)]
            l_new = []
            for r in range(2):
                psum, pv = None, None
                for k0 in range(0, tk, PV_DEPTH):
                    p = jnp.exp2(buf[r, k0:k0 + PV_DEPTH, :] + e_ref[eidx, k0:k0 + PV_DEPTH, :])
                    ps = jnp.sum(p, axis=0, keepdims=True)
                    d = _dot(vt[:, k0:k0 + PV_DEPTH], p.astype(BF16))
                    psum, pv = (ps, d) if pv is None else (psum + ps, pv + d)
                l_new.append(l[r] + psum * half * half)
                acc_ref[r] += pv * half * half
            return tuple(l_new)

        def pair(j, l):
            stage_a(j + 1, s1_ref, rhs)
            l = stage_b(j, s_ref, l)
            done = j + 2 > jh
            w_next = tuple(jnp.where(done, rhs_next[r], rhs[r]) for r in range(2))
            stage_a(jnp.where(done, jl_next, j + 2), s_ref, w_next)
            return stage_b(j + 1, s1_ref, l)

        def body(t, l):
            j = jl + 2 + 4 * t
            return pair(j + 2, pair(j, l))

        finish_previous()
        zero = jnp.zeros((1, tq), F32)
        l = pair(jl, (zero, zero))
        rest = (jh - jl + 1) // 2 - 1
        l = lax.fori_loop(0, rest // 2, body, l)
        l_ref[0] = l[0]
        l_ref[1] = l[1]

        @pl.when(rest % 2 == 1)
        def _():
            l = pair(jh - 1, (l_ref[0], l_ref[1]))
            l_ref[0] = l[0]
            l_ref[1] = l[1]

        pre_ref[0] = 1

    def stage_a(j, m_old):
        eidx, cj = tile_bias(j)
        kt = k_ref[pl.ds(pl.multiple_of(j * tk, tk), tk), :]
        m_new, shift, alpha = [], [], []
        for r in range(2):
            s = _dot(kt, rhs[r]) + e_ref[eidx]
            s_ref[r] = s
            mr = jnp.maximum(m_old[r], jnp.max(s, axis=0, keepdims=True) + cj)
            m_new.append(mr)
            shift.append(mr - cj)
            alpha.append(jnp.exp2(m_old[r] - mr))
        return tuple(m_new), tuple(shift), tuple(alpha)

    def stage_b(j, shift, alpha, l_old):
        vt = vT_ref[:, pl.ds(pl.multiple_of(j * tk, tk), tk)]
        l_new = []
        for r in range(2):
            p = jnp.exp2(s_ref[r] - shift[r])
            l_new.append(alpha[r] * l_old[r] + jnp.sum(p, axis=0, keepdims=True))
            acc_ref[r] = alpha[r] * acc_ref[r] + _dot(vt, p.astype(BF16))
        return tuple(l_new)

    def body(t, carry):
        m, shift, alpha, l = carry
        j = jlo + t
        l = stage_b(jnp.maximum(j - 1, jlo), shift, alpha, l)
        m, shift, alpha = stage_a(jnp.minimum(j, jhi), m)
        return m, shift, alpha, l

    @pl.when(score_bound > FIXED_SHIFT_MAX_LOG2)
    def _():
        neg = jnp.full((1, tq), NEG_BIG, F32)
        big = jnp.full((1, tq), -NEG_BIG, F32)
        one = jnp.ones((1, tq), F32)
        zero = jnp.zeros((1, tq), F32)
        finish_previous()
        plan_next()
        pre_ref[0] = 0
        init = ((neg, neg), (big, big), (one, one), (zero, zero))
        _, _, _, l = lax.fori_loop(0, jhi - jlo + 2, body, init)
        l_ref[0] = l[0]
        l_ref[1] = l[1]


def _attn_epilogue(tq, qi, lq1_ref, lk1_ref, lq2_ref, lk2_ref, sg_ref, gate_ref, o_ref, acc_ref, l_ref):
    lam = (jnp.exp(jnp.sum(lq1_ref[...] * lk1_ref[...], axis=-1, keepdims=True))
           - jnp.exp(jnp.sum(lq2_ref[...] * lk2_ref[...], axis=-1, keepdims=True)) + LAM_INIT)
    o = acc_ref[0] / l_ref[0] - lam * (acc_ref[1] / l_ref[1])
    ms = jnp.mean(o * o, axis=0, keepdims=True)
    q0 = pl.multiple_of(qi * tq, tq)
    on = (o * lax.rsqrt(ms + RMS_EPS) * sg_ref[...] * (1.0 - LAM_INIT)).T
    o_ref[pl.ds(q0, tq), :] = (on * _silu(gate_ref[pl.ds(q0, tq), :])).astype(BF16)


def _attn_kernel(tq, tk, nk, nq, slopes_ref, kn_ref, qn_ref, qT_ref, k_ref, vT_ref, lq1_ref, lk1_ref, lq2_ref,
                 lk2_ref, sg_ref, gate_ref, o_ref, e_ref, s_ref, s1_ref, acc_ref, l_ref, plan_i_ref, plan_f_ref, pre_ref):
    bi, hi = pl.program_id(0), pl.program_id(1)
    slope = slopes_ref[hi]
    ii = lax.broadcasted_iota(jnp.int32, (tk, tq), 1)
    jj = lax.broadcasted_iota(jnp.int32, (tk, tq), 0)
    rel = (ii - jj).astype(F32)
    e_ref[0] = -slope * (rel + float(tk - 1))
    e_ref[1] = slope * (rel - float(tq - 1))
    for d in range(tk // tq):
        e_ref[2 + d] = -slope * jnp.abs(rel + float(d * tq))
    s_ref[...] = jnp.zeros_like(s_ref)
    acc_ref[...] = jnp.zeros_like(acc_ref)
    l_ref[...] = jnp.ones_like(l_ref)
    _attn_plan(tq, tk, nk, nq, bi, hi, 0, slope, kn_ref, qn_ref, plan_i_ref, plan_f_ref)
    pre_ref[0] = 0

    def q_tile(qi, carry):
        _attn_q_tile(tq, tk, nk, nq, bi, hi, qi, slope, kn_ref, qn_ref, qT_ref, k_ref, vT_ref, lq1_ref, lk1_ref,
                     lq2_ref, lk2_ref, sg_ref, gate_ref, o_ref, e_ref, s_ref, s1_ref, acc_ref, l_ref, plan_i_ref, plan_f_ref,
                     pre_ref)
        return carry

    lax.fori_loop(0, nq, q_tile, 0)
    _attn_epilogue(tq, nq - 1, lq1_ref, lk1_ref, lq2_ref, lk2_ref, sg_ref, gate_ref, o_ref, acc_ref, l_ref)


def _diff_attn(qT, k3, vT, kn, qn, lq1, lk1, lq2, lk2, subln_g, gate3, tq=ATTN_TQ, tk=ATTN_TK):
    b, s, _ = k3.shape
    nq, nk = s // tq, s // tk
    ratio = tk // tq
    assert tk % tq == 0 and nk % 2 == 0
    hw = 2 * ATTN_HEAD_DIM
    slopes = jnp.exp2(-8.0 * (jnp.arange(ATTN_HEADS, dtype=F32) + 1.0) / ATTN_HEADS) * LOG2E
    kn_tab = jnp.transpose(kn[:, :, 0].reshape(b, nk, N_MAPS), (0, 2, 1)).reshape(-1)
    kparts = tk // ATTN_QPART
    qn_tab = jnp.transpose(qn[:, :, :kparts].reshape(b, nk, N_MAPS, kparts), (0, 2, 1, 3)).reshape(-1)
    vec = lambda a: a.reshape(1, -1)
    return pl.pallas_call(
        functools.partial(_attn_kernel, tq, tk, nk, nq),
        grid=(b, ATTN_HEADS),
        in_specs=[
            pl.BlockSpec(memory_space=pltpu.SMEM),
            pl.BlockSpec(memory_space=pltpu.SMEM),
            pl.BlockSpec(memory_space=pltpu.SMEM),
            pl.BlockSpec((hw, s), lambda bi, hi: (hi, bi)),
            pl.BlockSpec((None, s, hw), lambda bi, hi: (bi, 0, hi)),
            pl.BlockSpec((ATTN_V_DIM, s), lambda bi, hi: (hi, bi)),
            _const_spec((1, ATTN_HEAD_DIM)), _const_spec((1, ATTN_HEAD_DIM)),
            _const_spec((1, ATTN_HEAD_DIM)), _const_spec((1, ATTN_HEAD_DIM)),
            _const_spec((ATTN_V_DIM, 1)),
            pl.BlockSpec((None, s, ATTN_V_DIM), lambda bi, hi: (bi, 0, hi)),
        ],
        out_specs=pl.BlockSpec((None, s, ATTN_V_DIM), lambda bi, hi: (bi, 0, hi)),
        out_shape=jax.ShapeDtypeStruct((b, s, ATTN_HEADS * ATTN_V_DIM), BF16),
        scratch_shapes=[pltpu.VMEM((2 + tk // tq, tk, tq), F32),
                        pltpu.VMEM((2, tk, tq), F32), pltpu.VMEM((2, tk, tq), F32),
                        pltpu.VMEM((2, ATTN_V_DIM, tq), F32), pltpu.VMEM((2, 1, tq), F32),
                        pltpu.SMEM((4,), jnp.int32), pltpu.SMEM((1,), F32), pltpu.SMEM((1,), jnp.int32)],
        compiler_params=pltpu.CompilerParams(dimension_semantics=("parallel", "parallel"),
                                             vmem_limit_bytes=VMEM_LIMIT),
        name="diff_attn",
    )(slopes, kn_tab, qn_tab, qT, k3, vT, vec(lq1), vec(lk1), vec(lq2), vec(lk2), subln_g.reshape(-1, 1), gate3)


def _out_proj_kernel(x_ref, eg_ref, eb_ref, ys_ref, ya_ref, w_ref, g_ref, b_ref, out_ref):
    h = _layer_norm(x_ref[...], eg_ref[...], eb_ref[...])
    mix = _dot(ys_ref[...], w_ref[:D_SSM, :]) + _dot(ya_ref[...], w_ref[D_SSM:, :])
    out_ref[...] = _layer_norm(ALPHA * h + mix, g_ref[...], b_ref[...])


def _out_proj(x2d, ln_emb_g, ln_emb_b, y_ssm, y_attn, w_out, ln_g, ln_b, tm=512):
    t = x2d.shape[0]
    row = lambda n: pl.BlockSpec((tm, n), lambda i: (i, 0))
    vec = lambda a: a.reshape(1, -1)
    return pl.pallas_call(
        _out_proj_kernel,
        grid=(t // tm,),
        in_specs=[row(D_MODEL), _const_spec((1, D_MODEL)), _const_spec((1, D_MODEL)),
                  row(D_SSM), row(D_ATTN),
                  _const_spec((D_SSM + D_ATTN, D_MODEL)), _const_spec((1, D_MODEL)), _const_spec((1, D_MODEL))],
        out_specs=row(D_MODEL),
        out_shape=jax.ShapeDtypeStruct((t, D_MODEL), F32),
        compiler_params=pltpu.CompilerParams(dimension_semantics=("parallel",), vmem_limit_bytes=VMEM_LIMIT),
        name="out_proj",
    )(x2d, vec(ln_emb_g), vec(ln_emb_b), y_ssm, y_attn, w_out.astype(BF16), vec(ln_g), vec(ln_b))


def kernel(x, ln_emb_g, ln_emb_b, w_in, conv_w, conv_b, A_log_fwd, A_log_bwd, dt_bias_fwd, dt_bias_bwd, D_skip,
           ssm_norm_g, lambda_q1, lambda_k1, lambda_q2, lambda_k2, subln_g, w_out, ln_g, ln_b):
    b, s, _ = x.shape
    t = b * s
    x2d = x.reshape(t, D_MODEL)
    z, xbc, dtT, qT, k, vT, gate, kn, qn = _ln_inproj(x2d, ln_emb_g, ln_emb_b, w_in[0])

    xs, bt, cm = _conv_silu(xbc.reshape(b, s, D_XBC), conv_w[0], conv_b[0])
    y_fwd = _ssd_pass(False, xs, bt, cm, dtT, A_log_fwd[0], dt_bias_fwd[0], (D_skip[0],))
    y_ssm = _ssd_pass(True, xs, bt, cm, dtT, A_log_bwd[0], dt_bias_bwd[0],
                      (y_fwd, z.reshape(b, s, D_SSM), ssm_norm_g[0]))

    y_attn = _diff_attn(qT, k.reshape(b, s, D_ATTN), vT, kn, qn, lambda_q1[0], lambda_k1[0], lambda_q2[0],
                        lambda_k2[0], subln_g[0], gate.reshape(b, s, D_ATTN)).reshape(t, D_ATTN)

    out = _out_proj(x2d, ln_emb_g, ln_emb_b, y_ssm.reshape(t, D_SSM), y_attn, w_out[0], ln_g[0], ln_b[0])
    return out.reshape(b, s, D_MODEL)
```
